```python
import math
import jax, jax.numpy as jnp
from jax import lax
import numpy as np

D_MODEL = 1024
BATCH = 8
SEQ = 2048
DEPTH = 4
DEC_BATCH = 128
DEC_SEQ = 8
PAST_LEN = 8192
PAGE_SIZE = 128

F32 = jnp.float32
EPS = 1e-6
N_MIXERS = 3
N_LAYERS_A = (DEPTH + 2) // 3
N_LAYERS_B = (DEPTH + 1) // 3
N_LAYERS_C = DEPTH // 3
CONV_W = 4
A_HEADS = 4
A_DK = D_MODEL // 8
A_DV = D_MODEL // A_HEADS
A_QK = A_HEADS * A_DK
A_V = A_HEADS * A_DV
A_PROJ = 2 * A_QK + 2 * A_V + 2 * A_HEADS
A_CHUNK = 64
B_INNER = 2 * D_MODEL
B_HEADDIM = 64
B_HEADS = B_INNER // B_HEADDIM
B_STATE = 128
B_GROUPS = 8
B_GN = B_GROUPS * B_STATE
B_CONV_DIM = B_INNER + 2 * B_GN
B_PROJ = B_INNER + B_CONV_DIM + B_HEADS
B_CHUNK = 64
C_HEADS = 16
C_KV_HEADS = 4
C_GROUP = C_HEADS // C_KV_HEADS
C_HD = 64
C_PROJ = (C_HEADS + 2 * C_KV_HEADS) * C_HD
WINDOW = 128
SWA_BUF = min(WINDOW, PAST_LEN)
ROPE_THETA = 10000.0
N_EXPERTS = 32
TOP_K = 4
D_FF = D_MODEL
SWIGLU_LIMIT = 7.0
SWIGLU_ALPHA = 1.702

kernel_name = 'hybrid_mlstm_ssd_swa_moe_step'


def rms_norm(x):
    xf = x.astype(F32)
    return (xf * lax.rsqrt(jnp.mean(xf * xf, axis=-1, keepdims=True) + EPS)).astype(x.dtype)


def rope(x, pos):
    half = x.shape[-1] // 2
    inv = ROPE_THETA ** (-jnp.arange(half, dtype=F32) / half)
    ang = pos.astype(F32)[:, None] * inv[None, :]
    cos = jnp.cos(ang)[None, :, None, :]
    sin = jnp.sin(ang)[None, :, None, :]
    x1, x2 = x[..., :half], x[..., half:]
    return jnp.concatenate([x1 * cos - x2 * sin, x2 * cos + x1 * sin], axis=-1)


def causal_conv_silu(x, buf, w, b):
    T = x.shape[1]
    xp = jnp.concatenate([buf.astype(x.dtype), x], axis=1)
    y = b
    for i in range(CONV_W):
        y = y + xp[:, i:i + T] * w[i]
    return jax.nn.silu(y), xp[:, T:]


def _to_chunks(x, L):
    B, T = x.shape[:2]
    return jnp.moveaxis(x.reshape((B, T // L, L) + x.shape[2:]), 1, 0)


def _from_chunks(y):
    y = jnp.moveaxis(y, 0, 1)
    return y.reshape((y.shape[0], y.shape[1] * y.shape[2]) + y.shape[3:])


def mlstm_chunk(carry, inp):
    C0, n0, m0 = carry
    q, k, v, log_i, log_f = inp
    L = q.shape[1]
    a = jnp.transpose(jnp.cumsum(log_f, axis=1), (0, 2, 1))
    li = jnp.transpose(log_i, (0, 2, 1))
    causal = jnp.tril(jnp.ones((L, L), bool))
    dmat = jnp.where(causal, a[..., :, None] - a[..., None, :] + li[..., None, :], -jnp.inf)
    inter = a + m0[..., None]
    m = jnp.maximum(inter, jnp.max(dmat, axis=-1))
    s = jnp.einsum('blhd,bshd->bhls', q, k) * jnp.exp(dmat - m[..., None])
    w_inter = jnp.transpose(jnp.exp(inter - m), (0, 2, 1))
    num = jnp.einsum('bhls,bshv->blhv', s, v) + w_inter[..., None] * jnp.einsum('blhd,bhdv->blhv', q, C0)
    den = jnp.transpose(jnp.sum(s, axis=-1), (0, 2, 1)) + w_inter * jnp.einsum('blhd,bhd->blh', q, n0)
    m_t = jnp.transpose(m, (0, 2, 1))
    h = num / jnp.maximum(jnp.abs(den), jnp.exp(-m_t))[..., None]
    a_end = a[..., -1]
    src = a_end[..., None] - a + li
    m_end = jnp.maximum(a_end + m0, jnp.max(src, axis=-1))
    w_src = jnp.exp(src - m_end[..., None])
    w_old = jnp.exp(a_end + m0 - m_end)
    C1 = w_old[..., None, None] * C0 + jnp.einsum('bhs,bshd,bshv->bhdv', w_src, k, v)
    n1 = w_old[..., None] * n0 + jnp.einsum('bhs,bshd->bhd', w_src, k)
    return (C1, n1, m_end), h


def mlstm_mixer(h, conv_buf, C0, n0, m0, w_in, conv_w, conv_b, gate_b, norm_g, w_out):
    B, T, _ = h.shape
    p = h @ w_in
    qk, conv_new = causal_conv_silu(p[..., :2 * A_QK], conv_buf, conv_w, conv_b)
    q = qk[..., :A_QK].reshape(B, T, A_HEADS, A_DK).astype(F32) * (A_DK ** -0.5)
    k = qk[..., A_QK:].reshape(B, T, A_HEADS, A_DK).astype(F32)
    v = p[..., 2 * A_QK:2 * A_QK + A_V].reshape(B, T, A_HEADS, A_DV).astype(F32)
    o = jax.nn.sigmoid(p[..., 2 * A_QK + A_V:2 * A_QK + 2 * A_V].astype(F32)).reshape(B, T, A_HEADS, A_DV)
    gates = p[..., 2 * A_QK + 2 * A_V:].astype(F32) + gate_b.astype(F32)
    log_i = gates[..., :A_HEADS]
    log_f = jax.nn.log_sigmoid(gates[..., A_HEADS:])
    L = math.gcd(T, A_CHUNK)
    (C1, n1, m1), hs = lax.scan(mlstm_chunk, (C0.astype(F32), n0.astype(F32), m0.astype(F32)),
                                tuple(_to_chunks(t, L) for t in (q, k, v, log_i, log_f)))
    hs = rms_norm(_from_chunks(hs)) * norm_g.astype(F32)
    out = (o * hs).reshape(B, T, A_V).astype(h.dtype) @ w_out
    return out, conv_new, C1.astype(C0.dtype), n1.astype(n0.dtype), m1.astype(m0.dtype)


def ssd_chunk(h0, inp):
    x, Bm, Cm, dt, dA = inp
    bsz, L = x.shape[:2]
    G, E = B_GROUPS, B_HEADS // B_GROUPS
    a = jnp.cumsum(dA, axis=1).reshape(bsz, L, G, E)
    seg = a[:, :, None] - a[:, None, :]
    causal = jnp.tril(jnp.ones((L, L), bool))[None, :, :, None, None]
    decay = jnp.exp(jnp.where(causal, seg, -jnp.inf))
    xdt = (x * dt[..., None]).reshape(bsz, L, G, E, B_HEADDIM)
    cb = jnp.einsum('blgn,bsgn->blsg', Cm, Bm)
    y = jnp.einsum('blsge,bsgep->blgep', cb[..., None] * decay, xdt)
    h0g = h0.reshape(bsz, G, E, B_HEADDIM, B_STATE)
    y = y + jnp.exp(a)[..., None] * jnp.einsum('blgn,bgepn->blgep', Cm, h0g)
    a_end = a[:, -1]
    w_end = jnp.exp(a_end[:, None] - a)
    h1 = jnp.exp(a_end)[..., None, None] * h0g + jnp.einsum('bsge,bsgep,bsgn->bgepn', w_end, xdt, Bm)
    return h1.reshape(bsz, B_HEADS, B_HEADDIM, B_STATE), y.reshape(bsz, L, B_HEADS, B_HEADDIM)


def ssd_mixer(h, conv_buf, s0, w_in, conv_w, conv_b, dt_bias, a_log, d_skip, norm_g, w_out):
    B, T, _ = h.shape
    p = h @ w_in
    z = p[..., :B_INNER]
    xbc, conv_new = causal_conv_silu(p[..., B_INNER:B_INNER + B_CONV_DIM], conv_buf, conv_w, conv_b)
    dt_raw = p[..., B_INNER + B_CONV_DIM:]
    x = xbc[..., :B_INNER].astype(F32).reshape(B, T, B_HEADS, B_HEADDIM)
    Bm = xbc[..., B_INNER:B_INNER + B_GN].astype(F32).reshape(B, T, B_GROUPS, B_STATE)
    Cm = xbc[..., B_INNER + B_GN:].astype(F32).reshape(B, T, B_GROUPS, B_STATE)
    dt = jax.nn.softplus(dt_raw.astype(F32) + dt_bias.astype(F32))
    dA = dt * (-jnp.exp(a_log.astype(F32)))
    L = math.gcd(T, B_CHUNK)
    s1, y = lax.scan(ssd_chunk, s0.astype(F32), tuple(_to_chunks(t, L) for t in (x, Bm, Cm, dt, dA)))
    y = _from_chunks(y) + d_skip.astype(F32)[:, None] * x
    y = y.reshape(B, T, B_INNER) * jax.nn.silu(z.astype(F32))
    y = rms_norm(y.reshape(B, T, B_GROUPS, B_INNER // B_GROUPS)) * norm_g.astype(F32).reshape(B_GROUPS, -1)
    out = y.reshape(B, T, B_INNER).astype(h.dtype) @ w_out
    return out, conv_new, s1.astype(s0.dtype)


def swa_mixer(h, pos0, k_buf, v_buf, w_in, q_g, k_g, sinks, w_out):
    B, T, _ = h.shape
    p = h @ w_in
    q = p[..., :C_HEADS * C_HD].reshape(B, T, C_HEADS, C_HD)
    k = p[..., C_HEADS * C_HD:(C_HEADS + C_KV_HEADS) * C_HD].reshape(B, T, C_KV_HEADS, C_HD)
    v = p[..., (C_HEADS + C_KV_HEADS) * C_HD:].reshape(B, T, C_KV_HEADS, C_HD)
    pos = pos0 + jnp.arange(T)
    q = rope(rms_norm(q.astype(F32)) * q_g.astype(F32), pos)
    k = rope(rms_norm(k.astype(F32)) * k_g.astype(F32), pos)
    nbuf = k_buf.shape[1]
    k_all = jnp.concatenate([k_buf, k.astype(k_buf.dtype)], axis=1)
    v_all = jnp.concatenate([v_buf, v.astype(v_buf.dtype)], axis=1)
    Q = math.gcd(T, WINDOW)
    nblk = T // Q
    idx = jnp.arange(nblk)[:, None] * Q + jnp.arange(nbuf + Q)[None, :]
    kb = k_all[:, idx].astype(F32)
    vb = v_all[:, idx].astype(F32)
    qb = q.reshape(B, nblk, Q, C_KV_HEADS, C_GROUP, C_HD)
    kpos = pos0 - nbuf + idx
    diff = pos.reshape(nblk, Q)[:, :, None] - kpos[:, None, :]
    mask = (kpos[:, None, :] >= 0) & (diff >= 0) & (diff <= WINDOW)
    logits = jnp.einsum('bnqkgd,bnskd->bnkgqs', qb, kb) * (C_HD ** -0.5)
    logits = jnp.where(mask[None, :, None, None], logits, -jnp.inf)
    sink = sinks.astype(F32).reshape(1, 1, C_KV_HEADS, C_GROUP, 1)
    mx = jnp.maximum(jnp.max(logits, axis=-1), sink)
    pr = jnp.exp(logits - mx[..., None])
    den = jnp.sum(pr, axis=-1) + jnp.exp(sink - mx)
    o = jnp.einsum('bnkgqs,bnskd->bnqkgd', pr, vb) / jnp.transpose(den, (0, 1, 4, 2, 3))[..., None]
    out = o.reshape(B, T, C_HEADS * C_HD).astype(h.dtype) @ w_out
    return out, k_all[:, -nbuf:], v_all[:, -nbuf:]


def moe_ffn(h, w_r, b_r, w1, b1, w2, b2):
    B, T, D = h.shape
    x = h.reshape(B * T, D)
    logits = (x @ w_r).astype(F32) + b_r.astype(F32)
    top_v, top_i = lax.top_k(logits, TOP_K)
    gate = jax.nn.softmax(top_v, axis=-1)
    comb = jnp.sum(jax.nn.one_hot(top_i, N_EXPERTS, dtype=F32) * gate[..., None], axis=1)
    out = jnp.zeros((B * T, D), F32)
    for e in range(N_EXPERTS):
        gu = x @ w1[e] + b1[e]
        g, u = gu[..., :D_FF], gu[..., D_FF:]
        g = jnp.minimum(g, SWIGLU_LIMIT)
        u = jnp.clip(u, -SWIGLU_LIMIT, SWIGLU_LIMIT)
        act = (u + 1) * (g * jax.nn.sigmoid(SWIGLU_ALPHA * g))
        out = out + comb[:, e:e + 1] * (act @ w2[e] + b2[e]).astype(F32)
    return out.astype(h.dtype).reshape(B, T, D)


def run_trunk(x, c, pos0, states, params):
    a_C, a_n, a_m, a_conv, b_ssm, b_conv, c_k, c_v = states
    (ada_w, ada_b,
     mlstm_w_in, mlstm_conv_w, mlstm_conv_b, mlstm_gate_b, mlstm_norm_g, mlstm_w_out,
     ssd_w_in, ssd_conv_w, ssd_conv_b, ssd_dt_bias, ssd_a_log, ssd_d_skip, ssd_norm_g, ssd_w_out,
     swa_w_in, swa_q_norm_g, swa_k_norm_g, swa_sinks, swa_w_out,
     moe_w_router, moe_b_router, moe_w1, moe_b1, moe_w2, moe_b2) = params
    new = [[] for _ in range(8)]
    cs = jax.nn.silu(c)
    for layer in range(DEPTH):
        kind, j = layer % N_MIXERS, layer // N_MIXERS
        mod = (cs @ ada_w[layer] + ada_b[layer])[:, None, :]
        sh1, sc1, g1, sh2, sc2, g2 = jnp.split(mod, 6, axis=-1)
        h = rms_norm(x) * (1 + sc1) + sh1
        if kind == 0:
            out, cv, C1, n1, m1 = mlstm_mixer(h, a_conv[j], a_C[j], a_n[j], a_m[j], mlstm_w_in[j], mlstm_conv_w[j],
                                              mlstm_conv_b[j], mlstm_gate_b[j], mlstm_norm_g[j], mlstm_w_out[j])
            new[0].append(C1)
            new[1].append(n1)
            new[2].append(m1)
            new[3].append(cv)
        elif kind == 1:
            out, cv, s1 = ssd_mixer(h, b_conv[j], b_ssm[j], ssd_w_in[j], ssd_conv_w[j], ssd_conv_b[j], ssd_dt_bias[j],
                                    ssd_a_log[j], ssd_d_skip[j], ssd_norm_g[j], ssd_w_out[j])
            new[4].append(s1)
            new[5].append(cv)
        else:
            out, k1, v1 = swa_mixer(h, pos0, c_k[j], c_v[j], swa_w_in[j], swa_q_norm_g[j], swa_k_norm_g[j],
                                    swa_sinks[j], swa_w_out[j])
            new[6].append(k1)
            new[7].append(v1)
        x = x + g1 * out
        h = rms_norm(x) * (1 + sc2) + sh2
        x = x + g2 * moe_ffn(h, moe_w_router[layer], moe_b_router[layer], moe_w1[layer], moe_b1[layer],
                             moe_w2[layer], moe_b2[layer])
    return x, tuple(jnp.stack(s) for s in new)


def setup_inputs(seed: int = 0) -> dict:
    key = jax.random.key(seed)
    keys = list(jax.random.split(key, 64))

    def nrm(shape, scale=1.0):
        return scale * jax.random.normal(keys.pop(), shape, F32)

    def unif(shape, lo, hi):
        return jax.random.uniform(keys.pop(), shape, F32, lo, hi)

    nA, nB, nC, nL, E = N_LAYERS_A, N_LAYERS_B, N_LAYERS_C, DEPTH, N_EXPERTS
    dt0 = jnp.exp(unif((nB, B_HEADS), math.log(1e-3), math.log(1e-1)))
    return {
        'x_prompt': nrm((BATCH, SEQ, D_MODEL)),
        'x_sample': nrm((DEC_BATCH, DEC_SEQ, D_MODEL)),
        'c_prompt': nrm((BATCH, D_MODEL)),
        'c_sample': nrm((DEC_BATCH, D_MODEL)),
        'state_mlstm_C': nrm((nA, DEC_BATCH, A_HEADS, A_DK, A_DV)),
        'state_mlstm_n': nrm((nA, DEC_BATCH, A_HEADS, A_DK)),
        'state_mlstm_m': nrm((nA, DEC_BATCH, A_HEADS)),
        'state_mlstm_conv': nrm((nA, DEC_BATCH, CONV_W - 1, 2 * A_QK)),
        'state_ssm': nrm((nB, DEC_BATCH, B_HEADS, B_HEADDIM, B_STATE), 0.5),
        'state_ssm_conv': nrm((nB, DEC_BATCH, CONV_W - 1, B_CONV_DIM)),
        'cache_swa_k': nrm((nC, DEC_BATCH, SWA_BUF, C_KV_HEADS, C_HD)),
        'cache_swa_v': nrm((nC, DEC_BATCH, SWA_BUF, C_KV_HEADS, C_HD)),
        'ada_w': nrm((nL, D_MODEL, 6 * D_MODEL), 0.5 * D_MODEL ** -0.5),
        'ada_b': nrm((nL, 6 * D_MODEL), 0.02),
        'mlstm_w_in': nrm((nA, D_MODEL, A_PROJ), D_MODEL ** -0.5),
        'mlstm_conv_w': nrm((nA, CONV_W, 2 * A_QK), CONV_W ** -0.5),
        'mlstm_conv_b': nrm((nA, 2 * A_QK), 0.02),
        'mlstm_gate_b': jnp.concatenate([nrm((nA, A_HEADS), 0.1), 3.0 + nrm((nA, A_HEADS), 0.5)], axis=-1),
        'mlstm_norm_g': 1.0 + nrm((nA, A_HEADS, A_DV), 0.02),
        'mlstm_w_out': nrm((nA, A_V, D_MODEL), A_V ** -0.5),
        'ssd_w_in': nrm((nB, D_MODEL, B_PROJ), D_MODEL ** -0.5),
        'ssd_conv_w': nrm((nB, CONV_W, B_CONV_DIM), CONV_W ** -0.5),
        'ssd_conv_b': nrm((nB, B_CONV_DIM), 0.02),
        'ssd_dt_bias': dt0 + jnp.log(-jnp.expm1(-dt0)),
        'ssd_a_log': jnp.log(unif((nB, B_HEADS), 1.0, 16.0)),
        'ssd_d_skip': 1.0 + nrm((nB, B_HEADS), 0.1),
        'ssd_norm_g': 1.0 + nrm((nB, B_INNER), 0.02),
        'ssd_w_out': nrm((nB, B_INNER, D_MODEL), B_INNER ** -0.5),
        'swa_w_in': nrm((nC, D_MODEL, C_PROJ), D_MODEL ** -0.5),
        'swa_q_norm_g': 1.0 + nrm((nC, C_HD), 0.02),
        'swa_k_norm_g': 1.0 + nrm((nC, C_HD), 0.02),
        'swa_sinks': nrm((nC, C_HEADS), 0.5),
        'swa_w_out': nrm((nC, C_HEADS * C_HD, D_MODEL), (C_HEADS * C_HD) ** -0.5),
        'moe_w_router': nrm((nL, D_MODEL, E), D_MODEL ** -0.5),
        'moe_b_router': nrm((nL, E), 0.01),
        'moe_w1': nrm((nL, E, D_MODEL, 2 * D_FF), D_MODEL ** -0.5),
        'moe_b1': nrm((nL, E, 2 * D_FF), 0.01),
        'moe_w2': nrm((nL, E, D_FF, D_MODEL), D_FF ** -0.5),
        'moe_b2': nrm((nL, E, D_MODEL), 0.01),
    }


def reference(x_prompt, x_sample, c_prompt, c_sample,
              state_mlstm_C, state_mlstm_n, state_mlstm_m, state_mlstm_conv,
              state_ssm, state_ssm_conv, cache_swa_k, cache_swa_v,
              ada_w, ada_b,
              mlstm_w_in, mlstm_conv_w, mlstm_conv_b, mlstm_gate_b, mlstm_norm_g, mlstm_w_out,
              ssd_w_in, ssd_conv_w, ssd_conv_b, ssd_dt_bias, ssd_a_log, ssd_d_skip, ssd_norm_g, ssd_w_out,
              swa_w_in, swa_q_norm_g, swa_k_norm_g, swa_sinks, swa_w_out,
              moe_w_router, moe_b_router, moe_w1, moe_b1, moe_w2, moe_b2):
    params = (ada_w, ada_b,
              mlstm_w_in, mlstm_conv_w, mlstm_conv_b, mlstm_gate_b, mlstm_norm_g, mlstm_w_out,
              ssd_w_in, ssd_conv_w, ssd_conv_b, ssd_dt_bias, ssd_a_log, ssd_d_skip, ssd_norm_g, ssd_w_out,
              swa_w_in, swa_q_norm_g, swa_k_norm_g, swa_sinks, swa_w_out,
              moe_w_router, moe_b_router, moe_w1, moe_b1, moe_w2, moe_b2)
    sample_states = (state_mlstm_C, state_mlstm_n, state_mlstm_m, state_mlstm_conv,
                     state_ssm, state_ssm_conv, cache_swa_k, cache_swa_v)
    bp = x_prompt.shape[0]
    prompt_states = tuple(jnp.zeros((s.shape[0], bp) + s.shape[2:], x_prompt.dtype) for s in sample_states)
    y_prompt, (pC, pn, pm, pac, pss, pbc, pk, pv) = run_trunk(x_prompt, c_prompt, 0, prompt_states, params)
    y_sample, (sC, sn, sm, sac, sss, sbc, sk, sv) = run_trunk(x_sample, c_sample, PAST_LEN, sample_states, params)
    return (y_prompt, y_sample, pC, sC, pn, sn, pm, sm, pac, sac, pss, sss, pbc, sbc, pk, sk, pv, sv)
```

```python
import functools
import math

import jax
import jax.numpy as jnp
from jax import lax
from jax.experimental import pallas as pl
from jax.experimental.pallas import tpu as pltpu

F32 = jnp.float32
BF16 = jnp.bfloat16
I32 = jnp.int32

D_MODEL = 1024
DEPTH = 4
PAST_LEN = 8192
EPS = 1e-6
CONV_W = 4
A_HEADS = 4
A_DK = D_MODEL // 8
A_DV = D_MODEL // A_HEADS
A_QK = A_HEADS * A_DK
A_V = A_HEADS * A_DV
B_INNER = 2 * D_MODEL
B_HEADDIM = 64
B_HEADS = B_INNER // B_HEADDIM
B_STATE = 128
B_GROUPS = 8
B_GN = B_GROUPS * B_STATE
B_CONV_DIM = B_INNER + 2 * B_GN
B_HPG = B_HEADS // B_GROUPS
B_GW = B_HPG * B_HEADDIM
C_HEADS = 16
C_KV_HEADS = 4
C_GROUP = C_HEADS // C_KV_HEADS
C_HD = 64
WINDOW = 128
ROPE_THETA = 10000.0
N_EXPERTS = 32
TOP_K = 4
D_FF = D_MODEL
SWIGLU_LIMIT = 7.0
SWIGLU_ALPHA = 1.702

LANES = 128
SUBLANES = 8
VMEM_LIMIT_BYTES = 56 * 1024 * 1024

ROW_TILE = 256
SEQ_CHUNK = 256
SHORT_BLOCK = 128
MOE_TILE = 256
NEG_INF = float("-inf")


def _cparams(*sem):
    return pltpu.CompilerParams(dimension_semantics=sem, vmem_limit_bytes=VMEM_LIMIT_BYTES)


def _silu(x):
    return x * jax.nn.sigmoid(x)


def _dot(a, b):
    return jnp.dot(a, b, preferred_element_type=F32)


def _dot_nt(a, b):
    return lax.dot_general(a, b, (((1,), (1,)), ((), ())), preferred_element_type=F32)


class Group:
    def __init__(self, B, T):
        self.B, self.T = B, T
        self.N = B * T
        self.long = T % SEQ_CHUNK == 0
        if self.long:
            self.S, self.L = 1, SEQ_CHUNK
        else:
            assert T == SUBLANES and self.N % SHORT_BLOCK == 0
            self.S, self.L = SHORT_BLOCK // T, T
        self.R = self.S * self.L
        self.n_blocks = B // self.S
        self.n_chunks = T // self.L
        self.tm = min(ROW_TILE, self.N)
        assert self.N % self.tm == 0 and (not self.long or T % self.tm == 0)

    def expand_mod(self, mod):
        if self.long:
            return mod.reshape(self.B, 1, 6 * D_MODEL)
        return jnp.repeat(mod, self.T, axis=0)

    def mod_spec(self, j, rows, row_block_fn):
        if self.long:
            per_seq = self.T // rows
            return pl.BlockSpec((None, 1, D_MODEL), lambda *g: (row_block_fn(*g) // per_seq, 0, j))
        return pl.BlockSpec((rows, D_MODEL), lambda *g: (row_block_fn(*g), j))


def _ada_kernel(c_ref, w_ref, b_ref, o_ref):
    cs = _silu(c_ref[...])
    o_ref[...] = _dot(cs.astype(BF16), w_ref[...].astype(BF16)) + b_ref[...]


def ada_modulation(c_all, ada_w, ada_b):
    rows = c_all.shape[0]
    tn = 1536
    n_out = 6 * D_MODEL
    return pl.pallas_call(
        _ada_kernel,
        grid=(DEPTH, n_out // tn),
        in_specs=[
            pl.BlockSpec((rows, D_MODEL), lambda l, j: (0, 0)),
            pl.BlockSpec((None, D_MODEL, tn), lambda l, j: (l, 0, j)),
            pl.BlockSpec((None, 1, tn), lambda l, j: (l, 0, j)),
        ],
        out_specs=pl.BlockSpec((None, rows, tn), lambda l, j: (l, 0, j)),
        out_shape=jax.ShapeDtypeStruct((DEPTH, rows, n_out), F32),
        compiler_params=_cparams("arbitrary", "arbitrary"),
        name="ada_modulation",
    )(c_all, ada_w, ada_b.reshape(DEPTH, 1, n_out))


def _modulated_norm(x, sc, sh):
    ms = jnp.mean(x * x, axis=-1, keepdims=True)
    return (x * lax.rsqrt(ms + EPS)) * (1.0 + sc) + sh


def _norm_mm_kernel(x_ref, sc_ref, sh_ref, *refs, n_w):
    h = _modulated_norm(x_ref[...], sc_ref[...], sh_ref[...]).astype(BF16)
    for w_ref, o_ref in zip(refs[:n_w], refs[n_w:]):
        o_ref[...] = _dot(h, w_ref[...])


def norm_matmul(grp, x, mod, j_scale, j_shift, weights):
    tm = grp.tm
    n_w = len(weights)
    in_specs = [
        pl.BlockSpec((tm, D_MODEL), lambda i: (i, 0)),
        grp.mod_spec(j_scale, tm, lambda i: i),
        grp.mod_spec(j_shift, tm, lambda i: i),
    ] + [pl.BlockSpec(w.shape, lambda i: (0, 0)) for w in weights]
    out_specs = [pl.BlockSpec((tm, w.shape[1]), lambda i: (i, 0)) for w in weights]
    out_shape = [jax.ShapeDtypeStruct((grp.N, w.shape[1]), F32) for w in weights]
    return pl.pallas_call(
        functools.partial(_norm_mm_kernel, n_w=n_w),
        grid=(grp.N // tm,),
        in_specs=in_specs,
        out_specs=out_specs,
        out_shape=out_shape,
        compiler_params=_cparams("arbitrary"),
        name="norm_matmul",
    )(x, mod, mod, *weights)


def _mm_res_kernel(y_ref, w_ref, x_ref, g_ref, o_ref):
    o_ref[...] = x_ref[...] + g_ref[...] * _dot(y_ref[...].astype(BF16), w_ref[...])


def matmul_residual(grp, y, w, x, mod, j_gate):
    tm = grp.tm
    k = y.shape[1]
    return pl.pallas_call(
        _mm_res_kernel,
        grid=(grp.N // tm,),
        in_specs=[
            pl.BlockSpec((tm, k), lambda i: (i, 0)),
            pl.BlockSpec(w.shape, lambda i: (0, 0)),
            pl.BlockSpec((tm, D_MODEL), lambda i: (i, 0)),
            grp.mod_spec(j_gate, tm, lambda i: i),
        ],
        out_specs=pl.BlockSpec((tm, D_MODEL), lambda i: (i, 0)),
        out_shape=jax.ShapeDtypeStruct((grp.N, D_MODEL), F32),
        compiler_params=_cparams("arbitrary"),
        name="matmul_residual",
    )(y, w, x, mod)


def _router_kernel(x_ref, sc_ref, sh_ref, wr_ref, br_ref, cnt_in_ref,
                   h_ref, ri_ref, rg_ref, cnt_out_ref, cnt_scr):
    i = pl.program_id(0)
    tm = x_ref.shape[0]

    @pl.when(i == 0)
    def _():
        cnt_scr[...] = cnt_in_ref[...]

    h = _modulated_norm(x_ref[...], sc_ref[...], sh_ref[...])
    h_ref[...] = h
    logits = jnp.dot(h, wr_ref[...], preferred_element_type=F32,
                     precision=lax.Precision.HIGHEST) + br_ref[...]
    lane = lax.broadcasted_iota(I32, (tm, LANES), 1)
    lane_f = lane.astype(F32)
    work = jnp.where(lane < N_EXPERTS, logits, NEG_INF)
    top_v, top_sel, top_i = [], [], []
    for _ in range(TOP_K):
        mx = jnp.max(work, axis=1, keepdims=True)
        idx = jnp.min(jnp.where(work == mx, lane_f, float(LANES)), axis=1, keepdims=True)
        sel = lane_f == idx
        work = jnp.where(sel, NEG_INF, work)
        top_v.append(mx)
        top_sel.append(sel)
        top_i.append(idx)
    ex = [jnp.exp(v - top_v[0]) for v in top_v]
    inv = 1.0 / (ex[0] + ex[1] + ex[2] + ex[3])
    chosen = jnp.zeros((tm, LANES), F32)
    for sel in top_sel:
        chosen = jnp.where(sel, 1.0, chosen)
    r = lax.broadcasted_iota(I32, (tm, tm), 0)
    c = lax.broadcasted_iota(I32, (tm, tm), 1)
    before = jnp.where(c < r, 1.0, 0.0).astype(BF16)
    rank_all = _dot(before, chosen.astype(BF16)) + cnt_scr[...]
    ri = jnp.zeros((tm, LANES), I32)
    rg = jnp.zeros((tm, LANES), F32)
    for k in range(TOP_K):
        rank_k = jnp.sum(jnp.where(top_sel[k], rank_all, 0.0), axis=1, keepdims=True)
        ri = jnp.where(lane == k, top_i[k].astype(I32), ri)
        ri = jnp.where(lane == TOP_K + k, rank_k.astype(I32), ri)
        rg = jnp.where(lane == k, ex[k] * inv, rg)
    ri_ref[...] = ri
    rg_ref[...] = rg
    cnt_scr[...] = cnt_scr[...] + jnp.sum(chosen, axis=0, keepdims=True)
    cnt_out_ref[...] = cnt_scr[...]


def moe_router(grp, x, mod, w_r, b_r, cnt_in):
    tm = grp.tm
    return pl.pallas_call(
        _router_kernel,
        grid=(grp.N // tm,),
        in_specs=[
            pl.BlockSpec((tm, D_MODEL), lambda i: (i, 0)),
            grp.mod_spec(4, tm, lambda i: i),
            grp.mod_spec(3, tm, lambda i: i),
            pl.BlockSpec((D_MODEL, LANES), lambda i: (0, 0)),
            pl.BlockSpec((1, LANES), lambda i: (0, 0)),
            pl.BlockSpec((1, LANES), lambda i: (0, 0)),
        ],
        out_specs=[
            pl.BlockSpec((tm, D_MODEL), lambda i: (i, 0)),
            pl.BlockSpec((tm, LANES), lambda i: (i, 0)),
            pl.BlockSpec((tm, LANES), lambda i: (i, 0)),
            pl.BlockSpec((1, LANES), lambda i: (0, 0)),
        ],
        out_shape=[
            jax.ShapeDtypeStruct((grp.N, D_MODEL), F32),
            jax.ShapeDtypeStruct((grp.N, LANES), I32),
            jax.ShapeDtypeStruct((grp.N, LANES), F32),
            jax.ShapeDtypeStruct((1, LANES), F32),
        ],
        scratch_shapes=[pltpu.VMEM((1, LANES), F32)],
        compiler_params=_cparams("arbitrary"),
        name="moe_router",
    )(x, mod, mod, w_r, b_r, cnt_in)


def _row_copy(src_ref, src_row, dst_ref, dst_row, sem):
    return pltpu.make_async_copy(src_ref.at[pl.ds(src_row, 1)], dst_ref.at[pl.ds(dst_row, 1)], sem)


def _dispatch_kernel(pos_ref, h_ref, *refs, aliased):
    xs_ref, sem = refs[-2], refs[-1]
    i = pl.program_id(0)
    tm = h_ref.shape[0]
    base = i * (tm * TOP_K)

    def start(t, carry):
        for k in range(TOP_K):
            _row_copy(h_ref, t, xs_ref, pos_ref[base + t * TOP_K + k], sem).start()
        return carry

    lax.fori_loop(0, tm, start, 0)

    def wait(t, carry):
        for k in range(TOP_K):
            _row_copy(h_ref, 0, xs_ref, 0, sem).wait()
        return carry

    lax.fori_loop(0, tm, wait, 0)


def moe_dispatch(grp, h, pos, xs_prev, n_sorted):
    tm = grp.tm
    aliased = xs_prev is not None
    in_specs = [pl.BlockSpec((tm, D_MODEL), lambda i, p: (i, 0))]
    args = [pos, h]
    if aliased:
        in_specs.append(pl.BlockSpec(memory_space=pl.ANY))
        args.append(xs_prev)
    return pl.pallas_call(
        functools.partial(_dispatch_kernel, aliased=aliased),
        grid_spec=pltpu.PrefetchScalarGridSpec(
            num_scalar_prefetch=1,
            grid=(grp.N // tm,),
            in_specs=in_specs,
            out_specs=pl.BlockSpec(memory_space=pl.ANY),
            scratch_shapes=[pltpu.SemaphoreType.DMA],
        ),
        out_shape=jax.ShapeDtypeStruct((n_sorted, D_MODEL), F32),
        input_output_aliases={2: 0} if aliased else {},
        compiler_params=_cparams("arbitrary"),
        name="moe_dispatch",
    )(*args)


def _padfill_kernel(cnt_ref, off_ref, xs_in_ref, xs_ref, zero_scr, sem):
    zero_scr[...] = jnp.zeros(zero_scr.shape, F32)
    for e in range(N_EXPERTS):
        n = cnt_ref[e]
        first = off_ref[e] + n
        pad = (MOE_TILE - (n & (MOE_TILE - 1))) & (MOE_TILE - 1)

        def start(r, carry, first=first):
            _row_copy(zero_scr, 0, xs_ref, first + r, sem).start()
            return carry

        lax.fori_loop(0, pad, start, 0)

        def wait(r, carry):
            _row_copy(zero_scr, 0, xs_ref, 0, sem).wait()
            return carry

        lax.fori_loop(0, pad, wait, 0)


def moe_padfill(xs, cnt, off):
    return pl.pallas_call(
        _padfill_kernel,
        grid_spec=pltpu.PrefetchScalarGridSpec(
            num_scalar_prefetch=2,
            grid=(1,),
            in_specs=[pl.BlockSpec(memory_space=pl.ANY)],
            out_specs=pl.BlockSpec(memory_space=pl.ANY),
            scratch_shapes=[pltpu.VMEM((SUBLANES, D_MODEL), F32), pltpu.SemaphoreType.DMA],
        ),
        out_shape=jax.ShapeDtypeStruct(xs.shape, F32),
        input_output_aliases={2: 0},
        compiler_params=_cparams("arbitrary"),
        name="moe_padfill",
    )(cnt, off, xs)


def _combine_kernel(pos_ref, ys_ref, rg_ref, x_ref, g_ref, o_ref, ybuf, sem):
    i = pl.program_id(0)
    tm = x_ref.shape[0]
    base = i * (tm * TOP_K)

    def start(t, carry):
        for k in range(TOP_K):
            _row_copy(ys_ref, pos_ref[base + t * TOP_K + k], ybuf.at[k], t, sem).start()
        return carry

    lax.fori_loop(0, tm, start, 0)

    def wait(t, carry):
        for k in range(TOP_K):
            _row_copy(ys_ref, 0, ybuf.at[k], 0, sem).wait()
        return carry

    lax.fori_loop(0, tm, wait, 0)
    rg = rg_ref[...]
    acc = rg[:, 0:1] * ybuf[0]
    for k in range(1, TOP_K):
        acc = acc + rg[:, k:k + 1] * ybuf[k]
    o_ref[...] = x_ref[...] + g_ref[...] * acc


def moe_combine(grp, ys, pos, route_g, x, mod):
    tm = grp.tm
    return pl.pallas_call(
        _combine_kernel,
        grid_spec=pltpu.PrefetchScalarGridSpec(
            num_scalar_prefetch=1,
            grid=(grp.N // tm,),
            in_specs=[
                pl.BlockSpec(memory_space=pl.ANY),
                pl.BlockSpec((tm, LANES), lambda i, p: (i, 0)),
                pl.BlockSpec((tm, D_MODEL), lambda i, p: (i, 0)),
                grp.mod_spec(5, tm, lambda i, p: i),
            ],
            out_specs=pl.BlockSpec((tm, D_MODEL), lambda i, p: (i, 0)),
            scratch_shapes=[pltpu.VMEM((TOP_K, tm, D_MODEL), F32), pltpu.SemaphoreType.DMA],
        ),
        out_shape=jax.ShapeDtypeStruct((grp.N, D_MODEL), F32),
        compiler_params=_cparams("arbitrary"),
        name="moe_combine",
    )(pos, ys, route_g, x, mod)


def _experts_kernel(te_ref, tb_ref, nv_ref, x_ref, w1_ref, b1_ref, w2_ref, b2_ref, y_ref, w1_scr, w2_scr):
    i = pl.program_id(0)

    @pl.when(i < nv_ref[0])
    def _():
        prev = te_ref[jnp.maximum(i - 1, 0)]

        @pl.when((i == 0) | (te_ref[i] != prev))
        def _():
            w1_scr[...] = w1_ref[...].astype(BF16)
            w2_scr[...] = w2_ref[...].astype(BF16)

        gu = _dot(x_ref[...].astype(BF16), w1_scr[...]) + b1_ref[...]
        g = jnp.minimum(gu[:, :D_FF], SWIGLU_LIMIT)
        u = jnp.clip(gu[:, D_FF:], -SWIGLU_LIMIT, SWIGLU_LIMIT)
        act = (u + 1.0) * (g * jax.nn.sigmoid(SWIGLU_ALPHA * g))
        y_ref[...] = _dot(act.astype(BF16), w2_scr[...]) + b2_ref[...]


def moe_experts(xs, tile_expert, tile_block, n_valid, w1, b1, w2, b2, layer):
    n_tiles = tile_expert.shape[0]
    tm = MOE_TILE
    return pl.pallas_call(
        _experts_kernel,
        grid_spec=pltpu.PrefetchScalarGridSpec(
            num_scalar_prefetch=3,
            grid=(n_tiles,),
            in_specs=[
                pl.BlockSpec((tm, D_MODEL), lambda i, te, tb, nv: (tb[i], 0)),
                pl.BlockSpec((None, None, D_MODEL, 2 * D_FF), lambda i, te, tb, nv: (layer, te[i], 0, 0)),
                pl.BlockSpec((None, None, 1, 2 * D_FF), lambda i, te, tb, nv: (layer, te[i], 0, 0)),
                pl.BlockSpec((None, None, D_FF, D_MODEL), lambda i, te, tb, nv: (layer, te[i], 0, 0)),
                pl.BlockSpec((None, None, 1, D_MODEL), lambda i, te, tb, nv: (layer, te[i], 0, 0)),
            ],
            out_specs=pl.BlockSpec((tm, D_MODEL), lambda i, te, tb, nv: (tb[i], 0)),
            scratch_shapes=[pltpu.VMEM((D_MODEL, 2 * D_FF), BF16), pltpu.VMEM((D_FF, D_MODEL), BF16)],
        ),
        out_shape=jax.ShapeDtypeStruct(xs.shape, F32),
        compiler_params=_cparams("arbitrary"),
        name="moe_experts",
    )(tile_expert, tile_block, n_valid, xs, w1, b1.reshape(DEPTH, N_EXPERTS, 1, 2 * D_FF),
      w2, b2.reshape(DEPTH, N_EXPERTS, 1, D_MODEL))


def moe_layer(groups, xs_in, mods, layer, w_router, b_router, w1, b1, w2, b2):
    n_total = sum(g.N for g in groups)
    n_pairs = n_total * TOP_K
    n_tiles = -(-n_pairs // MOE_TILE) + N_EXPERTS
    n_sorted = n_tiles * MOE_TILE
    w_r = jnp.pad(w_router[layer], ((0, 0), (0, LANES - N_EXPERTS)))
    b_r = jnp.pad(b_router[layer], (0, LANES - N_EXPERTS)).reshape(1, LANES)
    cnt = jnp.zeros((1, LANES), F32)
    routed = []
    for grp, x, mod in zip(groups, xs_in, mods):
        h, route_i, route_g, cnt = moe_router(grp, x, mod, w_r, b_r, cnt)
        routed.append((h, route_i, route_g))
    counts = cnt[0, :N_EXPERTS].astype(I32)
    tiles_per = (counts + MOE_TILE - 1) // MOE_TILE
    tile_end = jnp.cumsum(tiles_per)
    tile_start = tile_end - tiles_per
    n_valid = tile_end[-1]
    tidx = jnp.minimum(jnp.arange(n_tiles, dtype=I32), n_valid - 1)
    tile_expert = jnp.searchsorted(tile_end, tidx, side="right").astype(I32)
    row_off = tile_start * MOE_TILE
    positions = []
    xs = None
    for grp, (h, route_i, _) in zip(groups, routed):
        pos = (row_off[route_i[:, :TOP_K]] + route_i[:, TOP_K:2 * TOP_K]).reshape(-1)
        positions.append(pos)
        xs = moe_dispatch(grp, h, pos, xs, n_sorted)
    xs = moe_padfill(xs, counts, row_off)
    ys = moe_experts(xs, tile_expert, tidx, n_valid.reshape(1), w1, b1, w2, b2, layer)
    return [moe_combine(grp, ys, pos, route_g, x, mod)
            for grp, x, mod, pos, (_, _, route_g) in zip(groups, xs_in, mods, positions, routed)]


def _seg_cumsum(x, seg, reverse=False):
    rows = x.shape[0]
    row = lax.broadcasted_iota(I32, x.shape, 0) & (seg - 1)
    sh = 1
    while sh < seg:
        if reverse:
            x = x + jnp.where(row + sh < seg, pltpu.roll(x, rows - sh, axis=0), 0.0)
        else:
            x = x + jnp.where(row >= sh, pltpu.roll(x, sh, axis=0), 0.0)
        sh *= 2
    return x


def _segment_mask(rows, seg):
    ti = lax.broadcasted_iota(I32, (rows, rows), 0)
    si = lax.broadcasted_iota(I32, (rows, rows), 1)
    mask = si <= ti
    if seg < rows:
        shift = seg.bit_length() - 1
        mask = mask & ((ti >> shift) == (si >> shift))
    return mask


def _causal_conv_silu(x, xp_scr, w_ref, b_ref, n_seg, seg, carry):
    rows, ch = x.shape
    xp_scr[:, SUBLANES:, :] = x.reshape(n_seg, seg, ch)
    acc = b_ref[...]
    for i in range(CONV_W):
        back = CONV_W - 1 - i
        xs = x if back == 0 else xp_scr[:, SUBLANES - back:SUBLANES - back + seg, :].reshape(rows, ch)
        acc = acc + xs * w_ref[i:i + 1, :]
    if carry:
        xp_scr[:, SUBLANES - (CONV_W - 1):SUBLANES, :] = xp_scr[:, SUBLANES + seg - (CONV_W - 1):SUBLANES + seg, :]
    return _silu(acc)


def _pad_history(buf):
    return jnp.pad(buf, ((0, 0), (SUBLANES - (CONV_W - 1), 0), (0, 0)))


def _mlstm_kernel(p_ref, gi_ref, gf_ref, hist_ref, c0_ref, n0_ref, m0_ref, cw_ref, cb_ref, gbi_ref, gbf_ref,
                  ng_ref, y_ref, c1_ref, n1_ref, m1_ref, xp_scr, *state_scr, n_seg, seg, n_chunks):
    rows = n_seg * seg
    chunk = pl.program_id(1)
    carried = n_chunks > 1
    if carried:
        c_st, n_st, m_st = state_scr

        @pl.when(chunk == 0)
        def _():
            c_st[...] = c0_ref[...]
            n_st[...] = n0_ref[...]
            m_st[...] = m0_ref[...]
            xp_scr[:, 0:SUBLANES, :] = hist_ref[...]

        c_in, n_in, m_in, c_out, n_out, m_out = c_st, n_st, m_st, c_st, n_st, m_st
    else:
        xp_scr[:, 0:SUBLANES, :] = hist_ref[...]
        c_in, n_in, m_in, c_out, n_out, m_out = c0_ref, n0_ref, m0_ref, c1_ref, n1_ref, m1_ref

    qk = _causal_conv_silu(p_ref[:, :2 * A_QK], xp_scr, cw_ref, cb_ref, n_seg, seg, carried)
    log_i = gi_ref[...] + gbi_ref[...]
    f_pre = gf_ref[...] + gbf_ref[...]
    log_f = jnp.minimum(f_pre, 0.0) - jnp.log1p(jnp.exp(-jnp.abs(f_pre)))
    a = _seg_cumsum(log_f, seg)
    a_rev = _seg_cumsum(log_f, seg, reverse=True)
    m0 = m_in[...]
    inter3 = a.reshape(n_seg, seg, LANES) + m0
    inter = inter3.reshape(rows, LANES)
    src3 = (a_rev - log_f + log_i).reshape(n_seg, seg, LANES)
    inter_end = inter3[:, seg - 1:seg, :]
    m_end = jnp.maximum(inter_end, jnp.max(src3, axis=1, keepdims=True))
    w_old = jnp.exp(inter_end - m_end)
    w_src = jnp.exp(src3 - m_end).reshape(rows, LANES)
    b_t = (a - log_i).T
    mask = _segment_mask(rows, seg)
    seg_of_lane = lax.broadcasted_iota(I32, (A_DK, rows), 1) >> (seg.bit_length() - 1)

    for h in range(A_HEADS):
        q = qk[:, h * A_DK:(h + 1) * A_DK] * (A_DK ** -0.5)
        k = qk[:, A_QK + h * A_DK:A_QK + (h + 1) * A_DK]
        vb = p_ref[:, 2 * A_QK + h * A_DV:2 * A_QK + (h + 1) * A_DV].astype(BF16)
        qb = q.astype(BF16)
        dmat = jnp.where(mask, a[:, h:h + 1] - b_t[h:h + 1, :], NEG_INF)
        m_tok = jnp.maximum(inter[:, h:h + 1], jnp.max(dmat, axis=1, keepdims=True))
        s = _dot_nt(qb, k.astype(BF16)) * jnp.exp(dmat - m_tok)
        w_inter = jnp.exp(inter[:, h:h + 1] - m_tok)
        if n_seg == 1:
            q_c = _dot(qb, c_in[0, h].astype(BF16))
        else:
            q_c = jnp.concatenate([_dot(q[g * seg:(g + 1) * seg], c_in[g, h]) for g in range(n_seg)], axis=0)
        n3 = n_in[:, h:h + 1, :]
        n_tok = jnp.broadcast_to(n3, (n_seg, seg, A_DK)).reshape(rows, A_DK)
        num = _dot(s.astype(BF16), vb) + w_inter * q_c
        den = jnp.sum(s, axis=1, keepdims=True) + w_inter * jnp.sum(q * n_tok, axis=1, keepdims=True)
        hv = num / jnp.maximum(jnp.abs(den), jnp.exp(-m_tok))
        hn = hv * lax.rsqrt(jnp.mean(hv * hv, axis=1, keepdims=True) + EPS) * ng_ref[h:h + 1, :]
        o = jax.nn.sigmoid(p_ref[:, 2 * A_QK + A_V + h * A_DV:2 * A_QK + A_V + (h + 1) * A_DV])
        y_ref[:, h * A_DV:(h + 1) * A_DV] = o * hn
        kw = k * w_src[:, h:h + 1]
        kw_t = kw.T
        for g in range(n_seg):
            kg = kw_t if n_seg == 1 else jnp.where(seg_of_lane == g, kw_t, 0.0)
            c_out[g, h] = w_old[g, :, h:h + 1] * c_in[g, h] + _dot(kg.astype(BF16), vb)
        n_out[:, h:h + 1, :] = (w_old[:, :, h:h + 1] * n3
                                + jnp.sum(kw.reshape(n_seg, seg, A_DK), axis=1, keepdims=True))
    m_out[...] = m_end

    if carried:
        @pl.when(chunk == n_chunks - 1)
        def _():
            c1_ref[...] = c_st[...]
            n1_ref[...] = n_st[...]
            m1_ref[...] = m_st[...]


def _lane_pad(w, cols=LANES):
    return jnp.pad(w, ((0, 0), (0, cols - w.shape[1])))


def mlstm_mixer(grp, x, mod, state, w_in, conv_w, conv_b, gate_b, norm_g, w_out):
    st_c, st_n, st_m, st_conv = state
    n_main = 2 * A_QK + 2 * A_V
    wb = w_in.astype(BF16)
    p_main, gate_i, gate_f = norm_matmul(
        grp, x, mod, 1, 0,
        [wb[:, :n_main], _lane_pad(wb[:, n_main:n_main + A_HEADS]), _lane_pad(wb[:, n_main + A_HEADS:])])
    n_seg, seg, n_chunks, rows = grp.S, grp.L, grp.n_chunks, grp.R
    row_map = lambda b, c: (b * n_chunks + c, 0)
    scratch = [pltpu.VMEM((n_seg, SUBLANES + seg, 2 * A_QK), F32)]
    if n_chunks > 1:
        scratch += [pltpu.VMEM((n_seg, A_HEADS, A_DK, A_DV), F32), pltpu.VMEM((n_seg, A_HEADS, A_DK), F32),
                    pltpu.VMEM((n_seg, 1, LANES), F32)]
    full = lambda shape: pl.BlockSpec(shape, lambda b, c: (0,) * len(shape))
    y, c1, n1, m1 = pl.pallas_call(
        functools.partial(_mlstm_kernel, n_seg=n_seg, seg=seg, n_chunks=n_chunks),
        grid=(grp.n_blocks, n_chunks),
        in_specs=[
            pl.BlockSpec((rows, n_main), row_map),
            pl.BlockSpec((rows, LANES), row_map),
            pl.BlockSpec((rows, LANES), row_map),
            pl.BlockSpec((n_seg, SUBLANES, 2 * A_QK), lambda b, c: (b, 0, 0)),
            pl.BlockSpec((n_seg, A_HEADS, A_DK, A_DV), lambda b, c: (b, 0, 0, 0)),
            pl.BlockSpec((n_seg, A_HEADS, A_DK), lambda b, c: (b, 0, 0)),
            pl.BlockSpec((n_seg, 1, LANES), lambda b, c: (b, 0, 0)),
            full((CONV_W, 2 * A_QK)), full((1, 2 * A_QK)), full((1, LANES)), full((1, LANES)),
            full((A_HEADS, A_DV)),
        ],
        out_specs=[
            pl.BlockSpec((rows, A_V), row_map),
            pl.BlockSpec((n_seg, A_HEADS, A_DK, A_DV), lambda b, c: (b, 0, 0, 0)),
            pl.BlockSpec((n_seg, A_HEADS, A_DK), lambda b, c: (b, 0, 0)),
            pl.BlockSpec((n_seg, 1, LANES), lambda b, c: (b, 0, 0)),
        ],
        out_shape=[
            jax.ShapeDtypeStruct((grp.N, A_V), F32),
            jax.ShapeDtypeStruct(st_c.shape, F32),
            jax.ShapeDtypeStruct(st_n.shape, F32),
            jax.ShapeDtypeStruct((grp.B, 1, LANES), F32),
        ],
        scratch_shapes=scratch,
        compiler_params=_cparams("arbitrary", "arbitrary"),
        name="mlstm_chunk",
    )(p_main, gate_i, gate_f, _pad_history(st_conv), st_c, st_n, _lane_pad(st_m)[:, None, :],
      conv_w, conv_b.reshape(1, -1), _lane_pad(gate_b[None, :A_HEADS]), _lane_pad(gate_b[None, A_HEADS:]), norm_g)
    x_new = matmul_residual(grp, y, w_out.astype(BF16), x, mod, 2)
    conv_new = p_main.reshape(grp.B, grp.T, n_main)[:, grp.T - (CONV_W - 1):, :2 * A_QK]
    return x_new, (c1, n1, m1[:, 0, :A_HEADS], conv_new)


def _softplus(x):
    return jnp.maximum(x, 0.0) + jnp.log1p(jnp.exp(-jnp.abs(x)))


def _ssd_kernel(z_ref, x_ref, b_ref, c_ref, dt_ref, hx_ref, hb_ref, hc_ref, h0_ref,
                cwx_ref, cbx_ref, cwb_ref, cbb_ref, cwc_ref, cbc_ref, dtb_ref, alog_ref, dskip_ref, ng_ref,
                y_ref, h1_ref, xpx, xpb, xpc, *state_scr, n_seg, seg, n_chunks):
    rows = n_seg * seg
    chunk = pl.program_id(2)
    carried = n_chunks > 1

    def load_history():
        xpx[:, 0:SUBLANES, :] = hx_ref[...]
        xpb[:, 0:SUBLANES, :] = hb_ref[...]
        xpc[:, 0:SUBLANES, :] = hc_ref[...]

    if carried:
        (h_st,) = state_scr

        @pl.when(chunk == 0)
        def _():
            h_st[...] = h0_ref[...].reshape(n_seg, B_GW, B_STATE)
            load_history()

        get_h = lambda g: h_st[g]

        def set_h(g, val):
            h_st[g] = val
    else:
        load_history()
        get_h = lambda g: h0_ref[g].reshape(B_GW, B_STATE)

        def set_h(g, val):
            h1_ref[g] = val.reshape(B_HPG, B_HEADDIM, B_STATE)

    x = _causal_conv_silu(x_ref[...], xpx, cwx_ref, cbx_ref, n_seg, seg, carried)
    bm = _causal_conv_silu(b_ref[...], xpb, cwb_ref, cbb_ref, n_seg, seg, carried)
    cm = _causal_conv_silu(c_ref[...], xpc, cwc_ref, cbc_ref, n_seg, seg, carried)
    dt = _softplus(dt_ref[...] + dtb_ref[...])
    da = dt * (-jnp.exp(alog_ref[...]))
    a = _seg_cumsum(da, seg)
    w_end = jnp.exp(_seg_cumsum(da, seg, reverse=True) - da)
    a_t = a.T
    a_end = a.reshape(n_seg, seg, LANES)[:, seg - 1:seg, :]
    mask = _segment_mask(rows, seg)
    head_of_lane = lax.broadcasted_iota(I32, (rows, B_GW), 1) >> 6
    head_of_row = lax.broadcasted_iota(I32, (B_GW, B_STATE), 0) >> 6
    shift = seg.bit_length() - 1
    seg_of_lane = lax.broadcasted_iota(I32, (B_GW, rows), 1) >> shift

    def per_head(cols, selector):
        out = cols[B_HPG - 1]
        for e in range(B_HPG - 2, -1, -1):
            out = jnp.where(selector == e, cols[e], out)
        return out

    xdt = x * per_head([dt[:, e:e + 1] for e in range(B_HPG)], head_of_lane)
    xdt_b = xdt.astype(BF16)
    bm_b = bm.astype(BF16)
    cb = _dot_nt(cm.astype(BF16), bm_b)
    y = None
    for e in range(B_HPG):
        decay = jnp.exp(jnp.where(mask, a[:, e:e + 1] - a_t[e:e + 1, :], NEG_INF))
        ye = _dot((cb * decay).astype(BF16), xdt_b)
        y = ye if y is None else jnp.where(head_of_lane == e, ye, y)
    if n_seg == 1:
        y_state = _dot_nt(cm.astype(BF16), get_h(0).astype(BF16))
    else:
        y_state = jnp.concatenate([_dot_nt(cm[g * seg:(g + 1) * seg], get_h(g)) for g in range(n_seg)], axis=0)
    y = y + per_head([jnp.exp(a[:, e:e + 1]) for e in range(B_HPG)], head_of_lane) * y_state
    y = (y + dskip_ref[...] * x) * _silu(z_ref[...])
    y_ref[...] = y * lax.rsqrt(jnp.mean(y * y, axis=1, keepdims=True) + EPS) * ng_ref[...]
    xw_t = (xdt * per_head([w_end[:, e:e + 1] for e in range(B_HPG)], head_of_lane)).T
    for g in range(n_seg):
        xg = xw_t if n_seg == 1 else jnp.where(seg_of_lane == g, xw_t, 0.0)
        keep = per_head([jnp.exp(a_end[g][:, e:e + 1]) for e in range(B_HPG)], head_of_row)
        set_h(g, keep * get_h(g) + _dot(xg.astype(BF16), bm_b))

    if carried:
        @pl.when(chunk == n_chunks - 1)
        def _():
            h1_ref[...] = h_st[...].reshape(n_seg, B_HPG, B_HEADDIM, B_STATE)


def ssd_mixer(grp, x, mod, state, w_in, conv_w, conv_b, dt_bias, a_log, d_skip, norm_g, w_out):
    st_h, st_conv = state
    n_main = B_INNER + B_CONV_DIM
    wb = w_in.astype(BF16)
    group_lanes = lambda v: jnp.pad(v.reshape(-1, B_GROUPS, B_HPG), ((0, 0), (0, 0), (0, LANES - B_HPG))).reshape(
        -1, B_GROUPS * LANES)
    p_main, dt_raw = norm_matmul(grp, x, mod, 1, 0, [wb[:, :n_main], group_lanes(wb[:, n_main:])])
    n_seg, seg, n_chunks, rows = grp.S, grp.L, grp.n_chunks, grp.R
    xo, bo, co = B_INNER // B_GW, (2 * B_INNER) // B_STATE, (2 * B_INNER + B_GN) // B_STATE
    cxo, cbo, cco = 0, B_INNER // B_STATE, (B_INNER + B_GN) // B_STATE
    row = lambda b, g, c: b * n_chunks + c
    hist = _pad_history(st_conv)
    scratch = [pltpu.VMEM((n_seg, SUBLANES + seg, B_GW), F32), pltpu.VMEM((n_seg, SUBLANES + seg, B_STATE), F32),
               pltpu.VMEM((n_seg, SUBLANES + seg, B_STATE), F32)]
    if n_chunks > 1:
        scratch.append(pltpu.VMEM((n_seg, B_GW, B_STATE), F32))
    state_spec = pl.BlockSpec((n_seg, B_HPG, B_HEADDIM, B_STATE), lambda b, g, c: (b, g, 0, 0))
    y, h1 = pl.pallas_call(
        functools.partial(_ssd_kernel, n_seg=n_seg, seg=seg, n_chunks=n_chunks),
        grid=(grp.n_blocks, B_GROUPS, n_chunks),
        in_specs=[
            pl.BlockSpec((rows, B_GW), lambda b, g, c: (row(b, g, c), g)),
            pl.BlockSpec((rows, B_GW), lambda b, g, c: (row(b, g, c), xo + g)),
            pl.BlockSpec((rows, B_STATE), lambda b, g, c: (row(b, g, c), bo + g)),
            pl.BlockSpec((rows, B_STATE), lambda b, g, c: (row(b, g, c), co + g)),
            pl.BlockSpec((rows, LANES), lambda b, g, c: (row(b, g, c), g)),
            pl.BlockSpec((n_seg, SUBLANES, B_GW), lambda b, g, c: (b, 0, cxo + g)),
            pl.BlockSpec((n_seg, SUBLANES, B_STATE), lambda b, g, c: (b, 0, cbo + g)),
            pl.BlockSpec((n_seg, SUBLANES, B_STATE), lambda b, g, c: (b, 0, cco + g)),
            state_spec,
            pl.BlockSpec((CONV_W, B_GW), lambda b, g, c: (0, cxo + g)),
            pl.BlockSpec((1, B_GW), lambda b, g, c: (0, cxo + g)),
            pl.BlockSpec((CONV_W, B_STATE), lambda b, g, c: (0, cbo + g)),
            pl.BlockSpec((1, B_STATE), lambda b, g, c: (0, cbo + g)),
            pl.BlockSpec((CONV_W, B_STATE), lambda b, g, c: (0, cco + g)),
            pl.BlockSpec((1, B_STATE), lambda b, g, c: (0, cco + g)),
            pl.BlockSpec((1, LANES), lambda b, g, c: (0, g)),
            pl.BlockSpec((1, LANES), lambda b, g, c: (0, g)),
            pl.BlockSpec((1, B_GW), lambda b, g, c: (0, g)),
            pl.BlockSpec((1, B_GW), lambda b, g, c: (0, g)),
        ],
        out_specs=[pl.BlockSpec((rows, B_GW), lambda b, g, c: (row(b, g, c), g)), state_spec],
        out_shape=[jax.ShapeDtypeStruct((grp.N, B_INNER), F32), jax.ShapeDtypeStruct(st_h.shape, F32)],
        scratch_shapes=scratch,
        compiler_params=_cparams("arbitrary", "arbitrary", "arbitrary"),
        name="ssd_chunk",
    )(p_main, p_main, p_main, p_main, dt_raw, hist, hist, hist, st_h,
      conv_w, conv_b.reshape(1, -1), conv_w, conv_b.reshape(1, -1), conv_w, conv_b.reshape(1, -1),
      group_lanes(dt_bias[None, :]), group_lanes(a_log[None, :]),
      jnp.repeat(d_skip, B_HEADDIM)[None, :], norm_g[None, :])
    x_new = matmul_residual(grp, y, w_out.astype(BF16), x, mod, 2)
    conv_new = p_main.reshape(grp.B, grp.T, n_main)[:, grp.T - (CONV_W - 1):, B_INNER:]
    return x_new, (h1, conv_new)


C_Q = C_HEADS * C_HD
C_KV = C_KV_HEADS * C_HD


def _rope_kernel(p_ref, cos_ref, sin_ref, qg_ref, kg_ref, q_ref, k_ref):
    tm = p_ref.shape[0]
    lane = lax.broadcasted_iota(I32, (tm, LANES), 1)
    low_head = lane < C_HD
    first_half = (lane & (C_HD - 1)) < C_HD // 2
    cos, sin = cos_ref[...], sin_ref[...]

    def norm_rope(xb, gain):
        sq = xb * xb
        s_lo = jnp.sum(jnp.where(low_head, sq, 0.0), axis=1, keepdims=True)
        s_hi = jnp.sum(jnp.where(low_head, 0.0, sq), axis=1, keepdims=True)
        ms = jnp.where(low_head, s_lo, s_hi) * (1.0 / C_HD)
        xn = xb * lax.rsqrt(ms + EPS) * gain
        partner = jnp.where(first_half, pltpu.roll(xn, LANES - C_HD // 2, axis=1), pltpu.roll(xn, C_HD // 2, axis=1))
        return xn * cos + partner * sin

    for j in range(C_Q // LANES):
        q_ref[:, j * LANES:(j + 1) * LANES] = norm_rope(p_ref[:, j * LANES:(j + 1) * LANES], qg_ref[...])
    for j in range(C_KV // LANES):
        k_ref[:, j * LANES:(j + 1) * LANES] = norm_rope(p_ref[:, C_Q + j * LANES:C_Q + (j + 1) * LANES], kg_ref[...])


def _swa_kernel(q_ref, k0_ref, k1_ref, v0_ref, v1_ref, sink_ref, o_ref, *, n_units, tq, blocks_per_seq):
    i = pl.program_id(0)
    rows = C_GROUP * tq
    t = lax.broadcasted_iota(I32, (rows, 2 * WINDOW), 0) & (tq - 1)
    s = lax.broadcasted_iota(I32, (rows, 2 * WINDOW), 1)
    valid = (s >= t) & (s <= t + WINDOW)
    if blocks_per_seq:
        valid = valid & (s >= jnp.where(lax.rem(i, blocks_per_seq) == 0, WINDOW, 0))

    def with_past(past, new):
        if tq < WINDOW:
            new = jnp.concatenate([new, jnp.zeros((WINDOW - tq, new.shape[1]), F32)], axis=0)
        return jnp.concatenate([past, new], axis=0)

    for u in range(n_units):
        kk = with_past(k0_ref[u], k1_ref[u])
        vv = with_past(v0_ref[u], v1_ref[u])
        qu = q_ref[u]
        outs = []
        for kh in range(C_KV_HEADS):
            kc = kk[:, kh * C_HD:(kh + 1) * C_HD].astype(BF16)
            vc = vv[:, kh * C_HD:(kh + 1) * C_HD].astype(BF16)
            qs = jnp.concatenate([qu[:, (kh * C_GROUP + j) * C_HD:(kh * C_GROUP + j + 1) * C_HD]
                                  for j in range(C_GROUP)], axis=0).astype(BF16)
            logits = jnp.where(valid, _dot_nt(qs, kc) * (C_HD ** -0.5), NEG_INF)
            sink = sink_ref[kh][:, 0:1]
            mx = jnp.maximum(jnp.max(logits, axis=1, keepdims=True), sink)
            pr = jnp.exp(logits - mx)
            den = jnp.sum(pr, axis=1, keepdims=True) + jnp.exp(sink - mx)
            o = _dot(pr.astype(BF16), vc) / den
            outs += [o[j * tq:(j + 1) * tq] for j in range(C_GROUP)]
        o_ref[u] = jnp.concatenate(outs, axis=1)


def swa_mixer(grp, x, mod, cache, pos0, w_in, q_g, k_g, sinks, w_out):
    (p,) = norm_matmul(grp, x, mod, 1, 0, [w_in.astype(BF16)])
    half = C_HD // 2
    inv = ROPE_THETA ** (-jnp.arange(half, dtype=F32) / half)
    ang = (pos0 + jnp.arange(grp.T)).astype(F32)[:, None] * inv[None, :]
    cos, sin = jnp.cos(ang), jnp.sin(ang)
    per_token = lambda a: jnp.tile(a, (grp.B, 1))
    cos_t = per_token(jnp.tile(cos, (1, LANES // half)))
    sin_t = per_token(jnp.tile(jnp.concatenate([-sin, sin], axis=1), (1, LANES // C_HD)))
    tm = grp.tm
    gains = lambda g: jnp.tile(g, LANES // C_HD)[None, :]
    qr, kr = pl.pallas_call(
        _rope_kernel,
        grid=(grp.N // tm,),
        in_specs=[
            pl.BlockSpec((tm, C_Q + 2 * C_KV), lambda i: (i, 0)),
            pl.BlockSpec((tm, LANES), lambda i: (i, 0)),
            pl.BlockSpec((tm, LANES), lambda i: (i, 0)),
            pl.BlockSpec((1, LANES), lambda i: (0, 0)),
            pl.BlockSpec((1, LANES), lambda i: (0, 0)),
        ],
        out_specs=[pl.BlockSpec((tm, C_Q), lambda i: (i, 0)), pl.BlockSpec((tm, C_KV), lambda i: (i, 0))],
        out_shape=[jax.ShapeDtypeStruct((grp.N, C_Q), F32), jax.ShapeDtypeStruct((grp.N, C_KV), F32)],
        compiler_params=_cparams("arbitrary"),
        name="qk_norm_rope",
    )(p, cos_t, sin_t, gains(q_g), gains(k_g))

    v_col = (C_Q + C_KV) // C_KV
    if cache is None:
        assert grp.T % WINDOW == 0
        tq, n_units, blocks_per_seq = WINDOW, 1, grp.T // WINDOW
        n_steps = grp.N // tq
        k3 = kr.reshape(n_steps, tq, C_KV)
        p3 = p.reshape(n_steps, tq, C_Q + 2 * C_KV)
        k_args = (k3, k3, p3, p3)
        prev = lambda i: (jnp.maximum(i - 1, 0), 0, 0)
        k_specs = [
            pl.BlockSpec((1, tq, C_KV), prev),
            pl.BlockSpec((1, tq, C_KV), lambda i: (i, 0, 0)),
            pl.BlockSpec((1, tq, C_KV), lambda i: (jnp.maximum(i - 1, 0), 0, v_col)),
            pl.BlockSpec((1, tq, C_KV), lambda i: (i, 0, v_col)),
        ]
    else:
        tq, n_units, blocks_per_seq = grp.T, SUBLANES, 0
        n_steps = grp.B // n_units
        k_args = (cache[0].reshape(grp.B, WINDOW, C_KV), kr.reshape(grp.B, tq, C_KV),
                  cache[1].reshape(grp.B, WINDOW, C_KV), p.reshape(grp.B, tq, C_Q + 2 * C_KV))
        k_specs = [
            pl.BlockSpec((n_units, WINDOW, C_KV), lambda i: (i, 0, 0)),
            pl.BlockSpec((n_units, tq, C_KV), lambda i: (i, 0, 0)),
            pl.BlockSpec((n_units, WINDOW, C_KV), lambda i: (i, 0, 0)),
            pl.BlockSpec((n_units, tq, C_KV), lambda i: (i, 0, v_col)),
        ]
    sink_rows = jnp.broadcast_to(sinks.reshape(C_KV_HEADS, C_GROUP, 1, 1),
                                 (C_KV_HEADS, C_GROUP, tq, LANES)).reshape(C_KV_HEADS, C_GROUP * tq, LANES)
    o = pl.pallas_call(
        functools.partial(_swa_kernel, n_units=n_units, tq=tq, blocks_per_seq=blocks_per_seq),
        grid=(n_steps,),
        in_specs=[pl.BlockSpec((n_units, tq, C_Q), lambda i: (i, 0, 0))] + k_specs
        + [pl.BlockSpec(sink_rows.shape, lambda i: (0, 0, 0))],
        out_specs=pl.BlockSpec((n_units, tq, C_Q), lambda i: (i, 0, 0)),
        out_shape=jax.ShapeDtypeStruct((grp.N // tq, tq, C_Q), F32),
        compiler_params=_cparams("arbitrary"),
        name="swa_attention",
    )(qr.reshape(grp.N // tq, tq, C_Q), *k_args, sink_rows)
    x_new = matmul_residual(grp, o.reshape(grp.N, C_Q), w_out.astype(BF16), x, mod, 2)
    k_new = kr.reshape(grp.B, grp.T, C_KV_HEADS, C_HD)
    v_new = p[:, C_Q + C_KV:].reshape(grp.B, grp.T, C_KV_HEADS, C_HD)
    if cache is None:
        return x_new, (k_new[:, grp.T - WINDOW:], v_new[:, grp.T - WINDOW:])
    return x_new, (jnp.concatenate([cache[0][:, grp.T:], k_new], axis=1),
                   jnp.concatenate([cache[1][:, grp.T:], v_new], axis=1))


def kernel(x_prompt, x_sample, c_prompt, c_sample, state_mlstm_C, state_mlstm_n, state_mlstm_m, state_mlstm_conv,
           state_ssm, state_ssm_conv, cache_swa_k, cache_swa_v, ada_w, ada_b, mlstm_w_in, mlstm_conv_w,
           mlstm_conv_b, mlstm_gate_b, mlstm_norm_g, mlstm_w_out, ssd_w_in, ssd_conv_w, ssd_conv_b, ssd_dt_bias,
           ssd_a_log, ssd_d_skip, ssd_norm_g, ssd_w_out, swa_w_in, swa_q_norm_g, swa_k_norm_g, swa_sinks,
           swa_w_out, moe_w_router, moe_b_router, moe_w1, moe_b1, moe_w2, moe_b2):
    groups = [Group(*x_prompt.shape[:2]), Group(*x_sample.shape[:2])]
    n_prompt = groups[0].B
    xs = [x_prompt.reshape(-1, D_MODEL), x_sample.reshape(-1, D_MODEL)]
    mod_all = ada_modulation(jnp.concatenate([c_prompt, c_sample], axis=0), ada_w, ada_b)
    fresh = lambda s: jnp.zeros((n_prompt,) + s.shape[2:], F32)
    new = [[[] for _ in range(8)] for _ in groups]
    for layer in range(DEPTH):
        kind, j = layer % 3, layer // 3
        mods = [groups[0].expand_mod(mod_all[layer, :n_prompt]), groups[1].expand_mod(mod_all[layer, n_prompt:])]
        for gi, grp in enumerate(groups):
            if kind == 0:
                state = (state_mlstm_C, state_mlstm_n, state_mlstm_m, state_mlstm_conv)
                state = tuple(s[j] if gi else fresh(s) for s in state)
                xs[gi], st = mlstm_mixer(grp, xs[gi], mods[gi], state, mlstm_w_in[j], mlstm_conv_w[j], mlstm_conv_b[j],
                                         mlstm_gate_b[j], mlstm_norm_g[j], mlstm_w_out[j])
                first = 0
            elif kind == 1:
                state = tuple(s[j] if gi else fresh(s) for s in (state_ssm, state_ssm_conv))
                xs[gi], st = ssd_mixer(grp, xs[gi], mods[gi], state, ssd_w_in[j], ssd_conv_w[j], ssd_conv_b[j],
                                       ssd_dt_bias[j], ssd_a_log[j], ssd_d_skip[j], ssd_norm_g[j], ssd_w_out[j])
                first = 4
            else:
                cache = (cache_swa_k[j], cache_swa_v[j]) if gi else None
                xs[gi], st = swa_mixer(grp, xs[gi], mods[gi], cache, PAST_LEN if gi else 0, swa_w_in[j],
                                       swa_q_norm_g[j], swa_k_norm_g[j], swa_sinks[j], swa_w_out[j])
                first = 6
            for offset, s in enumerate(st):
                new[gi][first + offset].append(s)
        xs = moe_layer(groups, xs, mods, layer, moe_w_router, moe_b_router, moe_w1, moe_b1, moe_w2, moe_b2)
    outs = [xs[0].reshape(x_prompt.shape), xs[1].reshape(x_sample.shape)]
    for slot in range(8):
        outs += [jnp.stack(new[0][slot]), jnp.stack(new[1][slot])]
    return tuple(outs)
```

```python
import functools
import math

import jax
import jax.numpy as jnp
from jax import lax
from jax.experimental import pallas as pl
from jax.experimental.pallas import tpu as pltpu

F32 = jnp.float32
BF16 = jnp.bfloat16
I32 = jnp.int32

D_MODEL = 1024
DEPTH = 4
PAST_LEN = 8192
EPS = 1e-6
CONV_W = 4
A_HEADS = 4
A_DK = D_MODEL // 8
A_DV = D_MODEL // A_HEADS
A_QK = A_HEADS * A_DK
A_V = A_HEADS * A_DV
B_INNER = 2 * D_MODEL
B_HEADDIM = 64
B_HEADS = B_INNER // B_HEADDIM
B_STATE = 128
B_GROUPS = 8
B_GN = B_GROUPS * B_STATE
B_CONV_DIM = B_INNER + 2 * B_GN
B_HPG = B_HEADS // B_GROUPS
B_GW = B_HPG * B_HEADDIM
C_HEADS = 16
C_KV_HEADS = 4
C_GROUP = C_HEADS // C_KV_HEADS
C_HD = 64
WINDOW = 128
ROPE_THETA = 10000.0
N_EXPERTS = 32
TOP_K = 4
D_FF = D_MODEL
SWIGLU_LIMIT = 7.0
SWIGLU_ALPHA = 1.702

LANES = 128
SUBLANES = 8
VMEM_LIMIT_BYTES = 56 * 1024 * 1024

ROW_TILE = 256
SEQ_CHUNK = 256
SHORT_BLOCK = 128
MOE_TILE = 512
FF_CHUNK = 256
NEG_INF = float("-inf")
ROW_AS_TILE = (SUBLANES, D_MODEL // SUBLANES)


def _cparams(*sem):
    return pltpu.CompilerParams(dimension_semantics=sem, vmem_limit_bytes=VMEM_LIMIT_BYTES)


def _silu(x):
    return x * jax.nn.sigmoid(x)


def _dot(a, b):
    return jnp.dot(a, b, preferred_element_type=F32)


def _dot_nt(a, b):
    return lax.dot_general(a, b, (((1,), (1,)), ((), ())), preferred_element_type=F32)


class Group:
    def __init__(self, B, T):
        self.B, self.T = B, T
        self.N = B * T
        self.long = T % SEQ_CHUNK == 0
        if self.long:
            self.S, self.L = 1, SEQ_CHUNK
        else:
            assert T == SUBLANES and self.N % SHORT_BLOCK == 0
            self.S, self.L = SHORT_BLOCK // T, T
        self.R = self.S * self.L
        self.n_blocks = B // self.S
        self.n_chunks = T // self.L
        self.tm = min(ROW_TILE, self.N)
        assert self.N % self.tm == 0 and (not self.long or T % self.tm == 0)

    def expand_mod(self, mod):
        if self.long:
            return mod.reshape(self.B, 1, 6 * D_MODEL)
        return jnp.repeat(mod, self.T, axis=0)

    def mod_spec(self, j, rows, row_block_fn):
        if self.long:
            per_seq = self.T // rows
            return pl.BlockSpec((None, 1, D_MODEL), lambda *g: (row_block_fn(*g) // per_seq, 0, j))
        return pl.BlockSpec((rows, D_MODEL), lambda *g: (row_block_fn(*g), j))


def _ada_kernel(c_ref, w_ref, b_ref, o_ref):
    cs = _silu(c_ref[...])
    o_ref[...] = _dot(cs.astype(BF16), w_ref[...].astype(BF16)) + b_ref[...]


def ada_modulation(c_all, ada_w, ada_b):
    rows = c_all.shape[0]
    tn = 1536
    n_out = 6 * D_MODEL
    return pl.pallas_call(
        _ada_kernel,
        grid=(DEPTH, n_out // tn),
        in_specs=[
            pl.BlockSpec((rows, D_MODEL), lambda l, j: (0, 0)),
            pl.BlockSpec((None, D_MODEL, tn), lambda l, j: (l, 0, j)),
            pl.BlockSpec((None, 1, tn), lambda l, j: (l, 0, j)),
        ],
        out_specs=pl.BlockSpec((None, rows, tn), lambda l, j: (l, 0, j)),
        out_shape=jax.ShapeDtypeStruct((DEPTH, rows, n_out), F32),
        compiler_params=_cparams("arbitrary", "arbitrary"),
        name="ada_modulation",
    )(c_all, ada_w, ada_b.reshape(DEPTH, 1, n_out))


def _modulated_norm(x, sc, sh):
    ms = jnp.mean(x * x, axis=-1, keepdims=True)
    return (x * lax.rsqrt(ms + EPS)) * (1.0 + sc) + sh


def _norm_mm_kernel(x_ref, sc_ref, sh_ref, *refs, n_w):
    h = _modulated_norm(x_ref[...], sc_ref[...], sh_ref[...]).astype(BF16)
    for w_ref, o_ref in zip(refs[:n_w], refs[n_w:]):
        o_ref[...] = _dot(h, w_ref[...])


def norm_matmul(grp, x, mod, j_scale, j_shift, weights):
    tm = grp.tm
    n_w = len(weights)
    in_specs = [
        pl.BlockSpec((tm, D_MODEL), lambda i: (i, 0)),
        grp.mod_spec(j_scale, tm, lambda i: i),
        grp.mod_spec(j_shift, tm, lambda i: i),
    ] + [pl.BlockSpec(w.shape, lambda i: (0, 0)) for w in weights]
    out_specs = [pl.BlockSpec((tm, w.shape[1]), lambda i: (i, 0)) for w in weights]
    out_shape = [jax.ShapeDtypeStruct((grp.N, w.shape[1]), F32) for w in weights]
    return pl.pallas_call(
        functools.partial(_norm_mm_kernel, n_w=n_w),
        grid=(grp.N // tm,),
        in_specs=in_specs,
        out_specs=out_specs,
        out_shape=out_shape,
        compiler_params=_cparams("arbitrary"),
        name="norm_matmul",
    )(x, mod, mod, *weights)


def _mm_res_kernel(y_ref, w_ref, x_ref, g_ref, o_ref):
    o_ref[...] = x_ref[...] + g_ref[...] * _dot(y_ref[...].astype(BF16), w_ref[...])


def matmul_residual(grp, y, w, x, mod, j_gate):
    tm = grp.tm
    k = y.shape[1]
    return pl.pallas_call(
        _mm_res_kernel,
        grid=(grp.N // tm,),
        in_specs=[
            pl.BlockSpec((tm, k), lambda i: (i, 0)),
            pl.BlockSpec(w.shape, lambda i: (0, 0)),
            pl.BlockSpec((tm, D_MODEL), lambda i: (i, 0)),
            grp.mod_spec(j_gate, tm, lambda i: i),
        ],
        out_specs=pl.BlockSpec((tm, D_MODEL), lambda i: (i, 0)),
        out_shape=jax.ShapeDtypeStruct((grp.N, D_MODEL), F32),
        compiler_params=_cparams("arbitrary"),
        name="matmul_residual",
    )(y, w, x, mod)


def _router_kernel(x_ref, sc_ref, sh_ref, wr_ref, br_ref, cnt_in_ref,
                   h_ref, ri_ref, rg_ref, cnt_out_ref, cnt_scr):
    i = pl.program_id(0)
    tm = x_ref.shape[0]

    @pl.when(i == 0)
    def _():
        cnt_scr[...] = cnt_in_ref[...]

    h = _modulated_norm(x_ref[...], sc_ref[...], sh_ref[...])
    h_ref[...] = h.reshape(tm, *ROW_AS_TILE)
    logits = jnp.dot(h, wr_ref[...], preferred_element_type=F32,
                     precision=lax.Precision.HIGHEST) + br_ref[...]
    lane = lax.broadcasted_iota(I32, (tm, LANES), 1)
    lane_f = lane.astype(F32)
    work = jnp.where(lane < N_EXPERTS, logits, NEG_INF)
    top_v, top_sel, top_i = [], [], []
    for _ in range(TOP_K):
        mx = jnp.max(work, axis=1, keepdims=True)
        idx = jnp.min(jnp.where(work == mx, lane_f, float(LANES)), axis=1, keepdims=True)
        sel = lane_f == idx
        work = jnp.where(sel, NEG_INF, work)
        top_v.append(mx)
        top_sel.append(sel)
        top_i.append(idx)
    ex = [jnp.exp(v - top_v[0]) for v in top_v]
    inv = 1.0 / (ex[0] + ex[1] + ex[2] + ex[3])
    chosen = jnp.zeros((tm, LANES), F32)
    for sel in top_sel:
        chosen = jnp.where(sel, 1.0, chosen)
    r = lax.broadcasted_iota(I32, (tm, tm), 0)
    c = lax.broadcasted_iota(I32, (tm, tm), 1)
    before = jnp.where(c < r, 1.0, 0.0).astype(BF16)
    rank_all = _dot(before, chosen.astype(BF16)) + cnt_scr[...]
    ri = jnp.zeros((tm, LANES), I32)
    rg = jnp.zeros((tm, LANES), F32)
    for k in range(TOP_K):
        rank_k = jnp.sum(jnp.where(top_sel[k], rank_all, 0.0), axis=1, keepdims=True)
        ri = jnp.where(lane == k, top_i[k].astype(I32), ri)
        ri = jnp.where(lane == TOP_K + k, rank_k.astype(I32), ri)
        rg = jnp.where(lane == k, ex[k] * inv, rg)
    ri_ref[...] = ri
    rg_ref[...] = rg
    cnt_scr[...] = cnt_scr[...] + jnp.sum(chosen, axis=0, keepdims=True)
    cnt_out_ref[...] = cnt_scr[...]


def moe_router(grp, x, mod, w_r, b_r, cnt_in):
    tm = grp.tm
    return pl.pallas_call(
        _router_kernel,
        grid=(grp.N // tm,),
        in_specs=[
            pl.BlockSpec((tm, D_MODEL), lambda i: (i, 0)),
            grp.mod_spec(4, tm, lambda i: i),
            grp.mod_spec(3, tm, lambda i: i),
            pl.BlockSpec((D_MODEL, LANES), lambda i: (0, 0)),
            pl.BlockSpec((1, LANES), lambda i: (0, 0)),
            pl.BlockSpec((1, LANES), lambda i: (0, 0)),
        ],
        out_specs=[
            pl.BlockSpec((tm,) + ROW_AS_TILE, lambda i: (i, 0, 0)),
            pl.BlockSpec((tm, LANES), lambda i: (i, 0)),
            pl.BlockSpec((tm, LANES), lambda i: (i, 0)),
            pl.BlockSpec((1, LANES), lambda i: (0, 0)),
        ],
        out_shape=[
            jax.ShapeDtypeStruct((grp.N,) + ROW_AS_TILE, F32),
            jax.ShapeDtypeStruct((grp.N, LANES), I32),
            jax.ShapeDtypeStruct((grp.N, LANES), F32),
            jax.ShapeDtypeStruct((1, LANES), F32),
        ],
        scratch_shapes=[pltpu.VMEM((1, LANES), F32)],
        compiler_params=_cparams("arbitrary"),
        name="moe_router",
    )(x, mod, mod, w_r, b_r, cnt_in)


def _row_copy(src_ref, src_row, dst_ref, dst_row, sem):
    return pltpu.make_async_copy(src_ref.at[src_row], dst_ref.at[dst_row], sem)


DMA_UNROLL = 8


def _dispatch_kernel(pos_ref, cnt_ref, off_ref, nv_ref, *refs, tiles, n_sorted_tiles):
    n_groups = len(tiles)
    h_refs = refs[:n_groups]
    xs_ref, zero_scr, sem = refs[n_groups:]
    i = pl.program_id(0)
    tm = h_refs[0].shape[0]
    base = i * (tm * TOP_K)

    def scatter_tile(h_ref):
        def start(t, carry):
            for k in range(TOP_K):
                _row_copy(h_ref, t, xs_ref, pos_ref[base + t * TOP_K + k], sem).start()
            return carry

        lax.fori_loop(0, tm, start, 0, unroll=DMA_UNROLL)
        for _ in range(TOP_K):
            pltpu.make_async_copy(h_ref, xs_ref.at[pl.ds(0, tm)], sem).wait()

    first = 0
    for h_ref, n in zip(h_refs, tiles):
        pl.when((i >= first) & (i < first + n))(functools.partial(scatter_tile, h_ref))
        first += n

    @pl.when(i == first)
    def _():
        zero_scr[...] = jnp.zeros(zero_scr.shape, F32)
        for e in range(N_EXPERTS):
            n = cnt_ref[e]
            row0 = off_ref[e] + n
            pad = (MOE_TILE - (n & (MOE_TILE - 1))) & (MOE_TILE - 1)

            def start(r, carry, row0=row0):
                _row_copy(zero_scr, 0, xs_ref, row0 + r, sem).start()
                return carry

            lax.fori_loop(0, pad, start, 0)

            def wait(r, carry):
                _row_copy(zero_scr, 0, xs_ref, 0, sem).wait()
                return carry

            lax.fori_loop(0, pad, wait, 0)

        def fill_tile(j, carry):
            cp = pltpu.make_async_copy(zero_scr, xs_ref.at[pl.ds(j * MOE_TILE, MOE_TILE)], sem)
            cp.start()
            cp.wait()
            return carry

        lax.fori_loop(nv_ref[0], n_sorted_tiles, fill_tile, 0)


def moe_dispatch(groups, hs, pos_all, counts, row_off, n_valid, n_sorted_tiles):
    tm = groups[0].tm
    assert all(g.tm == tm for g in groups)
    tiles = tuple(g.N // tm for g in groups)
    starts = [sum(tiles[:gi]) for gi in range(len(tiles))]

    def h_spec(first, n):
        return pl.BlockSpec((tm,) + ROW_AS_TILE, lambda i, *_: (jnp.clip(i - first, 0, n - 1), 0, 0))

    return pl.pallas_call(
        functools.partial(_dispatch_kernel, tiles=tiles, n_sorted_tiles=n_sorted_tiles),
        grid_spec=pltpu.PrefetchScalarGridSpec(
            num_scalar_prefetch=4,
            grid=(sum(tiles) + 1,),
            in_specs=[h_spec(first, n) for first, n in zip(starts, tiles)],
            out_specs=pl.BlockSpec(memory_space=pl.ANY),
            scratch_shapes=[pltpu.VMEM((MOE_TILE,) + ROW_AS_TILE, F32), pltpu.SemaphoreType.DMA],
        ),
        out_shape=jax.ShapeDtypeStruct((n_sorted_tiles * MOE_TILE,) + ROW_AS_TILE, F32),
        compiler_params=_cparams("arbitrary"),
        name="moe_dispatch",
    )(pos_all, counts, row_off, n_valid, *hs)


def _combine_kernel(pos_ref, ys_ref, rg_ref, x_ref, g_ref, o_ref, ybuf, sems):
    i = pl.program_id(0)
    n = pl.num_programs(0)
    tm = x_ref.shape[0]

    def gather_tile(tile, slot):
        base = tile * (tm * TOP_K)

        def start(t, carry):
            for k in range(TOP_K):
                _row_copy(ys_ref, pos_ref[base + t * TOP_K + k], ybuf.at[slot, k], t, sems.at[slot]).start()
            return carry

        lax.fori_loop(0, tm, start, 0, unroll=DMA_UNROLL)

    slot = lax.rem(i, 2)

    @pl.when(i == 0)
    def _():
        gather_tile(0, 0)

    @pl.when(i + 1 < n)
    def _():
        gather_tile(i + 1, 1 - slot)

    for k in range(TOP_K):
        pltpu.make_async_copy(ys_ref.at[pl.ds(0, tm)], ybuf.at[slot, k], sems.at[slot]).wait()
    rg = rg_ref[...]
    acc = rg[:, 0:1] * ybuf[slot, 0].reshape(tm, D_MODEL)
    for k in range(1, TOP_K):
        acc = acc + rg[:, k:k + 1] * ybuf[slot, k].reshape(tm, D_MODEL)
    o_ref[...] = x_ref[...] + g_ref[...] * acc


def moe_combine(grp, ys, pos, route_g, x, mod):
    tm = grp.tm
    return pl.pallas_call(
        _combine_kernel,
        grid_spec=pltpu.PrefetchScalarGridSpec(
            num_scalar_prefetch=1,
            grid=(grp.N // tm,),
            in_specs=[
                pl.BlockSpec(memory_space=pl.ANY),
                pl.BlockSpec((tm, LANES), lambda i, p: (i, 0)),
                pl.BlockSpec((tm, D_MODEL), lambda i, p: (i, 0)),
                grp.mod_spec(5, tm, lambda i, p: i),
            ],
            out_specs=pl.BlockSpec((tm, D_MODEL), lambda i, p: (i, 0)),
            scratch_shapes=[pltpu.VMEM((2, TOP_K, tm) + ROW_AS_TILE, F32), pltpu.SemaphoreType.DMA((2,))],
        ),
        out_shape=jax.ShapeDtypeStruct((grp.N, D_MODEL), F32),
        compiler_params=_cparams("arbitrary"),
        name="moe_combine",
    )(pos, ys, route_g, x, mod)


def _experts_kernel(te_ref, tb_ref, nv_ref, x_ref, w1_ref, b1_ref, w2_ref, b2_ref, y_ref, w1_scr, w2_scr):
    i = pl.program_id(0)

    @pl.when(i < nv_ref[0])
    def _():
        prev = te_ref[jnp.maximum(i - 1, 0)]

        @pl.when((i == 0) | (te_ref[i] != prev))
        def _():
            w1_scr[...] = w1_ref[...].astype(BF16)
            w2_scr[...] = w2_ref[...].astype(BF16)

        xb = x_ref[...].reshape(MOE_TILE, D_MODEL).astype(BF16)
        y = b2_ref[...]
        for j in range(D_FF // FF_CHUNK):
            cols = slice(j * FF_CHUNK, (j + 1) * FF_CHUNK)
            ucols = slice(D_FF + j * FF_CHUNK, D_FF + (j + 1) * FF_CHUNK)
            g = jnp.minimum(_dot(xb, w1_scr[:, cols]) + b1_ref[:, cols], SWIGLU_LIMIT)
            u = jnp.clip(_dot(xb, w1_scr[:, ucols]) + b1_ref[:, ucols], -SWIGLU_LIMIT, SWIGLU_LIMIT)
            act = (u + 1.0) * (g * jax.nn.sigmoid(SWIGLU_ALPHA * g))
            y = y + _dot(act.astype(BF16), w2_scr[cols, :])
        y_ref[...] = y.reshape(y_ref.shape)

    @pl.when(i >= nv_ref[0])
    def _():
        y_ref[...] = jnp.zeros(y_ref.shape, F32)


def moe_experts(xs, tile_expert, tile_block, n_valid, w1, b1, w2, b2, layer):
    n_tiles = tile_expert.shape[0]
    tm = MOE_TILE
    return pl.pallas_call(
        _experts_kernel,
        grid_spec=pltpu.PrefetchScalarGridSpec(
            num_scalar_prefetch=3,
            grid=(n_tiles,),
            in_specs=[
                pl.BlockSpec((tm,) + ROW_AS_TILE, lambda i, te, tb, nv: (tb[i], 0, 0)),
                pl.BlockSpec((None, None, D_MODEL, 2 * D_FF), lambda i, te, tb, nv: (layer, te[i], 0, 0)),
                pl.BlockSpec((None, None, 1, 2 * D_FF), lambda i, te, tb, nv: (layer, te[i], 0, 0)),
                pl.BlockSpec((None, None, D_FF, D_MODEL), lambda i, te, tb, nv: (layer, te[i], 0, 0)),
                pl.BlockSpec((None, None, 1, D_MODEL), lambda i, te, tb, nv: (layer, te[i], 0, 0)),
            ],
            out_specs=pl.BlockSpec((tm,) + ROW_AS_TILE, lambda i, te, tb, nv: (i, 0, 0)),
            scratch_shapes=[pltpu.VMEM((D_MODEL, 2 * D_FF), BF16), pltpu.VMEM((D_FF, D_MODEL), BF16)],
        ),
        out_shape=jax.ShapeDtypeStruct(xs.shape, F32),
        compiler_params=_cparams("arbitrary"),
        name="moe_experts",
    )(tile_expert, tile_block, n_valid, xs, w1, b1.reshape(DEPTH, N_EXPERTS, 1, 2 * D_FF),
      w2, b2.reshape(DEPTH, N_EXPERTS, 1, D_MODEL))


def moe_layer(groups, xs_in, mods, layer, w_router, b_router, w1, b1, w2, b2):
    n_total = sum(g.N for g in groups)
    n_pairs = n_total * TOP_K
    n_tiles = -(-n_pairs // MOE_TILE) + N_EXPERTS
    w_r = jnp.pad(w_router[layer], ((0, 0), (0, LANES - N_EXPERTS)))
    b_r = jnp.pad(b_router[layer], (0, LANES - N_EXPERTS)).reshape(1, LANES)
    cnt = jnp.zeros((1, LANES), F32)
    routed = []
    for grp, x, mod in zip(groups, xs_in, mods):
        h, route_i, route_g, cnt = moe_router(grp, x, mod, w_r, b_r, cnt)
        routed.append((h, route_i, route_g))
    counts = cnt[0, :N_EXPERTS].astype(I32)
    tiles_per = (counts + MOE_TILE - 1) // MOE_TILE
    tile_end = jnp.cumsum(tiles_per)
    tile_start = tile_end - tiles_per
    n_valid = tile_end[-1:]
    tidx = jnp.minimum(jnp.arange(n_tiles, dtype=I32), n_valid - 1)
    tile_expert = jnp.sum((tile_end[None, :] <= tidx[:, None]).astype(I32), axis=1)
    row_off = tile_start * MOE_TILE
    positions = []
    for _, route_i, _ in routed:
        route_e, rank = route_i[:, :TOP_K], route_i[:, TOP_K:2 * TOP_K]
        onehot = route_e[:, :, None] == jnp.arange(N_EXPERTS, dtype=I32)[None, None, :]
        positions.append((jnp.sum(jnp.where(onehot, row_off[None, None, :], 0), axis=2) + rank).reshape(-1))
    xs = moe_dispatch(groups, [h for h, _, _ in routed], jnp.concatenate(positions), counts, row_off, n_valid,
                      n_tiles)
    ys = moe_experts(xs, tile_expert, tidx, n_valid, w1, b1, w2, b2, layer)
    return [moe_combine(grp, ys, pos, route_g, x, mod)
            for grp, x, mod, pos, (_, _, route_g) in zip(groups, xs_in, mods, positions, routed)]


def _seg_cumsum(x, seg, reverse=False):
    rows = x.shape[0]
    row = lax.broadcasted_iota(I32, x.shape, 0) & (seg - 1)
    sh = 1
    while sh < seg:
        if reverse:
            x = x + jnp.where(row + sh < seg, pltpu.roll(x, rows - sh, axis=0), 0.0)
        else:
            x = x + jnp.where(row >= sh, pltpu.roll(x, sh, axis=0), 0.0)
        sh *= 2
    return x


def _segment_mask(rows, seg):
    ti = lax.broadcasted_iota(I32, (rows, rows), 0)
    si = lax.broadcasted_iota(I32, (rows, rows), 1)
    mask = si <= ti
    if seg < rows:
        shift = seg.bit_length() - 1
        mask = mask & ((ti >> shift) == (si >> shift))
    return mask


def _causal_conv_silu(x, xp_scr, w_ref, b_ref, n_seg, seg, carry):
    rows, ch = x.shape
    xp_scr[:, SUBLANES:, :] = x.reshape(n_seg, seg, ch)
    acc = b_ref[...]
    for i in range(CONV_W):
        back = CONV_W - 1 - i
        xs = x if back == 0 else xp_scr[:, SUBLANES - back:SUBLANES - back + seg, :].reshape(rows, ch)
        acc = acc + xs * w_ref[i:i + 1, :]
    if carry:
        xp_scr[:, SUBLANES - (CONV_W - 1):SUBLANES, :] = xp_scr[:, SUBLANES + seg - (CONV_W - 1):SUBLANES + seg, :]
    return _silu(acc)


def _pad_history(buf):
    return jnp.pad(buf, ((0, 0), (SUBLANES - (CONV_W - 1), 0), (0, 0)))


def _mlstm_kernel(p_ref, gi_ref, gf_ref, hist_ref, c0_ref, n0_ref, m0_ref, cw_ref, cb_ref, gbi_ref, gbf_ref,
                  ng_ref, y_ref, c1_ref, n1_ref, m1_ref, xp_scr, *state_scr, n_seg, seg, n_chunks):
    rows = n_seg * seg
    chunk = pl.program_id(1)
    carried = n_chunks > 1
    if carried:
        c_st, n_st, m_st = state_scr

        @pl.when(chunk == 0)
        def _():
            c_st[...] = c0_ref[...]
            n_st[...] = n0_ref[...]
            m_st[...] = m0_ref[...]
            xp_scr[:, 0:SUBLANES, :] = hist_ref[...]

        c_in, n_in, m_in, c_out, n_out, m_out = c_st, n_st, m_st, c_st, n_st, m_st
    else:
        xp_scr[:, 0:SUBLANES, :] = hist_ref[...]
        c_in, n_in, m_in, c_out, n_out, m_out = c0_ref, n0_ref, m0_ref, c1_ref, n1_ref, m1_ref

    qk = _causal_conv_silu(p_ref[:, :2 * A_QK], xp_scr, cw_ref, cb_ref, n_seg, seg, carried)
    log_i = gi_ref[...] + gbi_ref[...]
    f_pre = gf_ref[...] + gbf_ref[...]
    log_f = jnp.minimum(f_pre, 0.0) - jnp.log1p(jnp.exp(-jnp.abs(f_pre)))
    a = _seg_cumsum(log_f, seg)
    a_rev = _seg_cumsum(log_f, seg, reverse=True)
    m0 = m_in[...]
    inter3 = a.reshape(n_seg, seg, LANES) + m0
    inter = inter3.reshape(rows, LANES)
    src3 = (a_rev - log_f + log_i).reshape(n_seg, seg, LANES)
    inter_end = inter3[:, seg - 1:seg, :]
    m_end = jnp.maximum(inter_end, jnp.max(src3, axis=1, keepdims=True))
    w_old = jnp.exp(inter_end - m_end)
    w_src = jnp.exp(src3 - m_end).reshape(rows, LANES)
    b_t = (a - log_i).T
    mask = _segment_mask(rows, seg)
    seg_of_lane = lax.broadcasted_iota(I32, (A_DK, rows), 1) >> (seg.bit_length() - 1)

    for h in range(A_HEADS):
        q = qk[:, h * A_DK:(h + 1) * A_DK] * (A_DK ** -0.5)
        k = qk[:, A_QK + h * A_DK:A_QK + (h + 1) * A_DK]
        vb = p_ref[:, 2 * A_QK + h * A_DV:2 * A_QK + (h + 1) * A_DV].astype(BF16)
        qb = q.astype(BF16)
        dmat = jnp.where(mask, a[:, h:h + 1] - b_t[h:h + 1, :], NEG_INF)
        m_tok = jnp.maximum(inter[:, h:h + 1], jnp.max(dmat, axis=1, keepdims=True))
        s = _dot_nt(qb, k.astype(BF16)) * jnp.exp(dmat - m_tok)
        w_inter = jnp.exp(inter[:, h:h + 1] - m_tok)
        if n_seg == 1:
            q_c = _dot(qb, c_in[0, h].astype(BF16))
        else:
            q_c = jnp.concatenate([_dot(q[g * seg:(g + 1) * seg], c_in[g, h]) for g in range(n_seg)], axis=0)
        n3 = n_in[:, h:h + 1, :]
        n_tok = jnp.broadcast_to(n3, (n_seg, seg, A_DK)).reshape(rows, A_DK)
        num = _dot(s.astype(BF16), vb) + w_inter * q_c
        den = jnp.sum(s, axis=1, keepdims=True) + w_inter * jnp.sum(q * n_tok, axis=1, keepdims=True)
        hv = num / jnp.maximum(jnp.abs(den), jnp.exp(-m_tok))
        hn = hv * lax.rsqrt(jnp.mean(hv * hv, axis=1, keepdims=True) + EPS) * ng_ref[h:h + 1, :]
        o = jax.nn.sigmoid(p_ref[:, 2 * A_QK + A_V + h * A_DV:2 * A_QK + A_V + (h + 1) * A_DV])
        y_ref[:, h * A_DV:(h + 1) * A_DV] = o * hn
        kw = k * w_src[:, h:h + 1]
        kw_t = kw.T
        for g in range(n_seg):
            kg = kw_t if n_seg == 1 else jnp.where(seg_of_lane == g, kw_t, 0.0)
            c_out[g, h] = w_old[g, :, h:h + 1] * c_in[g, h] + _dot(kg.astype(BF16), vb)
        n_out[:, h:h + 1, :] = (w_old[:, :, h:h + 1] * n3
                                + jnp.sum(kw.reshape(n_seg, seg, A_DK), axis=1, keepdims=True))
    m_out[...] = m_end

    if carried:
        @pl.when(chunk == n_chunks - 1)
        def _():
            c1_ref[...] = c_st[...]
            n1_ref[...] = n_st[...]
            m1_ref[...] = m_st[...]


def _lane_pad(w, cols=LANES):
    return jnp.pad(w, ((0, 0), (0, cols - w.shape[1])))


def mlstm_mixer(grp, x, mod, state, w_in, conv_w, conv_b, gate_b, norm_g, w_out):
    (st_c, c_layer), st_n, st_m, st_conv = state
    n_main = 2 * A_QK + 2 * A_V
    wb = w_in.astype(BF16)
    p_main, gate_i, gate_f = norm_matmul(
        grp, x, mod, 1, 0,
        [wb[:, :n_main], _lane_pad(wb[:, n_main:n_main + A_HEADS]), _lane_pad(wb[:, n_main + A_HEADS:])])
    n_seg, seg, n_chunks, rows = grp.S, grp.L, grp.n_chunks, grp.R
    row_map = lambda b, c: (b * n_chunks + c, 0)
    scratch = [pltpu.VMEM((n_seg, SUBLANES + seg, 2 * A_QK), F32)]
    if n_chunks > 1:
        scratch += [pltpu.VMEM((n_seg, A_HEADS, A_DK, A_DV), F32), pltpu.VMEM((n_seg, A_HEADS, A_DK), F32),
                    pltpu.VMEM((n_seg, 1, LANES), F32)]
    full = lambda shape: pl.BlockSpec(shape, lambda b, c: (0,) * len(shape))
    y, c1, n1, m1 = pl.pallas_call(
        functools.partial(_mlstm_kernel, n_seg=n_seg, seg=seg, n_chunks=n_chunks),
        grid=(grp.n_blocks, n_chunks),
        in_specs=[
            pl.BlockSpec((rows, n_main), row_map),
            pl.BlockSpec((rows, LANES), row_map),
            pl.BlockSpec((rows, LANES), row_map),
            pl.BlockSpec((n_seg, SUBLANES, 2 * A_QK), lambda b, c: (b, 0, 0)),
            pl.BlockSpec((None, n_seg, A_HEADS, A_DK, A_DV), lambda b, c: (c_layer, b, 0, 0, 0)),
            pl.BlockSpec((n_seg, A_HEADS, A_DK), lambda b, c: (b, 0, 0)),
            pl.BlockSpec((n_seg, 1, LANES), lambda b, c: (b, 0, 0)),
            full((CONV_W, 2 * A_QK)), full((1, 2 * A_QK)), full((1, LANES)), full((1, LANES)),
            full((A_HEADS, A_DV)),
        ],
        out_specs=[
            pl.BlockSpec((rows, A_V), row_map),
            pl.BlockSpec((n_seg, A_HEADS, A_DK, A_DV), lambda b, c: (b, 0, 0, 0)),
            pl.BlockSpec((n_seg, A_HEADS, A_DK), lambda b, c: (b, 0, 0)),
            pl.BlockSpec((n_seg, 1, LANES), lambda b, c: (b, 0, 0)),
        ],
        out_shape=[
            jax.ShapeDtypeStruct((grp.N, A_V), F32),
            jax.ShapeDtypeStruct(st_c.shape[1:], F32),
            jax.ShapeDtypeStruct(st_n.shape, F32),
            jax.ShapeDtypeStruct((grp.B, 1, LANES), F32),
        ],
        scratch_shapes=scratch,
        compiler_params=_cparams("arbitrary", "arbitrary"),
        name="mlstm_chunk",
    )(p_main, gate_i, gate_f, _pad_history(st_conv), st_c, st_n, _lane_pad(st_m)[:, None, :],
      conv_w, conv_b.reshape(1, -1), _lane_pad(gate_b[None, :A_HEADS]), _lane_pad(gate_b[None, A_HEADS:]), norm_g)
    x_new = matmul_residual(grp, y, w_out.astype(BF16), x, mod, 2)
    conv_new = p_main.reshape(grp.B, grp.T, n_main)[:, grp.T - (CONV_W - 1):, :2 * A_QK]
    return x_new, (c1, n1, m1[:, 0, :A_HEADS], conv_new)


def _softplus(x):
    return jnp.maximum(x, 0.0) + jnp.log1p(jnp.exp(-jnp.abs(x)))


def _ssd_kernel(z_ref, x_ref, b_ref, c_ref, dt_ref, hx_ref, hb_ref, hc_ref, h0_ref,
                cwx_ref, cbx_ref, cwb_ref, cbb_ref, cwc_ref, cbc_ref, dtb_ref, alog_ref, dskip_ref, ng_ref,
                y_ref, h1_ref, xpx, xpb, xpc, *state_scr, n_seg, seg, n_chunks):
    rows = n_seg * seg
    chunk = pl.program_id(2)
    carried = n_chunks > 1

    def load_history():
        xpx[:, 0:SUBLANES, :] = hx_ref[...]
        xpb[:, 0:SUBLANES, :] = hb_ref[...]
        xpc[:, 0:SUBLANES, :] = hc_ref[...]

    if carried:
        (h_st,) = state_scr

        @pl.when(chunk == 0)
        def _():
            h_st[...] = h0_ref[...].reshape(n_seg, B_GW, B_STATE)
            load_history()

        get_h = lambda g: h_st[g]

        def set_h(g, val):
            h_st[g] = val
    else:
        load_history()
        get_h = lambda g: h0_ref[g].reshape(B_GW, B_STATE)

        def set_h(g, val):
            h1_ref[g] = val.reshape(B_HPG, B_HEADDIM, B_STATE)

    x = _causal_conv_silu(x_ref[...], xpx, cwx_ref, cbx_ref, n_seg, seg, carried)
    bm = _causal_conv_silu(b_ref[...], xpb, cwb_ref, cbb_ref, n_seg, seg, carried)
    cm = _causal_conv_silu(c_ref[...], xpc, cwc_ref, cbc_ref, n_seg, seg, carried)
    dt = _softplus(dt_ref[...] + dtb_ref[...])
    da = dt * (-jnp.exp(alog_ref[...]))
    a = _seg_cumsum(da, seg)
    w_end = jnp.exp(_seg_cumsum(da, seg, reverse=True) - da)
    a_t = a.T
    a_end = a.reshape(n_seg, seg, LANES)[:, seg - 1:seg, :]
    mask = _segment_mask(rows, seg)
    head_of_lane = lax.broadcasted_iota(I32, (rows, B_GW), 1) >> 6
    head_of_row = lax.broadcasted_iota(I32, (B_GW, B_STATE), 0) >> 6
    shift = seg.bit_length() - 1
    seg_of_lane = lax.broadcasted_iota(I32, (B_GW, rows), 1) >> shift

    def per_head(cols, selector):
        out = cols[B_HPG - 1]
        for e in range(B_HPG - 2, -1, -1):
            out = jnp.where(selector == e, cols[e], out)
        return out

    xdt = x * per_head([dt[:, e:e + 1] for e in range(B_HPG)], head_of_lane)
    xdt_b = xdt.astype(BF16)
    bm_b = bm.astype(BF16)
    cb = _dot_nt(cm.astype(BF16), bm_b)
    y = None
    for e in range(B_HPG):
        decay = jnp.exp(jnp.where(mask, a[:, e:e + 1] - a_t[e:e + 1, :], NEG_INF))
        ye = _dot((cb * decay).astype(BF16), xdt_b)
        y = ye if y is None else jnp.where(head_of_lane == e, ye, y)
    if n_seg == 1:
        y_state = _dot_nt(cm.astype(BF16), get_h(0).astype(BF16))
    else:
        y_state = jnp.concatenate([_dot_nt(cm[g * seg:(g + 1) * seg], get_h(g)) for g in range(n_seg)], axis=0)
    y = y + per_head([jnp.exp(a[:, e:e + 1]) for e in range(B_HPG)], head_of_lane) * y_state
    y = (y + dskip_ref[...] * x) * _silu(z_ref[...])
    y_ref[...] = y * lax.rsqrt(jnp.mean(y * y, axis=1, keepdims=True) + EPS) * ng_ref[...]
    xw_t = (xdt * per_head([w_end[:, e:e + 1] for e in range(B_HPG)], head_of_lane)).T
    for g in range(n_seg):
        xg = xw_t if n_seg == 1 else jnp.where(seg_of_lane == g, xw_t, 0.0)
        keep = per_head([jnp.exp(a_end[g][:, e:e + 1]) for e in range(B_HPG)], head_of_row)
        set_h(g, keep * get_h(g) + _dot(xg.astype(BF16), bm_b))

    if carried:
        @pl.when(chunk == n_chunks - 1)
        def _():
            h1_ref[...] = h_st[...].reshape(n_seg, B_HPG, B_HEADDIM, B_STATE)


def ssd_mixer(grp, x, mod, state, w_in, conv_w, conv_b, dt_bias, a_log, d_skip, norm_g, w_out):
    st_h, st_conv = state
    n_main = B_INNER + B_CONV_DIM
    wb = w_in.astype(BF16)
    group_lanes = lambda v: jnp.pad(v.reshape(-1, B_GROUPS, B_HPG), ((0, 0), (0, 0), (0, LANES - B_HPG))).reshape(
        -1, B_GROUPS * LANES)
    p_main, dt_raw = norm_matmul(grp, x, mod, 1, 0, [wb[:, :n_main], group_lanes(wb[:, n_main:])])
    n_seg, seg, n_chunks, rows = grp.S, grp.L, grp.n_chunks, grp.R
    xo, bo, co = B_INNER // B_GW, (2 * B_INNER) // B_STATE, (2 * B_INNER + B_GN) // B_STATE
    cxo, cbo, cco = 0, B_INNER // B_STATE, (B_INNER + B_GN) // B_STATE
    row = lambda b, g, c: b * n_chunks + c
    hist = _pad_history(st_conv)
    scratch = [pltpu.VMEM((n_seg, SUBLANES + seg, B_GW), F32), pltpu.VMEM((n_seg, SUBLANES + seg, B_STATE), F32),
               pltpu.VMEM((n_seg, SUBLANES + seg, B_STATE), F32)]
    if n_chunks > 1:
        scratch.append(pltpu.VMEM((n_seg, B_GW, B_STATE), F32))
    state_spec = pl.BlockSpec((n_seg, B_HPG, B_HEADDIM, B_STATE), lambda b, g, c: (b, g, 0, 0))
    y, h1 = pl.pallas_call(
        functools.partial(_ssd_kernel, n_seg=n_seg, seg=seg, n_chunks=n_chunks),
        grid=(grp.n_blocks, B_GROUPS, n_chunks),
        in_specs=[
            pl.BlockSpec((rows, B_GW), lambda b, g, c: (row(b, g, c), g)),
            pl.BlockSpec((rows, B_GW), lambda b, g, c: (row(b, g, c), xo + g)),
            pl.BlockSpec((rows, B_STATE), lambda b, g, c: (row(b, g, c), bo + g)),
            pl.BlockSpec((rows, B_STATE), lambda b, g, c: (row(b, g, c), co + g)),
            pl.BlockSpec((rows, LANES), lambda b, g, c: (row(b, g, c), g)),
            pl.BlockSpec((n_seg, SUBLANES, B_GW), lambda b, g, c: (b, 0, cxo + g)),
            pl.BlockSpec((n_seg, SUBLANES, B_STATE), lambda b, g, c: (b, 0, cbo + g)),
            pl.BlockSpec((n_seg, SUBLANES, B_STATE), lambda b, g, c: (b, 0, cco + g)),
            state_spec,
            pl.BlockSpec((CONV_W, B_GW), lambda b, g, c: (0, cxo + g)),
            pl.BlockSpec((1, B_GW), lambda b, g, c: (0, cxo + g)),
            pl.BlockSpec((CONV_W, B_STATE), lambda b, g, c: (0, cbo + g)),
            pl.BlockSpec((1, B_STATE), lambda b, g, c: (0, cbo + g)),
            pl.BlockSpec((CONV_W, B_STATE), lambda b, g, c: (0, cco + g)),
            pl.BlockSpec((1, B_STATE), lambda b, g, c: (0, cco + g)),
            pl.BlockSpec((1, LANES), lambda b, g, c: (0, g)),
            pl.BlockSpec((1, LANES), lambda b, g, c: (0, g)),
            pl.BlockSpec((1, B_GW), lambda b, g, c: (0, g)),
            pl.BlockSpec((1, B_GW), lambda b, g, c: (0, g)),
        ],
        out_specs=[pl.BlockSpec((rows, B_GW), lambda b, g, c: (row(b, g, c), g)), state_spec],
        out_shape=[jax.ShapeDtypeStruct((grp.N, B_INNER), F32), jax.ShapeDtypeStruct(st_h.shape, F32)],
        scratch_shapes=scratch,
        compiler_params=_cparams("arbitrary", "arbitrary", "arbitrary"),
        name="ssd_chunk",
    )(p_main, p_main, p_main, p_main, dt_raw, hist, hist, hist, st_h,
      conv_w, conv_b.reshape(1, -1), conv_w, conv_b.reshape(1, -1), conv_w, conv_b.reshape(1, -1),
      group_lanes(dt_bias[None, :]), group_lanes(a_log[None, :]),
      jnp.repeat(d_skip, B_HEADDIM)[None, :], norm_g[None, :])
    x_new = matmul_residual(grp, y, w_out.astype(BF16), x, mod, 2)
    conv_new = p_main.reshape(grp.B, grp.T, n_main)[:, grp.T - (CONV_W - 1):, B_INNER:]
    return x_new, (h1, conv_new)


C_Q = C_HEADS * C_HD
C_KV = C_KV_HEADS * C_HD


def _rope_kernel(p_ref, cos_ref, sin_ref, qg_ref, kg_ref, q_ref, k_ref):
    tm = p_ref.shape[0]
    lane = lax.broadcasted_iota(I32, (tm, LANES), 1)
    low_head = lane < C_HD
    first_half = (lane & (C_HD - 1)) < C_HD // 2
    cos, sin = cos_ref[...], sin_ref[...]

    def norm_rope(xb, gain):
        sq = xb * xb
        s_lo = jnp.sum(jnp.where(low_head, sq, 0.0), axis=1, keepdims=True)
        s_hi = jnp.sum(jnp.where(low_head, 0.0, sq), axis=1, keepdims=True)
        ms = jnp.where(low_head, s_lo, s_hi) * (1.0 / C_HD)
        xn = xb * lax.rsqrt(ms + EPS) * gain
        partner = jnp.where(first_half, pltpu.roll(xn, LANES - C_HD // 2, axis=1), pltpu.roll(xn, C_HD // 2, axis=1))
        return xn * cos + partner * sin

    for j in range(C_Q // LANES):
        q_ref[:, j * LANES:(j + 1) * LANES] = norm_rope(p_ref[:, j * LANES:(j + 1) * LANES], qg_ref[...])
    for j in range(C_KV // LANES):
        k_ref[:, j * LANES:(j + 1) * LANES] = norm_rope(p_ref[:, C_Q + j * LANES:C_Q + (j + 1) * LANES], kg_ref[...])


def _swa_kernel(q_ref, k0_ref, k1_ref, v0_ref, v1_ref, sink_ref, o_ref, *, n_units, tq, blocks_per_seq):
    i = pl.program_id(0)
    rows = C_GROUP * tq
    t = lax.broadcasted_iota(I32, (rows, 2 * WINDOW), 0) & (tq - 1)
    s = lax.broadcasted_iota(I32, (rows, 2 * WINDOW), 1)
    valid = (s >= t) & (s <= t + WINDOW)
    if blocks_per_seq:
        valid = valid & (s >= jnp.where(lax.rem(i, blocks_per_seq) == 0, WINDOW, 0))

    def with_past(past, new):
        if tq < WINDOW:
            new = jnp.concatenate([new, jnp.zeros((WINDOW - tq, new.shape[1]), F32)], axis=0)
        return jnp.concatenate([past, new], axis=0)

    for u in range(n_units):
        kk = with_past(k0_ref[u], k1_ref[u])
        vv = with_past(v0_ref[u], v1_ref[u])
        qu = q_ref[u]
        outs = []
        for kh in range(C_KV_HEADS):
            kc = kk[:, kh * C_HD:(kh + 1) * C_HD].astype(BF16)
            vc = vv[:, kh * C_HD:(kh + 1) * C_HD].astype(BF16)
            qs = jnp.concatenate([qu[:, (kh * C_GROUP + j) * C_HD:(kh * C_GROUP + j + 1) * C_HD]
                                  for j in range(C_GROUP)], axis=0).astype(BF16)
            logits = jnp.where(valid, _dot_nt(qs, kc) * (C_HD ** -0.5), NEG_INF)
            sink = sink_ref[kh][:, 0:1]
            mx = jnp.maximum(jnp.max(logits, axis=1, keepdims=True), sink)
            pr = jnp.exp(logits - mx)
            den = jnp.sum(pr, axis=1, keepdims=True) + jnp.exp(sink - mx)
            o = _dot(pr.astype(BF16), vc) / den
            outs += [o[j * tq:(j + 1) * tq] for j in range(C_GROUP)]
        o_ref[u] = jnp.concatenate(outs, axis=1)


def swa_mixer(grp, x, mod, cache, pos0, w_in, q_g, k_g, sinks, w_out):
    (p,) = norm_matmul(grp, x, mod, 1, 0, [w_in.astype(BF16)])
    half = C_HD // 2
    inv = ROPE_THETA ** (-jnp.arange(half, dtype=F32) / half)
    ang = (pos0 + jnp.arange(grp.T)).astype(F32)[:, None] * inv[None, :]
    cos, sin = jnp.cos(ang), jnp.sin(ang)
    per_token = lambda a: jnp.tile(a, (grp.B, 1))
    cos_t = per_token(jnp.tile(cos, (1, LANES // half)))
    sin_t = per_token(jnp.tile(jnp.concatenate([-sin, sin], axis=1), (1, LANES // C_HD)))
    tm = grp.tm
    gains = lambda g: jnp.tile(g, LANES // C_HD)[None, :]
    qr, kr = pl.pallas_call(
        _rope_kernel,
        grid=(grp.N // tm,),
        in_specs=[
            pl.BlockSpec((tm, C_Q + 2 * C_KV), lambda i: (i, 0)),
            pl.BlockSpec((tm, LANES), lambda i: (i, 0)),
            pl.BlockSpec((tm, LANES), lambda i: (i, 0)),
            pl.BlockSpec((1, LANES), lambda i: (0, 0)),
            pl.BlockSpec((1, LANES), lambda i: (0, 0)),
        ],
        out_specs=[pl.BlockSpec((tm, C_Q), lambda i: (i, 0)), pl.BlockSpec((tm, C_KV), lambda i: (i, 0))],
        out_shape=[jax.ShapeDtypeStruct((grp.N, C_Q), F32), jax.ShapeDtypeStruct((grp.N, C_KV), F32)],
        compiler_params=_cparams("arbitrary"),
        name="qk_norm_rope",
    )(p, cos_t, sin_t, gains(q_g), gains(k_g))

    v_col = (C_Q + C_KV) // C_KV
    if cache is None:
        assert grp.T % WINDOW == 0
        tq, n_units, blocks_per_seq = WINDOW, 1, grp.T // WINDOW
        n_steps = grp.N // tq
        k3 = kr.reshape(n_steps, tq, C_KV)
        p3 = p.reshape(n_steps, tq, C_Q + 2 * C_KV)
        k_args = (k3, k3, p3, p3)
        prev = lambda i: (jnp.maximum(i - 1, 0), 0, 0)
        k_specs = [
            pl.BlockSpec((1, tq, C_KV), prev),
            pl.BlockSpec((1, tq, C_KV), lambda i: (i, 0, 0)),
            pl.BlockSpec((1, tq, C_KV), lambda i: (jnp.maximum(i - 1, 0), 0, v_col)),
            pl.BlockSpec((1, tq, C_KV), lambda i: (i, 0, v_col)),
        ]
    else:
        tq, n_units, blocks_per_seq = grp.T, SUBLANES, 0
        n_steps = grp.B // n_units
        k_args = (cache[0].reshape(grp.B, WINDOW, C_KV), kr.reshape(grp.B, tq, C_KV),
                  cache[1].reshape(grp.B, WINDOW, C_KV), p.reshape(grp.B, tq, C_Q + 2 * C_KV))
        k_specs = [
            pl.BlockSpec((n_units, WINDOW, C_KV), lambda i: (i, 0, 0)),
            pl.BlockSpec((n_units, tq, C_KV), lambda i: (i, 0, 0)),
            pl.BlockSpec((n_units, WINDOW, C_KV), lambda i: (i, 0, 0)),
            pl.BlockSpec((n_units, tq, C_KV), lambda i: (i, 0, v_col)),
        ]
    sink_rows = jnp.broadcast_to(sinks.reshape(C_KV_HEADS, C_GROUP, 1, 1),
                                 (C_KV_HEADS, C_GROUP, tq, LANES)).reshape(C_KV_HEADS, C_GROUP * tq, LANES)
    o = pl.pallas_call(
        functools.partial(_swa_kernel, n_units=n_units, tq=tq, blocks_per_seq=blocks_per_seq),
        grid=(n_steps,),
        in_specs=[pl.BlockSpec((n_units, tq, C_Q), lambda i: (i, 0, 0))] + k_specs
        + [pl.BlockSpec(sink_rows.shape, lambda i: (0, 0, 0))],
        out_specs=pl.BlockSpec((n_units, tq, C_Q), lambda i: (i, 0, 0)),
        out_shape=jax.ShapeDtypeStruct((grp.N // tq, tq, C_Q), F32),
        compiler_params=_cparams("arbitrary"),
        name="swa_attention",
    )(qr.reshape(grp.N // tq, tq, C_Q), *k_args, sink_rows)
    x_new = matmul_residual(grp, o.reshape(grp.N, C_Q), w_out.astype(BF16), x, mod, 2)
    k_new = kr.reshape(grp.B, grp.T, C_KV_HEADS, C_HD)
    v_new = p[:, C_Q + C_KV:].reshape(grp.B, grp.T, C_KV_HEADS, C_HD)
    if cache is None:
        return x_new, (k_new[:, grp.T - WINDOW:], v_new[:, grp.T - WINDOW:])
    return x_new, (jnp.concatenate([cache[0][:, grp.T:], k_new], axis=1),
                   jnp.concatenate([cache[1][:, grp.T:], v_new], axis=1))


def kernel(x_prompt, x_sample, c_prompt, c_sample, state_mlstm_C, state_mlstm_n, state_mlstm_m, state_mlstm_conv,
           state_ssm, state_ssm_conv, cache_swa_k, cache_swa_v, ada_w, ada_b, mlstm_w_in, mlstm_conv_w,
           mlstm_conv_b, mlstm_gate_b, mlstm_norm_g, mlstm_w_out, ssd_w_in, ssd_conv_w, ssd_conv_b, ssd_dt_bias,
           ssd_a_log, ssd_d_skip, ssd_norm_g, ssd_w_out, swa_w_in, swa_q_norm_g, swa_k_norm_g, swa_sinks,
           swa_w_out, moe_w_router, moe_b_router, moe_w1, moe_b1, moe_w2, moe_b2):
    groups = [Group(*x_prompt.shape[:2]), Group(*x_sample.shape[:2])]
    n_prompt = groups[0].B
    xs = [x_prompt.reshape(-1, D_MODEL), x_sample.reshape(-1, D_MODEL)]
    mod_all = ada_modulation(jnp.concatenate([c_prompt, c_sample], axis=0), ada_w, ada_b)
    fresh = lambda s: jnp.zeros((n_prompt,) + s.shape[2:], F32)
    new = [[[] for _ in range(8)] for _ in groups]
    for layer in range(DEPTH):
        kind, j = layer % 3, layer // 3
        mods = [groups[0].expand_mod(mod_all[layer, :n_prompt]), groups[1].expand_mod(mod_all[layer, n_prompt:])]
        for gi, grp in enumerate(groups):
            if kind == 0:
                state = (state_mlstm_n, state_mlstm_m, state_mlstm_conv)
                state = tuple(s[j] if gi else fresh(s) for s in state)
                state = ((state_mlstm_C, j) if gi else (fresh(state_mlstm_C)[None], 0),) + state
                xs[gi], st = mlstm_mixer(grp, xs[gi], mods[gi], state, mlstm_w_in[j], mlstm_conv_w[j], mlstm_conv_b[j],
                                         mlstm_gate_b[j], mlstm_norm_g[j], mlstm_w_out[j])
                first = 0
            elif kind == 1:
                state = tuple(s[j] if gi else fresh(s) for s in (state_ssm, state_ssm_conv))
                xs[gi], st = ssd_mixer(grp, xs[gi], mods[gi], state, ssd_w_in[j], ssd_conv_w[j], ssd_conv_b[j],
                                       ssd_dt_bias[j], ssd_a_log[j], ssd_d_skip[j], ssd_norm_g[j], ssd_w_out[j])
                first = 4
            else:
                cache = (cache_swa_k[j], cache_swa_v[j]) if gi else None
                xs[gi], st = swa_mixer(grp, xs[gi], mods[gi], cache, PAST_LEN if gi else 0, swa_w_in[j],
                                       swa_q_norm_g[j], swa_k_norm_g[j], swa_sinks[j], swa_w_out[j])
                first = 6
            for offset, s in enumerate(st):
                new[gi][first + offset].append(s)
        xs = moe_layer(groups, xs, mods, layer, moe_w_router, moe_b_router, moe_w1, moe_b1, moe_w2, moe_b2)
    outs = [xs[0].reshape(x_prompt.shape), xs[1].reshape(x_sample.shape)]
    for slot in range(8):
        outs += [jnp.stack(new[0][slot]), jnp.stack(new[1][slot])]
    return tuple(outs)
```

```python
import functools
import math

import jax
import jax.numpy as jnp
from jax import lax
from jax.experimental import pallas as pl
from jax.experimental.pallas import tpu as pltpu

F32 = jnp.float32
BF16 = jnp.bfloat16
I32 = jnp.int32

D_MODEL = 1024
DEPTH = 4
PAST_LEN = 8192
EPS = 1e-6
CONV_W = 4
A_HEADS = 4
A_DK = D_MODEL // 8
A_DV = D_MODEL // A_HEADS
A_QK = A_HEADS * A_DK
A_V = A_HEADS * A_DV
B_INNER = 2 * D_MODEL
B_HEADDIM = 64
B_HEADS = B_INNER // B_HEADDIM
B_STATE = 128
B_GROUPS = 8
B_GN = B_GROUPS * B_STATE
B_CONV_DIM = B_INNER + 2 * B_GN
B_HPG = B_HEADS // B_GROUPS
B_GW = B_HPG * B_HEADDIM
C_HEADS = 16
C_KV_HEADS = 4
C_GROUP = C_HEADS // C_KV_HEADS
C_HD = 64
WINDOW = 128
ROPE_THETA = 10000.0
N_EXPERTS = 32
TOP_K = 4
D_FF = D_MODEL
SWIGLU_LIMIT = 7.0
SWIGLU_ALPHA = 1.702

LANES = 128
SUBLANES = 8
VMEM_LIMIT_BYTES = 56 * 1024 * 1024

ROW_TILE = 256
SEQ_CHUNK = 256
SHORT_BLOCK = 128
MOE_TILE = 512
FF_CHUNK = 1024
NEG_INF = float("-inf")
ROW_AS_TILE = (SUBLANES, D_MODEL // SUBLANES)


def _cparams(*sem):
    return pltpu.CompilerParams(dimension_semantics=sem, vmem_limit_bytes=VMEM_LIMIT_BYTES)


def _silu(x):
    return x * jax.nn.sigmoid(x)


def _dot(a, b):
    return jnp.dot(a, b, preferred_element_type=F32)


def _dot_nt(a, b):
    return lax.dot_general(a, b, (((1,), (1,)), ((), ())), preferred_element_type=F32)


class Group:
    def __init__(self, B, T):
        self.B, self.T = B, T
        self.N = B * T
        self.long = T % SEQ_CHUNK == 0
        if self.long:
            self.S, self.L = 1, SEQ_CHUNK
        else:
            assert T == SUBLANES and self.N % SHORT_BLOCK == 0
            self.S, self.L = SHORT_BLOCK // T, T
        self.R = self.S * self.L
        self.n_blocks = B // self.S
        self.n_chunks = T // self.L
        self.tm = min(ROW_TILE, self.N)
        assert self.N % self.tm == 0 and (not self.long or T % self.tm == 0)

    def expand_mod(self, mod):
        if self.long:
            return mod.reshape(self.B, 1, 6 * D_MODEL)
        return jnp.repeat(mod, self.T, axis=0)

    def mod_spec(self, j, rows, row_block_fn):
        if self.long:
            per_seq = self.T // rows
            return pl.BlockSpec((None, 1, D_MODEL), lambda *g: (row_block_fn(*g) // per_seq, 0, j))
        return pl.BlockSpec((rows, D_MODEL), lambda *g: (row_block_fn(*g), j))


def _ada_kernel(c_ref, w_ref, b_ref, o_ref):
    cs = _silu(c_ref[...])
    o_ref[...] = _dot(cs.astype(BF16), w_ref[...].astype(BF16)) + b_ref[...]


def ada_modulation(c_all, ada_w, ada_b):
    rows = c_all.shape[0]
    tn = 1536
    n_out = 6 * D_MODEL
    return pl.pallas_call(
        _ada_kernel,
        grid=(DEPTH, n_out // tn),
        in_specs=[
            pl.BlockSpec((rows, D_MODEL), lambda l, j: (0, 0)),
            pl.BlockSpec((None, D_MODEL, tn), lambda l, j: (l, 0, j)),
            pl.BlockSpec((None, 1, tn), lambda l, j: (l, 0, j)),
        ],
        out_specs=pl.BlockSpec((None, rows, tn), lambda l, j: (l, 0, j)),
        out_shape=jax.ShapeDtypeStruct((DEPTH, rows, n_out), F32),
        compiler_params=_cparams("arbitrary", "arbitrary"),
        name="ada_modulation",
    )(c_all, ada_w, ada_b.reshape(DEPTH, 1, n_out))


def _modulated_norm(x, sc, sh):
    ms = jnp.mean(x * x, axis=-1, keepdims=True)
    return (x * lax.rsqrt(ms + EPS)) * (1.0 + sc) + sh


def _norm_mm_kernel(x_ref, sc_ref, sh_ref, *refs, n_w):
    h = _modulated_norm(x_ref[...], sc_ref[...], sh_ref[...]).astype(BF16)
    for w_ref, o_ref in zip(refs[:n_w], refs[n_w:]):
        o_ref[...] = _dot(h, w_ref[...])


def norm_matmul(grp, x, mod, j_scale, j_shift, weights):
    tm = grp.tm
    n_w = len(weights)
    in_specs = [
        pl.BlockSpec((tm, D_MODEL), lambda i: (i, 0)),
        grp.mod_spec(j_scale, tm, lambda i: i),
        grp.mod_spec(j_shift, tm, lambda i: i),
    ] + [pl.BlockSpec(w.shape, lambda i: (0, 0)) for w in weights]
    out_specs = [pl.BlockSpec((tm, w.shape[1]), lambda i: (i, 0)) for w in weights]
    out_shape = [jax.ShapeDtypeStruct((grp.N, w.shape[1]), F32) for w in weights]
    return pl.pallas_call(
        functools.partial(_norm_mm_kernel, n_w=n_w),
        grid=(grp.N // tm,),
        in_specs=in_specs,
        out_specs=out_specs,
        out_shape=out_shape,
        compiler_params=_cparams("arbitrary"),
        name="norm_matmul",
    )(x, mod, mod, *weights)


def _mm_res_kernel(y_ref, w_ref, x_ref, g_ref, o_ref):
    o_ref[...] = x_ref[...] + g_ref[...] * _dot(y_ref[...].astype(BF16), w_ref[...])


def matmul_residual(grp, y, w, x, mod, j_gate):
    tm = grp.tm
    k = y.shape[1]
    return pl.pallas_call(
        _mm_res_kernel,
        grid=(grp.N // tm,),
        in_specs=[
            pl.BlockSpec((tm, k), lambda i: (i, 0)),
            pl.BlockSpec(w.shape, lambda i: (0, 0)),
            pl.BlockSpec((tm, D_MODEL), lambda i: (i, 0)),
            grp.mod_spec(j_gate, tm, lambda i: i),
        ],
        out_specs=pl.BlockSpec((tm, D_MODEL), lambda i: (i, 0)),
        out_shape=jax.ShapeDtypeStruct((grp.N, D_MODEL), F32),
        compiler_params=_cparams("arbitrary"),
        name="matmul_residual",
    )(y, w, x, mod)


def _router_kernel(x_ref, sc_ref, sh_ref, wr_ref, br_ref, cnt_in_ref,
                   h_ref, ri_ref, rg_ref, cnt_out_ref, cnt_scr):
    i = pl.program_id(0)
    tm = x_ref.shape[0]

    @pl.when(i == 0)
    def _():
        cnt_scr[...] = cnt_in_ref[...]

    h = _modulated_norm(x_ref[...], sc_ref[...], sh_ref[...])
    h_ref[...] = h.reshape(tm, *ROW_AS_TILE)
    logits = jnp.dot(h, wr_ref[...], preferred_element_type=F32,
                     precision=lax.Precision.HIGHEST) + br_ref[...]
    lane = lax.broadcasted_iota(I32, (tm, LANES), 1)
    lane_f = lane.astype(F32)
    work = jnp.where(lane < N_EXPERTS, logits, NEG_INF)
    top_v, top_sel, top_i = [], [], []
    for _ in range(TOP_K):
        mx = jnp.max(work, axis=1, keepdims=True)
        idx = jnp.min(jnp.where(work == mx, lane_f, float(LANES)), axis=1, keepdims=True)
        sel = lane_f == idx
        work = jnp.where(sel, NEG_INF, work)
        top_v.append(mx)
        top_sel.append(sel)
        top_i.append(idx)
    ex = [jnp.exp(v - top_v[0]) for v in top_v]
    inv = 1.0 / (ex[0] + ex[1] + ex[2] + ex[3])
    chosen = jnp.zeros((tm, LANES), F32)
    for sel in top_sel:
        chosen = jnp.where(sel, 1.0, chosen)
    r = lax.broadcasted_iota(I32, (tm, tm), 0)
    c = lax.broadcasted_iota(I32, (tm, tm), 1)
    before = jnp.where(c < r, 1.0, 0.0).astype(BF16)
    rank_all = _dot(before, chosen.astype(BF16)) + cnt_scr[...]
    ri = jnp.zeros((tm, LANES), I32)
    rg = jnp.zeros((tm, LANES), F32)
    for k in range(TOP_K):
        rank_k = jnp.sum(jnp.where(top_sel[k], rank_all, 0.0), axis=1, keepdims=True)
        ri = jnp.where(lane == k, top_i[k].astype(I32), ri)
        ri = jnp.where(lane == TOP_K + k, rank_k.astype(I32), ri)
        rg = jnp.where(lane == k, ex[k] * inv, rg)
    ri_ref[...] = ri
    rg_ref[...] = rg
    cnt_scr[...] = cnt_scr[...] + jnp.sum(chosen, axis=0, keepdims=True)
    cnt_out_ref[...] = cnt_scr[...]


def moe_router(grp, x, mod, w_r, b_r, cnt_in):
    tm = grp.tm
    return pl.pallas_call(
        _router_kernel,
        grid=(grp.N // tm,),
        in_specs=[
            pl.BlockSpec((tm, D_MODEL), lambda i: (i, 0)),
            grp.mod_spec(4, tm, lambda i: i),
            grp.mod_spec(3, tm, lambda i: i),
            pl.BlockSpec((D_MODEL, LANES), lambda i: (0, 0)),
            pl.BlockSpec((1, LANES), lambda i: (0, 0)),
            pl.BlockSpec((1, LANES), lambda i: (0, 0)),
        ],
        out_specs=[
            pl.BlockSpec((tm,) + ROW_AS_TILE, lambda i: (i, 0, 0)),
            pl.BlockSpec((tm, LANES), lambda i: (i, 0)),
            pl.BlockSpec((tm, LANES), lambda i: (i, 0)),
            pl.BlockSpec((1, LANES), lambda i: (0, 0)),
        ],
        out_shape=[
            jax.ShapeDtypeStruct((grp.N,) + ROW_AS_TILE, F32),
            jax.ShapeDtypeStruct((grp.N, LANES), I32),
            jax.ShapeDtypeStruct((grp.N, LANES), F32),
            jax.ShapeDtypeStruct((1, LANES), F32),
        ],
        scratch_shapes=[pltpu.VMEM((1, LANES), F32)],
        compiler_params=_cparams("arbitrary"),
        name="moe_router",
    )(x, mod, mod, w_r, b_r, cnt_in)


def _row_copy(src_ref, src_row, dst_ref, dst_row, sem):
    return pltpu.make_async_copy(src_ref.at[src_row], dst_ref.at[dst_row], sem)


DMA_UNROLL = 8


def _dispatch_kernel(pos_ref, cnt_ref, off_ref, nv_ref, *refs, tiles, n_sorted_tiles):
    n_groups = len(tiles)
    h_refs = refs[:n_groups]
    xs_ref, zero_scr, sem = refs[n_groups:]
    i = pl.program_id(0)
    tm = h_refs[0].shape[0]
    base = (i - 1) * (tm * TOP_K)

    def zero_tile(j):
        return pltpu.make_async_copy(zero_scr, xs_ref.at[pl.ds(j * MOE_TILE, MOE_TILE)], sem)

    def for_each_partial_tile(action):
        def body(e, carry):
            n = cnt_ref[e]

            @pl.when((n & (MOE_TILE - 1)) != 0)
            def _():
                action(zero_tile((off_ref[e] + n) // MOE_TILE))

            return carry

        lax.fori_loop(0, N_EXPERTS, body, 0)

        def tail(j, carry):
            action(zero_tile(j))
            return carry

        lax.fori_loop(nv_ref[0], n_sorted_tiles, tail, 0)

    @pl.when(i == 0)
    def _():
        zero_scr[...] = jnp.zeros(zero_scr.shape, F32)
        for_each_partial_tile(lambda cp: cp.start())
        for_each_partial_tile(lambda cp: cp.wait())

    def scatter_tile(h_ref):
        def start(t, carry):
            for k in range(TOP_K):
                _row_copy(h_ref, t, xs_ref, pos_ref[base + t * TOP_K + k], sem).start(priority=k % 2)
            return carry

        lax.fori_loop(0, tm, start, 0, unroll=DMA_UNROLL)
        for _ in range(TOP_K):
            pltpu.make_async_copy(h_ref, xs_ref.at[pl.ds(0, tm)], sem).wait()

    first = 1
    for h_ref, n in zip(h_refs, tiles):
        pl.when((i >= first) & (i < first + n))(functools.partial(scatter_tile, h_ref))
        first += n


def moe_dispatch(groups, hs, pos_all, counts, row_off, n_valid, n_sorted_tiles):
    tm = groups[0].tm
    assert all(g.tm == tm for g in groups)
    tiles = tuple(g.N // tm for g in groups)
    starts = [1 + sum(tiles[:gi]) for gi in range(len(tiles))]

    def h_spec(first, n):
        return pl.BlockSpec((tm,) + ROW_AS_TILE, lambda i, *_: (jnp.clip(i - first, 0, n - 1), 0, 0))

    return pl.pallas_call(
        functools.partial(_dispatch_kernel, tiles=tiles, n_sorted_tiles=n_sorted_tiles),
        grid_spec=pltpu.PrefetchScalarGridSpec(
            num_scalar_prefetch=4,
            grid=(sum(tiles) + 1,),
            in_specs=[h_spec(first, n) for first, n in zip(starts, tiles)],
            out_specs=pl.BlockSpec(memory_space=pl.ANY),
            scratch_shapes=[pltpu.VMEM((MOE_TILE,) + ROW_AS_TILE, F32), pltpu.SemaphoreType.DMA],
        ),
        out_shape=jax.ShapeDtypeStruct((n_sorted_tiles * MOE_TILE,) + ROW_AS_TILE, F32),
        compiler_params=_cparams("arbitrary"),
        name="moe_dispatch",
    )(pos_all, counts, row_off, n_valid, *hs)


def _combine_kernel(pos_ref, ys_ref, rg_ref, x_ref, g_ref, o_ref, ybuf, sems):
    i = pl.program_id(0)
    n = pl.num_programs(0)
    tm = x_ref.shape[0]

    def gather_tile(tile, slot):
        base = tile * (tm * TOP_K)

        def start(t, carry):
            for k in range(TOP_K):
                _row_copy(ys_ref, pos_ref[base + t * TOP_K + k], ybuf.at[slot, k], t,
                          sems.at[slot]).start(priority=k % 2)
            return carry

        lax.fori_loop(0, tm, start, 0, unroll=DMA_UNROLL)

    slot = lax.rem(i, 2)

    @pl.when(i == 0)
    def _():
        gather_tile(0, 0)

    @pl.when(i + 1 < n)
    def _():
        gather_tile(i + 1, 1 - slot)

    for k in range(TOP_K):
        pltpu.make_async_copy(ys_ref.at[pl.ds(0, tm)], ybuf.at[slot, k], sems.at[slot]).wait()
    rg = rg_ref[...]
    acc = rg[:, 0:1] * ybuf[slot, 0].reshape(tm, D_MODEL)
    for k in range(1, TOP_K):
        acc = acc + rg[:, k:k + 1] * ybuf[slot, k].reshape(tm, D_MODEL)
    o_ref[...] = x_ref[...] + g_ref[...] * acc


def moe_combine(grp, ys, pos, route_g, x, mod):
    tm = grp.tm
    return pl.pallas_call(
        _combine_kernel,
        grid_spec=pltpu.PrefetchScalarGridSpec(
            num_scalar_prefetch=1,
            grid=(grp.N // tm,),
            in_specs=[
                pl.BlockSpec(memory_space=pl.ANY),
                pl.BlockSpec((tm, LANES), lambda i, p: (i, 0)),
                pl.BlockSpec((tm, D_MODEL), lambda i, p: (i, 0)),
                grp.mod_spec(5, tm, lambda i, p: i),
            ],
            out_specs=pl.BlockSpec((tm, D_MODEL), lambda i, p: (i, 0)),
            scratch_shapes=[pltpu.VMEM((2, TOP_K, tm) + ROW_AS_TILE, F32), pltpu.SemaphoreType.DMA((2,))],
        ),
        out_shape=jax.ShapeDtypeStruct((grp.N, D_MODEL), F32),
        compiler_params=_cparams("arbitrary"),
        name="moe_combine",
    )(pos, ys, route_g, x, mod)


def _experts_kernel(te_ref, tb_ref, nv_ref, x_ref, w1_ref, b1_ref, w2_ref, b2_ref, y_ref, w1_scr, w2_scr):
    i = pl.program_id(0)

    @pl.when(i < nv_ref[0])
    def _():
        prev = te_ref[jnp.maximum(i - 1, 0)]

        @pl.when((i == 0) | (te_ref[i] != prev))
        def _():
            w1_scr[...] = w1_ref[...].astype(BF16)
            w2_scr[...] = w2_ref[...].astype(BF16)

        xb = x_ref[...].reshape(MOE_TILE, D_MODEL).astype(BF16)
        y = b2_ref[...]
        for j in range(D_FF // FF_CHUNK):
            cols = slice(j * FF_CHUNK, (j + 1) * FF_CHUNK)
            ucols = slice(D_FF + j * FF_CHUNK, D_FF + (j + 1) * FF_CHUNK)
            g = jnp.minimum(_dot(xb, w1_scr[:, cols]) + b1_ref[:, cols], SWIGLU_LIMIT)
            u = jnp.clip(_dot(xb, w1_scr[:, ucols]) + b1_ref[:, ucols], -SWIGLU_LIMIT, SWIGLU_LIMIT)
            act = (u + 1.0) * (g * jax.nn.sigmoid(SWIGLU_ALPHA * g))
            y = y + _dot(act.astype(BF16), w2_scr[cols, :])
        y_ref[...] = y.reshape(y_ref.shape)

    @pl.when(i >= nv_ref[0])
    def _():
        y_ref[...] = jnp.zeros(y_ref.shape, F32)


def moe_experts(xs, tile_expert, tile_block, n_valid, w1, b1, w2, b2, layer):
    n_tiles = tile_expert.shape[0]
    tm = MOE_TILE
    return pl.pallas_call(
        _experts_kernel,
        grid_spec=pltpu.PrefetchScalarGridSpec(
            num_scalar_prefetch=3,
            grid=(n_tiles,),
            in_specs=[
                pl.BlockSpec((tm,) + ROW_AS_TILE, lambda i, te, tb, nv: (tb[i], 0, 0)),
                pl.BlockSpec((None, None, D_MODEL, 2 * D_FF), lambda i, te, tb, nv: (layer, te[i], 0, 0)),
                pl.BlockSpec((None, None, 1, 2 * D_FF), lambda i, te, tb, nv: (layer, te[i], 0, 0)),
                pl.BlockSpec((None, None, D_FF, D_MODEL), lambda i, te, tb, nv: (layer, te[i], 0, 0)),
                pl.BlockSpec((None, None, 1, D_MODEL), lambda i, te, tb, nv: (layer, te[i], 0, 0)),
            ],
            out_specs=pl.BlockSpec((tm,) + ROW_AS_TILE, lambda i, te, tb, nv: (i, 0, 0)),
            scratch_shapes=[pltpu.VMEM((D_MODEL, 2 * D_FF), BF16), pltpu.VMEM((D_FF, D_MODEL), BF16)],
        ),
        out_shape=jax.ShapeDtypeStruct(xs.shape, F32),
        compiler_params=_cparams("arbitrary"),
        name="moe_experts",
    )(tile_expert, tile_block, n_valid, xs, w1, b1.reshape(DEPTH, N_EXPERTS, 1, 2 * D_FF),
      w2, b2.reshape(DEPTH, N_EXPERTS, 1, D_MODEL))


def moe_layer(groups, xs_in, mods, layer, w_router, b_router, w1, b1, w2, b2):
    n_total = sum(g.N for g in groups)
    n_pairs = n_total * TOP_K
    n_tiles = -(-n_pairs // MOE_TILE) + N_EXPERTS
    w_r = jnp.pad(w_router[layer], ((0, 0), (0, LANES - N_EXPERTS)))
    b_r = jnp.pad(b_router[layer], (0, LANES - N_EXPERTS)).reshape(1, LANES)
    cnt = jnp.zeros((1, LANES), F32)
    routed = []
    for grp, x, mod in zip(groups, xs_in, mods):
        h, route_i, route_g, cnt = moe_router(grp, x, mod, w_r, b_r, cnt)
        routed.append((h, route_i, route_g))
    counts = cnt[0, :N_EXPERTS].astype(I32)
    tiles_per = (counts + MOE_TILE - 1) // MOE_TILE
    tile_end = jnp.cumsum(tiles_per)
    tile_start = tile_end - tiles_per
    n_valid = tile_end[-1:]
    tidx = jnp.minimum(jnp.arange(n_tiles, dtype=I32), n_valid - 1)
    tile_expert = jnp.sum((tile_end[None, :] <= tidx[:, None]).astype(I32), axis=1)
    row_off = tile_start * MOE_TILE
    positions = []
    for _, route_i, _ in routed:
        route_e, rank = route_i[:, :TOP_K], route_i[:, TOP_K:2 * TOP_K]
        onehot = route_e[:, :, None] == jnp.arange(N_EXPERTS, dtype=I32)[None, None, :]
        positions.append((jnp.sum(jnp.where(onehot, row_off[None, None, :], 0), axis=2) + rank).reshape(-1))
    xs = moe_dispatch(groups, [h for h, _, _ in routed], jnp.concatenate(positions), counts, row_off, n_valid,
                      n_tiles)
    ys = moe_experts(xs, tile_expert, tidx, n_valid, w1, b1, w2, b2, layer)
    return [moe_combine(grp, ys, pos, route_g, x, mod)
            for grp, x, mod, pos, (_, _, route_g) in zip(groups, xs_in, mods, positions, routed)]


def _seg_cumsum(x, seg, reverse=False):
    rows = x.shape[0]
    row = lax.broadcasted_iota(I32, x.shape, 0) & (seg - 1)
    sh = 1
    while sh < seg:
        if reverse:
            x = x + jnp.where(row + sh < seg, pltpu.roll(x, rows - sh, axis=0), 0.0)
        else:
            x = x + jnp.where(row >= sh, pltpu.roll(x, sh, axis=0), 0.0)
        sh *= 2
    return x


def _segment_mask(rows, seg):
    ti = lax.broadcasted_iota(I32, (rows, rows), 0)
    si = lax.broadcasted_iota(I32, (rows, rows), 1)
    mask = si <= ti
    if seg < rows:
        shift = seg.bit_length() - 1
        mask = mask & ((ti >> shift) == (si >> shift))
    return mask


def _causal_conv_silu(x, xp_scr, w_ref, b_ref, n_seg, seg, carry):
    rows, ch = x.shape
    xp_scr[:, SUBLANES:, :] = x.reshape(n_seg, seg, ch)
    acc = b_ref[...]
    for i in range(CONV_W):
        back = CONV_W - 1 - i
        xs = x if back == 0 else xp_scr[:, SUBLANES - back:SUBLANES - back + seg, :].reshape(rows, ch)
        acc = acc + xs * w_ref[i:i + 1, :]
    if carry:
        xp_scr[:, SUBLANES - (CONV_W - 1):SUBLANES, :] = xp_scr[:, SUBLANES + seg - (CONV_W - 1):SUBLANES + seg, :]
    return _silu(acc)


def _pad_history(buf):
    return jnp.pad(buf, ((0, 0), (SUBLANES - (CONV_W - 1), 0), (0, 0)))


def _mlstm_kernel(p_ref, gi_ref, gf_ref, hist_ref, c0_ref, n0_ref, m0_ref, cw_ref, cb_ref, gbi_ref, gbf_ref,
                  ng_ref, y_ref, c1_ref, n1_ref, m1_ref, xp_scr, *state_scr, n_seg, seg, n_chunks):
    rows = n_seg * seg
    chunk = pl.program_id(1)
    carried = n_chunks > 1
    if carried:
        c_st, n_st, m_st = state_scr

        @pl.when(chunk == 0)
        def _():
            c_st[...] = c0_ref[...]
            n_st[...] = n0_ref[...]
            m_st[...] = m0_ref[...]
            xp_scr[:, 0:SUBLANES, :] = hist_ref[...]

        c_in, n_in, m_in, c_out, n_out, m_out = c_st, n_st, m_st, c_st, n_st, m_st
    else:
        xp_scr[:, 0:SUBLANES, :] = hist_ref[...]
        c_in, n_in, m_in, c_out, n_out, m_out = c0_ref, n0_ref, m0_ref, c1_ref, n1_ref, m1_ref

    qk = _causal_conv_silu(p_ref[:, :2 * A_QK], xp_scr, cw_ref, cb_ref, n_seg, seg, carried)
    log_i = gi_ref[...] + gbi_ref[...]
    f_pre = gf_ref[...] + gbf_ref[...]
    log_f = jnp.minimum(f_pre, 0.0) - jnp.log1p(jnp.exp(-jnp.abs(f_pre)))
    a = _seg_cumsum(log_f, seg)
    a_rev = _seg_cumsum(log_f, seg, reverse=True)
    m0 = m_in[...]
    inter3 = a.reshape(n_seg, seg, LANES) + m0
    inter = inter3.reshape(rows, LANES)
    src3 = (a_rev - log_f + log_i).reshape(n_seg, seg, LANES)
    inter_end = inter3[:, seg - 1:seg, :]
    m_end = jnp.maximum(inter_end, jnp.max(src3, axis=1, keepdims=True))
    w_old = jnp.exp(inter_end - m_end)
    w_src = jnp.exp(src3 - m_end).reshape(rows, LANES)
    b_t = (a - log_i).T
    mask = _segment_mask(rows, seg)
    seg_of_lane = lax.broadcasted_iota(I32, (A_DK, rows), 1) >> (seg.bit_length() - 1)

    for h in range(A_HEADS):
        q = qk[:, h * A_DK:(h + 1) * A_DK] * (A_DK ** -0.5)
        k = qk[:, A_QK + h * A_DK:A_QK + (h + 1) * A_DK]
        vb = p_ref[:, 2 * A_QK + h * A_DV:2 * A_QK + (h + 1) * A_DV].astype(BF16)
        qb = q.astype(BF16)
        dmat = jnp.where(mask, a[:, h:h + 1] - b_t[h:h + 1, :], NEG_INF)
        m_tok = jnp.maximum(inter[:, h:h + 1], jnp.max(dmat, axis=1, keepdims=True))
        s = _dot_nt(qb, k.astype(BF16)) * jnp.exp(dmat - m_tok)
        w_inter = jnp.exp(inter[:, h:h + 1] - m_tok)
        if n_seg == 1:
            q_c = _dot(qb, c_in[0, h].astype(BF16))
        else:
            q_c = jnp.concatenate([_dot(q[g * seg:(g + 1) * seg], c_in[g, h]) for g in range(n_seg)], axis=0)
        n3 = n_in[:, h:h + 1, :]
        n_tok = jnp.broadcast_to(n3, (n_seg, seg, A_DK)).reshape(rows, A_DK)
        num = _dot(s.astype(BF16), vb) + w_inter * q_c
        den = jnp.sum(s, axis=1, keepdims=True) + w_inter * jnp.sum(q * n_tok, axis=1, keepdims=True)
        hv = num / jnp.maximum(jnp.abs(den), jnp.exp(-m_tok))
        hn = hv * lax.rsqrt(jnp.mean(hv * hv, axis=1, keepdims=True) + EPS) * ng_ref[h:h + 1, :]
        o = jax.nn.sigmoid(p_ref[:, 2 * A_QK + A_V + h * A_DV:2 * A_QK + A_V + (h + 1) * A_DV])
        y_ref[:, h * A_DV:(h + 1) * A_DV] = o * hn
        kw = k * w_src[:, h:h + 1]
        kw_t = kw.T
        for g in range(n_seg):
            kg = kw_t if n_seg == 1 else jnp.where(seg_of_lane == g, kw_t, 0.0)
            c_out[g, h] = w_old[g, :, h:h + 1] * c_in[g, h] + _dot(kg.astype(BF16), vb)
        n_out[:, h:h + 1, :] = (w_old[:, :, h:h + 1] * n3
                                + jnp.sum(kw.reshape(n_seg, seg, A_DK), axis=1, keepdims=True))
    m_out[...] = m_end

    if carried:
        @pl.when(chunk == n_chunks - 1)
        def _():
            c1_ref[...] = c_st[...]
            n1_ref[...] = n_st[...]
            m1_ref[...] = m_st[...]


def _lane_pad(w, cols=LANES):
    return jnp.pad(w, ((0, 0), (0, cols - w.shape[1])))


def mlstm_mixer(grp, x, mod, state, w_in, conv_w, conv_b, gate_b, norm_g, w_out):
    (st_c, c_layer), st_n, st_m, st_conv = state
    n_main = 2 * A_QK + 2 * A_V
    wb = w_in.astype(BF16)
    p_main, gate_i, gate_f = norm_matmul(
        grp, x, mod, 1, 0,
        [wb[:, :n_main], _lane_pad(wb[:, n_main:n_main + A_HEADS]), _lane_pad(wb[:, n_main + A_HEADS:])])
    n_seg, seg, n_chunks, rows = grp.S, grp.L, grp.n_chunks, grp.R
    row_map = lambda b, c: (b * n_chunks + c, 0)
    scratch = [pltpu.VMEM((n_seg, SUBLANES + seg, 2 * A_QK), F32)]
    if n_chunks > 1:
        scratch += [pltpu.VMEM((n_seg, A_HEADS, A_DK, A_DV), F32), pltpu.VMEM((n_seg, A_HEADS, A_DK), F32),
                    pltpu.VMEM((n_seg, 1, LANES), F32)]
    full = lambda shape: pl.BlockSpec(shape, lambda b, c: (0,) * len(shape))
    y, c1, n1, m1 = pl.pallas_call(
        functools.partial(_mlstm_kernel, n_seg=n_seg, seg=seg, n_chunks=n_chunks),
        grid=(grp.n_blocks, n_chunks),
        in_specs=[
            pl.BlockSpec((rows, n_main), row_map),
            pl.BlockSpec((rows, LANES), row_map),
            pl.BlockSpec((rows, LANES), row_map),
            pl.BlockSpec((n_seg, SUBLANES, 2 * A_QK), lambda b, c: (b, 0, 0)),
            pl.BlockSpec((None, n_seg, A_HEADS, A_DK, A_DV), lambda b, c: (c_layer, b, 0, 0, 0)),
            pl.BlockSpec((n_seg, A_HEADS, A_DK), lambda b, c: (b, 0, 0)),
            pl.BlockSpec((n_seg, 1, LANES), lambda b, c: (b, 0, 0)),
            full((CONV_W, 2 * A_QK)), full((1, 2 * A_QK)), full((1, LANES)), full((1, LANES)),
            full((A_HEADS, A_DV)),
        ],
        out_specs=[
            pl.BlockSpec((rows, A_V), row_map),
            pl.BlockSpec((n_seg, A_HEADS, A_DK, A_DV), lambda b, c: (b, 0, 0, 0)),
            pl.BlockSpec((n_seg, A_HEADS, A_DK), lambda b, c: (b, 0, 0)),
            pl.BlockSpec((n_seg, 1, LANES), lambda b, c: (b, 0, 0)),
        ],
        out_shape=[
            jax.ShapeDtypeStruct((grp.N, A_V), F32),
            jax.ShapeDtypeStruct(st_c.shape[1:], F32),
            jax.ShapeDtypeStruct(st_n.shape, F32),
            jax.ShapeDtypeStruct((grp.B, 1, LANES), F32),
        ],
        scratch_shapes=scratch,
        compiler_params=_cparams("arbitrary", "arbitrary"),
        name="mlstm_chunk",
    )(p_main, gate_i, gate_f, _pad_history(st_conv), st_c, st_n, _lane_pad(st_m)[:, None, :],
      conv_w, conv_b.reshape(1, -1), _lane_pad(gate_b[None, :A_HEADS]), _lane_pad(gate_b[None, A_HEADS:]), norm_g)
    x_new = matmul_residual(grp, y, w_out.astype(BF16), x, mod, 2)
    conv_new = p_main.reshape(grp.B, grp.T, n_main)[:, grp.T - (CONV_W - 1):, :2 * A_QK]
    return x_new, (c1, n1, m1[:, 0, :A_HEADS], conv_new)


def _softplus(x):
    return jnp.maximum(x, 0.0) + jnp.log1p(jnp.exp(-jnp.abs(x)))


def _ssd_kernel(z_ref, x_ref, b_ref, c_ref, dt_ref, hx_ref, hb_ref, hc_ref, h0_ref,
                cwx_ref, cbx_ref, cwb_ref, cbb_ref, cwc_ref, cbc_ref, dtb_ref, alog_ref, dskip_ref, ng_ref,
                y_ref, h1_ref, xpx, xpb, xpc, *state_scr, n_seg, seg, n_chunks):
    rows = n_seg * seg
    chunk = pl.program_id(2)
    carried = n_chunks > 1

    def load_history():
        xpx[:, 0:SUBLANES, :] = hx_ref[...]
        xpb[:, 0:SUBLANES, :] = hb_ref[...]
        xpc[:, 0:SUBLANES, :] = hc_ref[...]

    if carried:
        (h_st,) = state_scr

        @pl.when(chunk == 0)
        def _():
            h_st[...] = h0_ref[...].reshape(n_seg, B_GW, B_STATE)
            load_history()

        get_h = lambda g: h_st[g]

        def set_h(g, val):
            h_st[g] = val
    else:
        load_history()
        get_h = lambda g: h0_ref[g].reshape(B_GW, B_STATE)

        def set_h(g, val):
            h1_ref[g] = val.reshape(B_HPG, B_HEADDIM, B_STATE)

    x = _causal_conv_silu(x_ref[...], xpx, cwx_ref, cbx_ref, n_seg, seg, carried)
    bm = _causal_conv_silu(b_ref[...], xpb, cwb_ref, cbb_ref, n_seg, seg, carried)
    cm = _causal_conv_silu(c_ref[...], xpc, cwc_ref, cbc_ref, n_seg, seg, carried)
    dt = _softplus(dt_ref[...] + dtb_ref[...])
    da = dt * (-jnp.exp(alog_ref[...]))
    a = _seg_cumsum(da, seg)
    w_end = jnp.exp(_seg_cumsum(da, seg, reverse=True) - da)
    a_t = a.T
    a_end = a.reshape(n_seg, seg, LANES)[:, seg - 1:seg, :]
    mask = _segment_mask(rows, seg)
    head_of_lane = lax.broadcasted_iota(I32, (rows, B_GW), 1) >> 6
    head_of_row = lax.broadcasted_iota(I32, (B_GW, B_STATE), 0) >> 6
    shift = seg.bit_length() - 1
    seg_of_lane = lax.broadcasted_iota(I32, (B_GW, rows), 1) >> shift

    def per_head(cols, selector):
        out = cols[B_HPG - 1]
        for e in range(B_HPG - 2, -1, -1):
            out = jnp.where(selector == e, cols[e], out)
        return out

    xdt = x * per_head([dt[:, e:e + 1] for e in range(B_HPG)], head_of_lane)
    xdt_b = xdt.astype(BF16)
    bm_b = bm.astype(BF16)
    cb = _dot_nt(cm.astype(BF16), bm_b)
    y = None
    for e in range(B_HPG):
        decay = jnp.exp(jnp.where(mask, a[:, e:e + 1] - a_t[e:e + 1, :], NEG_INF))
        ye = _dot((cb * decay).astype(BF16), xdt_b)
        y = ye if y is None else jnp.where(head_of_lane == e, ye, y)
    if n_seg == 1:
        y_state = _dot_nt(cm.astype(BF16), get_h(0).astype(BF16))
    else:
        y_state = jnp.concatenate([_dot_nt(cm[g * seg:(g + 1) * seg], get_h(g)) for g in range(n_seg)], axis=0)
    y = y + per_head([jnp.exp(a[:, e:e + 1]) for e in range(B_HPG)], head_of_lane) * y_state
    y = (y + dskip_ref[...] * x) * _silu(z_ref[...])
    y_ref[...] = y * lax.rsqrt(jnp.mean(y * y, axis=1, keepdims=True) + EPS) * ng_ref[...]
    xw_t = (xdt * per_head([w_end[:, e:e + 1] for e in range(B_HPG)], head_of_lane)).T
    for g in range(n_seg):
        xg = xw_t if n_seg == 1 else jnp.where(seg_of_lane == g, xw_t, 0.0)
        keep = per_head([jnp.exp(a_end[g][:, e:e + 1]) for e in range(B_HPG)], head_of_row)
        set_h(g, keep * get_h(g) + _dot(xg.astype(BF16), bm_b))

    if carried:
        @pl.when(chunk == n_chunks - 1)
        def _():
            h1_ref[...] = h_st[...].reshape(n_seg, B_HPG, B_HEADDIM, B_STATE)


def ssd_mixer(grp, x, mod, state, w_in, conv_w, conv_b, dt_bias, a_log, d_skip, norm_g, w_out):
    st_h, st_conv = state
    n_main = B_INNER + B_CONV_DIM
    wb = w_in.astype(BF16)
    group_lanes = lambda v: jnp.pad(v.reshape(-1, B_GROUPS, B_HPG), ((0, 0), (0, 0), (0, LANES - B_HPG))).reshape(
        -1, B_GROUPS * LANES)
    p_main, dt_raw = norm_matmul(grp, x, mod, 1, 0, [wb[:, :n_main], group_lanes(wb[:, n_main:])])
    n_seg, seg, n_chunks, rows = grp.S, grp.L, grp.n_chunks, grp.R
    xo, bo, co = B_INNER // B_GW, (2 * B_INNER) // B_STATE, (2 * B_INNER + B_GN) // B_STATE
    cxo, cbo, cco = 0, B_INNER // B_STATE, (B_INNER + B_GN) // B_STATE
    row = lambda b, g, c: b * n_chunks + c
    hist = _pad_history(st_conv)
    scratch = [pltpu.VMEM((n_seg, SUBLANES + seg, B_GW), F32), pltpu.VMEM((n_seg, SUBLANES + seg, B_STATE), F32),
               pltpu.VMEM((n_seg, SUBLANES + seg, B_STATE), F32)]
    if n_chunks > 1:
        scratch.append(pltpu.VMEM((n_seg, B_GW, B_STATE), F32))
    state_spec = pl.BlockSpec((n_seg, B_HPG, B_HEADDIM, B_STATE), lambda b, g, c: (b, g, 0, 0))
    y, h1 = pl.pallas_call(
        functools.partial(_ssd_kernel, n_seg=n_seg, seg=seg, n_chunks=n_chunks),
        grid=(grp.n_blocks, B_GROUPS, n_chunks),
        in_specs=[
            pl.BlockSpec((rows, B_GW), lambda b, g, c: (row(b, g, c), g)),
            pl.BlockSpec((rows, B_GW), lambda b, g, c: (row(b, g, c), xo + g)),
            pl.BlockSpec((rows, B_STATE), lambda b, g, c: (row(b, g, c), bo + g)),
            pl.BlockSpec((rows, B_STATE), lambda b, g, c: (row(b, g, c), co + g)),
            pl.BlockSpec((rows, LANES), lambda b, g, c: (row(b, g, c), g)),
            pl.BlockSpec((n_seg, SUBLANES, B_GW), lambda b, g, c: (b, 0, cxo + g)),
            pl.BlockSpec((n_seg, SUBLANES, B_STATE), lambda b, g, c: (b, 0, cbo + g)),
            pl.BlockSpec((n_seg, SUBLANES, B_STATE), lambda b, g, c: (b, 0, cco + g)),
            state_spec,
            pl.BlockSpec((CONV_W, B_GW), lambda b, g, c: (0, cxo + g)),
            pl.BlockSpec((1, B_GW), lambda b, g, c: (0, cxo + g)),
            pl.BlockSpec((CONV_W, B_STATE), lambda b, g, c: (0, cbo + g)),
            pl.BlockSpec((1, B_STATE), lambda b, g, c: (0, cbo + g)),
            pl.BlockSpec((CONV_W, B_STATE), lambda b, g, c: (0, cco + g)),
            pl.BlockSpec((1, B_STATE), lambda b, g, c: (0, cco + g)),
            pl.BlockSpec((1, LANES), lambda b, g, c: (0, g)),
            pl.BlockSpec((1, LANES), lambda b, g, c: (0, g)),
            pl.BlockSpec((1, B_GW), lambda b, g, c: (0, g)),
            pl.BlockSpec((1, B_GW), lambda b, g, c: (0, g)),
        ],
        out_specs=[pl.BlockSpec((rows, B_GW), lambda b, g, c: (row(b, g, c), g)), state_spec],
        out_shape=[jax.ShapeDtypeStruct((grp.N, B_INNER), F32), jax.ShapeDtypeStruct(st_h.shape, F32)],
        scratch_shapes=scratch,
        compiler_params=_cparams("arbitrary", "arbitrary", "arbitrary"),
        name="ssd_chunk",
    )(p_main, p_main, p_main, p_main, dt_raw, hist, hist, hist, st_h,
      conv_w, conv_b.reshape(1, -1), conv_w, conv_b.reshape(1, -1), conv_w, conv_b.reshape(1, -1),
      group_lanes(dt_bias[None, :]), group_lanes(a_log[None, :]),
      jnp.repeat(d_skip, B_HEADDIM)[None, :], norm_g[None, :])
    x_new = matmul_residual(grp, y, w_out.astype(BF16), x, mod, 2)
    conv_new = p_main.reshape(grp.B, grp.T, n_main)[:, grp.T - (CONV_W - 1):, B_INNER:]
    return x_new, (h1, conv_new)


C_Q = C_HEADS * C_HD
C_KV = C_KV_HEADS * C_HD


def _rope_kernel(p_ref, cos_ref, sin_ref, qg_ref, kg_ref, q_ref, k_ref):
    tm = p_ref.shape[0]
    lane = lax.broadcasted_iota(I32, (tm, LANES), 1)
    low_head = lane < C_HD
    first_half = (lane & (C_HD - 1)) < C_HD // 2
    cos, sin = cos_ref[...], sin_ref[...]

    def norm_rope(xb, gain):
        sq = xb * xb
        s_lo = jnp.sum(jnp.where(low_head, sq, 0.0), axis=1, keepdims=True)
        s_hi = jnp.sum(jnp.where(low_head, 0.0, sq), axis=1, keepdims=True)
        ms = jnp.where(low_head, s_lo, s_hi) * (1.0 / C_HD)
        xn = xb * lax.rsqrt(ms + EPS) * gain
        partner = jnp.where(first_half, pltpu.roll(xn, LANES - C_HD // 2, axis=1), pltpu.roll(xn, C_HD // 2, axis=1))
        return xn * cos + partner * sin

    for j in range(C_Q // LANES):
        q_ref[:, j * LANES:(j + 1) * LANES] = norm_rope(p_ref[:, j * LANES:(j + 1) * LANES], qg_ref[...])
    for j in range(C_KV // LANES):
        k_ref[:, j * LANES:(j + 1) * LANES] = norm_rope(p_ref[:, C_Q + j * LANES:C_Q + (j + 1) * LANES], kg_ref[...])


def _swa_kernel(q_ref, k0_ref, k1_ref, v0_ref, v1_ref, sink_ref, o_ref, *, n_units, tq, blocks_per_seq):
    i = pl.program_id(0)
    rows = C_GROUP * tq
    t = lax.broadcasted_iota(I32, (rows, 2 * WINDOW), 0) & (tq - 1)
    s = lax.broadcasted_iota(I32, (rows, 2 * WINDOW), 1)
    valid = (s >= t) & (s <= t + WINDOW)
    if blocks_per_seq:
        valid = valid & (s >= jnp.where(lax.rem(i, blocks_per_seq) == 0, WINDOW, 0))

    def with_past(past, new):
        if tq < WINDOW:
            new = jnp.concatenate([new, jnp.zeros((WINDOW - tq, new.shape[1]), F32)], axis=0)
        return jnp.concatenate([past, new], axis=0)

    for u in range(n_units):
        kk = with_past(k0_ref[u], k1_ref[u])
        vv = with_past(v0_ref[u], v1_ref[u])
        qu = q_ref[u]
        outs = []
        for kh in range(C_KV_HEADS):
            kc = kk[:, kh * C_HD:(kh + 1) * C_HD].astype(BF16)
            vc = vv[:, kh * C_HD:(kh + 1) * C_HD].astype(BF16)
            qs = jnp.concatenate([qu[:, (kh * C_GROUP + j) * C_HD:(kh * C_GROUP + j + 1) * C_HD]
                                  for j in range(C_GROUP)], axis=0).astype(BF16)
            logits = jnp.where(valid, _dot_nt(qs, kc) * (C_HD ** -0.5), NEG_INF)
            sink = sink_ref[kh][:, 0:1]
            mx = jnp.maximum(jnp.max(logits, axis=1, keepdims=True), sink)
            pr = jnp.exp(logits - mx)
            den = jnp.sum(pr, axis=1, keepdims=True) + jnp.exp(sink - mx)
            o = _dot(pr.astype(BF16), vc) / den
            outs += [o[j * tq:(j + 1) * tq] for j in range(C_GROUP)]
        o_ref[u] = jnp.concatenate(outs, axis=1)


def swa_mixer(grp, x, mod, cache, pos0, w_in, q_g, k_g, sinks, w_out):
    (p,) = norm_matmul(grp, x, mod, 1, 0, [w_in.astype(BF16)])
    half = C_HD // 2
    inv = ROPE_THETA ** (-jnp.arange(half, dtype=F32) / half)
    ang = (pos0 + jnp.arange(grp.T)).astype(F32)[:, None] * inv[None, :]
    cos, sin = jnp.cos(ang), jnp.sin(ang)
    per_token = lambda a: jnp.tile(a, (grp.B, 1))
    cos_t = per_token(jnp.tile(cos, (1, LANES // half)))
    sin_t = per_token(jnp.tile(jnp.concatenate([-sin, sin], axis=1), (1, LANES // C_HD)))
    tm = grp.tm
    gains = lambda g: jnp.tile(g, LANES // C_HD)[None, :]
    qr, kr = pl.pallas_call(
        _rope_kernel,
        grid=(grp.N // tm,),
        in_specs=[
            pl.BlockSpec((tm, C_Q + 2 * C_KV), lambda i: (i, 0)),
            pl.BlockSpec((tm, LANES), lambda i: (i, 0)),
            pl.BlockSpec((tm, LANES), lambda i: (i, 0)),
            pl.BlockSpec((1, LANES), lambda i: (0, 0)),
            pl.BlockSpec((1, LANES), lambda i: (0, 0)),
        ],
        out_specs=[pl.BlockSpec((tm, C_Q), lambda i: (i, 0)), pl.BlockSpec((tm, C_KV), lambda i: (i, 0))],
        out_shape=[jax.ShapeDtypeStruct((grp.N, C_Q), F32), jax.ShapeDtypeStruct((grp.N, C_KV), F32)],
        compiler_params=_cparams("arbitrary"),
        name="qk_norm_rope",
    )(p, cos_t, sin_t, gains(q_g), gains(k_g))

    v_col = (C_Q + C_KV) // C_KV
    if cache is None:
        assert grp.T % WINDOW == 0
        tq, n_units, blocks_per_seq = WINDOW, 1, grp.T // WINDOW
        n_steps = grp.N // tq
        k3 = kr.reshape(n_steps, tq, C_KV)
        p3 = p.reshape(n_steps, tq, C_Q + 2 * C_KV)
        k_args = (k3, k3, p3, p3)
        prev = lambda i: (jnp.maximum(i - 1, 0), 0, 0)
        k_specs = [
            pl.BlockSpec((1, tq, C_KV), prev),
            pl.BlockSpec((1, tq, C_KV), lambda i: (i, 0, 0)),
            pl.BlockSpec((1, tq, C_KV), lambda i: (jnp.maximum(i - 1, 0), 0, v_col)),
            pl.BlockSpec((1, tq, C_KV), lambda i: (i, 0, v_col)),
        ]
    else:
        tq, n_units, blocks_per_seq = grp.T, SUBLANES, 0
        n_steps = grp.B // n_units
        k_args = (cache[0].reshape(grp.B, WINDOW, C_KV), kr.reshape(grp.B, tq, C_KV),
                  cache[1].reshape(grp.B, WINDOW, C_KV), p.reshape(grp.B, tq, C_Q + 2 * C_KV))
        k_specs = [
            pl.BlockSpec((n_units, WINDOW, C_KV), lambda i: (i, 0, 0)),
            pl.BlockSpec((n_units, tq, C_KV), lambda i: (i, 0, 0)),
            pl.BlockSpec((n_units, WINDOW, C_KV), lambda i: (i, 0, 0)),
            pl.BlockSpec((n_units, tq, C_KV), lambda i: (i, 0, v_col)),
        ]
    sink_rows = jnp.broadcast_to(sinks.reshape(C_KV_HEADS, C_GROUP, 1, 1),
                                 (C_KV_HEADS, C_GROUP, tq, LANES)).reshape(C_KV_HEADS, C_GROUP * tq, LANES)
    o = pl.pallas_call(
        functools.partial(_swa_kernel, n_units=n_units, tq=tq, blocks_per_seq=blocks_per_seq),
        grid=(n_steps,),
        in_specs=[pl.BlockSpec((n_units, tq, C_Q), lambda i: (i, 0, 0))] + k_specs
        + [pl.BlockSpec(sink_rows.shape, lambda i: (0, 0, 0))],
        out_specs=pl.BlockSpec((n_units, tq, C_Q), lambda i: (i, 0, 0)),
        out_shape=jax.ShapeDtypeStruct((grp.N // tq, tq, C_Q), F32),
        compiler_params=_cparams("arbitrary"),
        name="swa_attention",
    )(qr.reshape(grp.N // tq, tq, C_Q), *k_args, sink_rows)
    x_new = matmul_residual(grp, o.reshape(grp.N, C_Q), w_out.astype(BF16), x, mod, 2)
    k_new = kr.reshape(grp.B, grp.T, C_KV_HEADS, C_HD)
    v_new = p[:, C_Q + C_KV:].reshape(grp.B, grp.T, C_KV_HEADS, C_HD)
    if cache is None:
        return x_new, (k_new[:, grp.T - WINDOW:], v_new[:, grp.T - WINDOW:])
    return x_new, (jnp.concatenate([cache[0][:, grp.T:], k_new], axis=1),
                   jnp.concatenate([cache[1][:, grp.T:], v_new], axis=1))


def kernel(x_prompt, x_sample, c_prompt, c_sample, state_mlstm_C, state_mlstm_n, state_mlstm_m, state_mlstm_conv,
           state_ssm, state_ssm_conv, cache_swa_k, cache_swa_v, ada_w, ada_b, mlstm_w_in, mlstm_conv_w,
           mlstm_conv_b, mlstm_gate_b, mlstm_norm_g, mlstm_w_out, ssd_w_in, ssd_conv_w, ssd_conv_b, ssd_dt_bias,
           ssd_a_log, ssd_d_skip, ssd_norm_g, ssd_w_out, swa_w_in, swa_q_norm_g, swa_k_norm_g, swa_sinks,
           swa_w_out, moe_w_router, moe_b_router, moe_w1, moe_b1, moe_w2, moe_b2):
    groups = [Group(*x_prompt.shape[:2]), Group(*x_sample.shape[:2])]
    n_prompt = groups[0].B
    xs = [x_prompt.reshape(-1, D_MODEL), x_sample.reshape(-1, D_MODEL)]
    mod_all = ada_modulation(jnp.concatenate([c_prompt, c_sample], axis=0), ada_w, ada_b)
    fresh = lambda s: jnp.zeros((n_prompt,) + s.shape[2:], F32)
    new = [[[] for _ in range(8)] for _ in groups]
    for layer in range(DEPTH):
        kind, j = layer % 3, layer // 3
        mods = [groups[0].expand_mod(mod_all[layer, :n_prompt]), groups[1].expand_mod(mod_all[layer, n_prompt:])]
        for gi, grp in enumerate(groups):
            if kind == 0:
                state = (state_mlstm_n, state_mlstm_m, state_mlstm_conv)
                state = tuple(s[j] if gi else fresh(s) for s in state)
                state = ((state_mlstm_C, j) if gi else (fresh(state_mlstm_C)[None], 0),) + state
                xs[gi], st = mlstm_mixer(grp, xs[gi], mods[gi], state, mlstm_w_in[j], mlstm_conv_w[j], mlstm_conv_b[j],
                                         mlstm_gate_b[j], mlstm_norm_g[j], mlstm_w_out[j])
                first = 0
            elif kind == 1:
                state = tuple(s[j] if gi else fresh(s) for s in (state_ssm, state_ssm_conv))
                xs[gi], st = ssd_mixer(grp, xs[gi], mods[gi], state, ssd_w_in[j], ssd_conv_w[j], ssd_conv_b[j],
                                       ssd_dt_bias[j], ssd_a_log[j], ssd_d_skip[j], ssd_norm_g[j], ssd_w_out[j])
                first = 4
            else:
                cache = (cache_swa_k[j], cache_swa_v[j]) if gi else None
                xs[gi], st = swa_mixer(grp, xs[gi], mods[gi], cache, PAST_LEN if gi else 0, swa_w_in[j],
                                       swa_q_norm_g[j], swa_k_norm_g[j], swa_sinks[j], swa_w_out[j])
                first = 6
            for offset, s in enumerate(st):
                new[gi][first + offset].append(s)
        xs = moe_layer(groups, xs, mods, layer, moe_w_router, moe_b_router, moe_w1, moe_b1, moe_w2, moe_b2)
    outs = [xs[0].reshape(x_prompt.shape), xs[1].reshape(x_sample.shape)]
    for slot in range(8):
        outs += [jnp.stack(new[0][slot]), jnp.stack(new[1][slot])]
    return tuple(outs)
```

```python
import functools
import math

import jax
import jax.numpy as jnp
from jax import lax
from jax.experimental import pallas as pl
from jax.experimental.pallas import tpu as pltpu

F32 = jnp.float32
BF16 = jnp.bfloat16
I32 = jnp.int32

D_MODEL = 1024
DEPTH = 4
PAST_LEN = 8192
EPS = 1e-6
CONV_W = 4
A_HEADS = 4
A_DK = D_MODEL // 8
A_DV = D_MODEL // A_HEADS
A_QK = A_HEADS * A_DK
A_V = A_HEADS * A_DV
B_INNER = 2 * D_MODEL
B_HEADDIM = 64
B_HEADS = B_INNER // B_HEADDIM
B_STATE = 128
B_GROUPS = 8
B_GN = B_GROUPS * B_STATE
B_CONV_DIM = B_INNER + 2 * B_GN
B_HPG = B_HEADS // B_GROUPS
B_GW = B_HPG * B_HEADDIM
C_HEADS = 16
C_KV_HEADS = 4
C_GROUP = C_HEADS // C_KV_HEADS
C_HD = 64
WINDOW = 128
ROPE_THETA = 10000.0
N_EXPERTS = 32
TOP_K = 4
D_FF = D_MODEL
SWIGLU_LIMIT = 7.0
SWIGLU_ALPHA = 1.702

LANES = 128
SUBLANES = 8
VMEM_LIMIT_BYTES = 56 * 1024 * 1024

ROW_TILE = 256
SEQ_CHUNK = 256
SHORT_BLOCK = 128
MOE_TILE = 512
FF_CHUNK = 1024
NEG_INF = float("-inf")
ROW_AS_TILE = (SUBLANES, D_MODEL // SUBLANES)


def _cparams(*sem):
    return pltpu.CompilerParams(dimension_semantics=sem, vmem_limit_bytes=VMEM_LIMIT_BYTES)


def _silu(x):
    return x * jax.nn.sigmoid(x)


def _dot(a, b):
    return jnp.dot(a, b, preferred_element_type=F32)


def _dot_nt(a, b):
    return lax.dot_general(a, b, (((1,), (1,)), ((), ())), preferred_element_type=F32)


class Group:
    def __init__(self, B, T):
        self.B, self.T = B, T
        self.N = B * T
        self.long = T % SEQ_CHUNK == 0
        if self.long:
            self.S, self.L = 1, SEQ_CHUNK
        else:
            assert T == SUBLANES and self.N % SHORT_BLOCK == 0
            self.S, self.L = SHORT_BLOCK // T, T
        self.R = self.S * self.L
        self.n_blocks = B // self.S
        self.n_chunks = T // self.L
        self.tm = min(ROW_TILE, self.N)
        assert self.N % self.tm == 0 and (not self.long or T % self.tm == 0)

    def expand_mod(self, mod):
        if self.long:
            return mod.reshape(self.B, 1, 6 * D_MODEL)
        return jnp.repeat(mod, self.T, axis=0)

    def mod_spec(self, j, rows, row_block_fn):
        if self.long:
            per_seq = self.T // rows
            return pl.BlockSpec((None, 1, D_MODEL), lambda *g: (row_block_fn(*g) // per_seq, 0, j))
        return pl.BlockSpec((rows, D_MODEL), lambda *g: (row_block_fn(*g), j))


def _ada_kernel(c_ref, w_ref, b_ref, o_ref):
    cs = _silu(c_ref[...])
    o_ref[...] = _dot(cs.astype(BF16), w_ref[...].astype(BF16)) + b_ref[...]


def ada_modulation(c_all, ada_w, ada_b):
    rows = c_all.shape[0]
    tn = 1536
    n_out = 6 * D_MODEL
    return pl.pallas_call(
        _ada_kernel,
        grid=(DEPTH, n_out // tn),
        in_specs=[
            pl.BlockSpec((rows, D_MODEL), lambda l, j: (0, 0)),
            pl.BlockSpec((None, D_MODEL, tn), lambda l, j: (l, 0, j)),
            pl.BlockSpec((None, 1, tn), lambda l, j: (l, 0, j)),
        ],
        out_specs=pl.BlockSpec((None, rows, tn), lambda l, j: (l, 0, j)),
        out_shape=jax.ShapeDtypeStruct((DEPTH, rows, n_out), F32),
        compiler_params=_cparams("arbitrary", "arbitrary"),
        name="ada_modulation",
    )(c_all, ada_w, ada_b.reshape(DEPTH, 1, n_out))


def _modulated_norm(x, sc, sh):
    ms = jnp.mean(x * x, axis=-1, keepdims=True)
    return (x * lax.rsqrt(ms + EPS)) * (1.0 + sc) + sh


def _norm_mm_kernel(x_ref, sc_ref, sh_ref, *refs, n_w):
    h = _modulated_norm(x_ref[...], sc_ref[...], sh_ref[...]).astype(BF16)
    for w_ref, o_ref in zip(refs[:n_w], refs[n_w:]):
        o_ref[...] = _dot(h, w_ref[...])


def norm_matmul(grp, x, mod, j_scale, j_shift, weights):
    tm = grp.tm
    n_w = len(weights)
    in_specs = [
        pl.BlockSpec((tm, D_MODEL), lambda i: (i, 0)),
        grp.mod_spec(j_scale, tm, lambda i: i),
        grp.mod_spec(j_shift, tm, lambda i: i),
    ] + [pl.BlockSpec(w.shape, lambda i: (0, 0)) for w in weights]
    out_specs = [pl.BlockSpec((tm, w.shape[1]), lambda i: (i, 0)) for w in weights]
    out_shape = [jax.ShapeDtypeStruct((grp.N, w.shape[1]), F32) for w in weights]
    return pl.pallas_call(
        functools.partial(_norm_mm_kernel, n_w=n_w),
        grid=(grp.N // tm,),
        in_specs=in_specs,
        out_specs=out_specs,
        out_shape=out_shape,
        compiler_params=_cparams("arbitrary"),
        name="norm_matmul",
    )(x, mod, mod, *weights)


def _mm_res_kernel(y_ref, w_ref, x_ref, g_ref, o_ref):
    o_ref[...] = x_ref[...] + g_ref[...] * _dot(y_ref[...].astype(BF16), w_ref[...])


def matmul_residual(grp, y, w, x, mod, j_gate):
    tm = grp.tm
    k = y.shape[1]
    return pl.pallas_call(
        _mm_res_kernel,
        grid=(grp.N // tm,),
        in_specs=[
            pl.BlockSpec((tm, k), lambda i: (i, 0)),
            pl.BlockSpec(w.shape, lambda i: (0, 0)),
            pl.BlockSpec((tm, D_MODEL), lambda i: (i, 0)),
            grp.mod_spec(j_gate, tm, lambda i: i),
        ],
        out_specs=pl.BlockSpec((tm, D_MODEL), lambda i: (i, 0)),
        out_shape=jax.ShapeDtypeStruct((grp.N, D_MODEL), F32),
        compiler_params=_cparams("arbitrary"),
        name="matmul_residual",
    )(y, w, x, mod)


def _router_kernel(x_ref, sc_ref, sh_ref, wr_ref, br_ref, cnt_in_ref,
                   h_ref, ri_ref, rg_ref, cnt_out_ref, cnt_scr):
    i = pl.program_id(0)
    tm = x_ref.shape[0]

    @pl.when(i == 0)
    def _():
        cnt_scr[...] = cnt_in_ref[...]

    h = _modulated_norm(x_ref[...], sc_ref[...], sh_ref[...])
    h_ref[...] = h.reshape(tm, *ROW_AS_TILE)
    h_hi = h.astype(BF16)
    h_lo = (h - h_hi.astype(F32)).astype(BF16)
    hi_terms = _dot(h_hi, wr_ref[...])
    logits = (hi_terms[:, :LANES] + hi_terms[:, LANES:]) + _dot(h_lo, wr_ref[:, :LANES]) + br_ref[...]
    lane = lax.broadcasted_iota(I32, (tm, LANES), 1)
    lane_f = lane.astype(F32)
    work = jnp.where(lane < N_EXPERTS, logits, NEG_INF)
    top_v, top_sel, top_i = [], [], []
    for _ in range(TOP_K):
        mx = jnp.max(work, axis=1, keepdims=True)
        idx = jnp.min(jnp.where(work == mx, lane_f, float(LANES)), axis=1, keepdims=True)
        sel = lane_f == idx
        work = jnp.where(sel, NEG_INF, work)
        top_v.append(mx)
        top_sel.append(sel)
        top_i.append(idx)
    ex = [jnp.exp(v - top_v[0]) for v in top_v]
    inv = 1.0 / (ex[0] + ex[1] + ex[2] + ex[3])
    chosen = jnp.zeros((tm, LANES), F32)
    for sel in top_sel:
        chosen = jnp.where(sel, 1.0, chosen)
    r = lax.broadcasted_iota(I32, (tm, tm), 0)
    c = lax.broadcasted_iota(I32, (tm, tm), 1)
    before = jnp.where(c < r, 1.0, 0.0).astype(BF16)
    rank_all = _dot(before, chosen.astype(BF16)) + cnt_scr[...]
    ri = jnp.zeros((tm, LANES), I32)
    rg = jnp.zeros((tm, LANES), F32)
    for k in range(TOP_K):
        rank_k = jnp.sum(jnp.where(top_sel[k], rank_all, 0.0), axis=1, keepdims=True)
        ri = jnp.where(lane == k, top_i[k].astype(I32), ri)
        ri = jnp.where(lane == TOP_K + k, rank_k.astype(I32), ri)
        rg = jnp.where(lane == k, ex[k] * inv, rg)
    ri_ref[...] = ri
    rg_ref[...] = rg
    cnt_scr[...] = cnt_scr[...] + jnp.sum(chosen, axis=0, keepdims=True)
    cnt_out_ref[...] = cnt_scr[...]


def moe_router(grp, x, mod, w_r, b_r, cnt_in):
    tm = grp.tm
    return pl.pallas_call(
        _router_kernel,
        grid=(grp.N // tm,),
        in_specs=[
            pl.BlockSpec((tm, D_MODEL), lambda i: (i, 0)),
            grp.mod_spec(4, tm, lambda i: i),
            grp.mod_spec(3, tm, lambda i: i),
            pl.BlockSpec((D_MODEL, 2 * LANES), lambda i: (0, 0)),
            pl.BlockSpec((1, LANES), lambda i: (0, 0)),
            pl.BlockSpec((1, LANES), lambda i: (0, 0)),
        ],
        out_specs=[
            pl.BlockSpec((tm,) + ROW_AS_TILE, lambda i: (i, 0, 0)),
            pl.BlockSpec((tm, LANES), lambda i: (i, 0)),
            pl.BlockSpec((tm, LANES), lambda i: (i, 0)),
            pl.BlockSpec((1, LANES), lambda i: (0, 0)),
        ],
        out_shape=[
            jax.ShapeDtypeStruct((grp.N,) + ROW_AS_TILE, F32),
            jax.ShapeDtypeStruct((grp.N, LANES), I32),
            jax.ShapeDtypeStruct((grp.N, LANES), F32),
            jax.ShapeDtypeStruct((1, LANES), F32),
        ],
        scratch_shapes=[pltpu.VMEM((1, LANES), F32)],
        compiler_params=_cparams("arbitrary"),
        name="moe_router",
    )(x, mod, mod, w_r, b_r, cnt_in)


def _row_copy(src_ref, src_row, dst_ref, dst_row, sem):
    return pltpu.make_async_copy(src_ref.at[src_row], dst_ref.at[dst_row], sem)


DMA_UNROLL = 8


def _dispatch_kernel(pos_ref, cnt_ref, off_ref, nv_ref, *refs, tiles, n_sorted_tiles):
    n_groups = len(tiles)
    h_refs = refs[:n_groups]
    xs_ref, zero_scr, sem = refs[n_groups:]
    i = pl.program_id(0)
    tm = h_refs[0].shape[0]
    base = (i - 1) * (tm * TOP_K)

    def zero_tile(j):
        return pltpu.make_async_copy(zero_scr, xs_ref.at[pl.ds(j * MOE_TILE, MOE_TILE)], sem)

    def for_each_partial_tile(action):
        def body(e, carry):
            n = cnt_ref[e]

            @pl.when((n & (MOE_TILE - 1)) != 0)
            def _():
                action(zero_tile((off_ref[e] + n) // MOE_TILE))

            return carry

        lax.fori_loop(0, N_EXPERTS, body, 0)

        def tail(j, carry):
            action(zero_tile(j))
            return carry

        lax.fori_loop(nv_ref[0], n_sorted_tiles, tail, 0)

    @pl.when(i == 0)
    def _():
        zero_scr[...] = jnp.zeros(zero_scr.shape, F32)
        for_each_partial_tile(lambda cp: cp.start())
        for_each_partial_tile(lambda cp: cp.wait())

    def scatter_tile(h_ref):
        def start(t, carry):
            for k in range(TOP_K):
                _row_copy(h_ref, t, xs_ref, pos_ref[base + t * TOP_K + k], sem).start(priority=k % 2)
            return carry

        lax.fori_loop(0, tm, start, 0, unroll=DMA_UNROLL)
        for _ in range(TOP_K):
            pltpu.make_async_copy(h_ref, xs_ref.at[pl.ds(0, tm)], sem).wait()

    first = 1
    for h_ref, n in zip(h_refs, tiles):
        pl.when((i >= first) & (i < first + n))(functools.partial(scatter_tile, h_ref))
        first += n


def moe_dispatch(groups, hs, pos_all, counts, row_off, n_valid, n_sorted_tiles):
    tm = groups[0].tm
    assert all(g.tm == tm for g in groups)
    tiles = tuple(g.N // tm for g in groups)
    starts = [1 + sum(tiles[:gi]) for gi in range(len(tiles))]

    def h_spec(first, n):
        return pl.BlockSpec((tm,) + ROW_AS_TILE, lambda i, *_: (jnp.clip(i - first, 0, n - 1), 0, 0))

    return pl.pallas_call(
        functools.partial(_dispatch_kernel, tiles=tiles, n_sorted_tiles=n_sorted_tiles),
        grid_spec=pltpu.PrefetchScalarGridSpec(
            num_scalar_prefetch=4,
            grid=(sum(tiles) + 1,),
            in_specs=[h_spec(first, n) for first, n in zip(starts, tiles)],
            out_specs=pl.BlockSpec(memory_space=pl.ANY),
            scratch_shapes=[pltpu.VMEM((MOE_TILE,) + ROW_AS_TILE, F32), pltpu.SemaphoreType.DMA],
        ),
        out_shape=jax.ShapeDtypeStruct((n_sorted_tiles * MOE_TILE,) + ROW_AS_TILE, F32),
        compiler_params=_cparams("arbitrary"),
        name="moe_dispatch",
    )(pos_all, counts, row_off, n_valid, *hs)


def _combine_kernel(pos_ref, ys_ref, rg_ref, x_ref, g_ref, o_ref, ybuf, sems):
    i = pl.program_id(0)
    n = pl.num_programs(0)
    tm = x_ref.shape[0]

    def gather_tile(tile, slot):
        base = tile * (tm * TOP_K)

        def start(t, carry):
            for k in range(TOP_K):
                _row_copy(ys_ref, pos_ref[base + t * TOP_K + k], ybuf.at[slot, k], t,
                          sems.at[slot]).start(priority=k % 2)
            return carry

        lax.fori_loop(0, tm, start, 0, unroll=DMA_UNROLL)

    slot = lax.rem(i, 2)

    @pl.when(i == 0)
    def _():
        gather_tile(0, 0)

    @pl.when(i + 1 < n)
    def _():
        gather_tile(i + 1, 1 - slot)

    for k in range(TOP_K):
        pltpu.make_async_copy(ys_ref.at[pl.ds(0, tm)], ybuf.at[slot, k], sems.at[slot]).wait()
    rg = rg_ref[...]
    acc = rg[:, 0:1] * ybuf[slot, 0].reshape(tm, D_MODEL)
    for k in range(1, TOP_K):
        acc = acc + rg[:, k:k + 1] * ybuf[slot, k].reshape(tm, D_MODEL)
    o_ref[...] = x_ref[...] + g_ref[...] * acc


def moe_combine(grp, ys, pos, route_g, x, mod):
    tm = grp.tm
    return pl.pallas_call(
        _combine_kernel,
        grid_spec=pltpu.PrefetchScalarGridSpec(
            num_scalar_prefetch=1,
            grid=(grp.N // tm,),
            in_specs=[
                pl.BlockSpec(memory_space=pl.ANY),
                pl.BlockSpec((tm, LANES), lambda i, p: (i, 0)),
                pl.BlockSpec((tm, D_MODEL), lambda i, p: (i, 0)),
                grp.mod_spec(5, tm, lambda i, p: i),
            ],
            out_specs=pl.BlockSpec((tm, D_MODEL), lambda i, p: (i, 0)),
            scratch_shapes=[pltpu.VMEM((2, TOP_K, tm) + ROW_AS_TILE, F32), pltpu.SemaphoreType.DMA((2,))],
        ),
        out_shape=jax.ShapeDtypeStruct((grp.N, D_MODEL), F32),
        compiler_params=_cparams("arbitrary"),
        name="moe_combine",
    )(pos, ys, route_g, x, mod)


def _experts_kernel(te_ref, tb_ref, nv_ref, to_ref, seq_ref, nu_ref, x_ref, w1_ref, b1_ref, w2_ref, b2_ref, y_ref,
                    w1_scr, w2_scr, w1_stage, w2_stage, sems, *, layer):
    i = pl.program_id(0)

    def weight_copies(ordinal):
        e = seq_ref[ordinal]
        slot = lax.rem(ordinal, 2)
        return (pltpu.make_async_copy(w1_ref.at[layer, e], w1_stage.at[slot], sems.at[slot]),
                pltpu.make_async_copy(w2_ref.at[layer, e], w2_stage.at[slot], sems.at[slot]))

    @pl.when(i == 0)
    def _():
        for cp in weight_copies(0):
            cp.start()

    @pl.when(i < nv_ref[0])
    def _():
        ordinal = to_ref[i]

        @pl.when((i == 0) | (ordinal != to_ref[jnp.maximum(i - 1, 0)]))
        def _():
            @pl.when(ordinal + 1 < nu_ref[0])
            def _():
                for cp in weight_copies(ordinal + 1):
                    cp.start()

            for cp in weight_copies(ordinal):
                cp.wait()
            slot = lax.rem(ordinal, 2)
            w1_scr[...] = w1_stage[slot].astype(BF16)
            w2_scr[...] = w2_stage[slot].astype(BF16)

        xb = x_ref[...].reshape(MOE_TILE, D_MODEL).astype(BF16)
        y = b2_ref[...]
        for j in range(D_FF // FF_CHUNK):
            cols = slice(j * FF_CHUNK, (j + 1) * FF_CHUNK)
            ucols = slice(D_FF + j * FF_CHUNK, D_FF + (j + 1) * FF_CHUNK)
            g = jnp.minimum(_dot(xb, w1_scr[:, cols]) + b1_ref[:, cols], SWIGLU_LIMIT)
            u = jnp.clip(_dot(xb, w1_scr[:, ucols]) + b1_ref[:, ucols], -SWIGLU_LIMIT, SWIGLU_LIMIT)
            act = (u + 1.0) * (g * jax.nn.sigmoid(SWIGLU_ALPHA * g))
            y = y + _dot(act.astype(BF16), w2_scr[cols, :])
        y_ref[...] = y.reshape(y_ref.shape)

    @pl.when(i >= nv_ref[0])
    def _():
        y_ref[...] = jnp.zeros(y_ref.shape, F32)


def moe_experts(xs, tile_expert, tile_block, n_valid, tiles_per, w1, b1, w2, b2, layer):
    n_tiles = tile_expert.shape[0]
    tm = MOE_TILE
    used = tiles_per > 0
    ordinal_of = jnp.cumsum(used.astype(I32)) - 1
    experts = jnp.arange(N_EXPERTS, dtype=I32)
    seq = jnp.sum(jnp.where(used[None, :] & (ordinal_of[None, :] == experts[:, None]), experts[None, :], 0), axis=1)
    tile_ordinal = jnp.sum(jnp.where(tile_expert[:, None] == experts[None, :], ordinal_of[None, :], 0), axis=1)
    n_used = jnp.sum(used.astype(I32)).reshape(1)
    bias_map = lambda i, te, *_: (layer, te[i], 0, 0)
    return pl.pallas_call(
        functools.partial(_experts_kernel, layer=layer),
        grid_spec=pltpu.PrefetchScalarGridSpec(
            num_scalar_prefetch=6,
            grid=(n_tiles,),
            in_specs=[
                pl.BlockSpec((tm,) + ROW_AS_TILE, lambda i, te, tb, *_: (tb[i], 0, 0)),
                pl.BlockSpec(memory_space=pl.ANY),
                pl.BlockSpec((None, None, 1, 2 * D_FF), bias_map),
                pl.BlockSpec(memory_space=pl.ANY),
                pl.BlockSpec((None, None, 1, D_MODEL), bias_map),
            ],
            out_specs=pl.BlockSpec((tm,) + ROW_AS_TILE, lambda i, *_: (i, 0, 0)),
            scratch_shapes=[pltpu.VMEM((D_MODEL, 2 * D_FF), BF16), pltpu.VMEM((D_FF, D_MODEL), BF16),
                            pltpu.VMEM((2, D_MODEL, 2 * D_FF), F32), pltpu.VMEM((2, D_FF, D_MODEL), F32),
                            pltpu.SemaphoreType.DMA((2,))],
        ),
        out_shape=jax.ShapeDtypeStruct(xs.shape, F32),
        compiler_params=_cparams("arbitrary"),
        name="moe_experts",
    )(tile_expert, tile_block, n_valid, tile_ordinal, seq, n_used, xs, w1,
      b1.reshape(DEPTH, N_EXPERTS, 1, 2 * D_FF), w2, b2.reshape(DEPTH, N_EXPERTS, 1, D_MODEL))


def moe_layer(groups, xs_in, mods, layer, w_router, b_router, w1, b1, w2, b2):
    n_total = sum(g.N for g in groups)
    n_pairs = n_total * TOP_K
    n_tiles = -(-n_pairs // MOE_TILE) + N_EXPERTS
    w_r = jnp.pad(w_router[layer], ((0, 0), (0, LANES - N_EXPERTS)))
    w_hi = w_r.astype(BF16)
    w_r = jnp.concatenate([w_hi, (w_r - w_hi.astype(F32)).astype(BF16)], axis=1)
    b_r =jnp.pad(b_router[layer], (0, LANES - N_EXPERTS)).reshape(1, LANES)
    cnt = jnp.zeros((1, LANES), F32)
    routed = []
    for grp, x, mod in zip(groups, xs_in, mods):
        h, route_i, route_g, cnt = moe_router(grp, x, mod, w_r, b_r, cnt)
        routed.append((h, route_i, route_g))
    counts = cnt[0, :N_EXPERTS].astype(I32)
    tiles_per = (counts + MOE_TILE - 1) // MOE_TILE
    tile_end = jnp.cumsum(tiles_per)
    tile_start = tile_end - tiles_per
    n_valid = tile_end[-1:]
    tidx = jnp.minimum(jnp.arange(n_tiles, dtype=I32), n_valid - 1)
    tile_expert = jnp.sum((tile_end[None, :] <= tidx[:, None]).astype(I32), axis=1)
    row_off = tile_start * MOE_TILE
    positions = []
    for _, route_i, _ in routed:
        route_e, rank = route_i[:, :TOP_K], route_i[:, TOP_K:2 * TOP_K]
        onehot = route_e[:, :, None] == jnp.arange(N_EXPERTS, dtype=I32)[None, None, :]
        positions.append((jnp.sum(jnp.where(onehot, row_off[None, None, :], 0), axis=2) + rank).reshape(-1))
    xs = moe_dispatch(groups, [h for h, _, _ in routed], jnp.concatenate(positions), counts, row_off, n_valid,
                      n_tiles)
    ys = moe_experts(xs, tile_expert, tidx, n_valid, tiles_per, w1, b1, w2, b2, layer)
    return [moe_combine(grp, ys, pos, route_g, x, mod)
            for grp, x, mod, pos, (_, _, route_g) in zip(groups, xs_in, mods, positions, routed)]


def _seg_cumsum(x, seg, reverse=False):
    rows = x.shape[0]
    row = lax.broadcasted_iota(I32, x.shape, 0) & (seg - 1)
    sh = 1
    while sh < seg:
        if reverse:
            x = x + jnp.where(row + sh < seg, pltpu.roll(x, rows - sh, axis=0), 0.0)
        else:
            x = x + jnp.where(row >= sh, pltpu.roll(x, sh, axis=0), 0.0)
        sh *= 2
    return x


def _segment_mask(rows, seg):
    ti = lax.broadcasted_iota(I32, (rows, rows), 0)
    si = lax.broadcasted_iota(I32, (rows, rows), 1)
    mask = si <= ti
    if seg < rows:
        shift = seg.bit_length() - 1
        mask = mask & ((ti >> shift) == (si >> shift))
    return mask


def _causal_conv_silu(x, xp_scr, w_ref, b_ref, n_seg, seg, carry):
    rows, ch = x.shape
    xp_scr[:, SUBLANES:, :] = x.reshape(n_seg, seg, ch)
    acc = b_ref[...]
    for i in range(CONV_W):
        back = CONV_W - 1 - i
        xs = x if back == 0 else xp_scr[:, SUBLANES - back:SUBLANES - back + seg, :].reshape(rows, ch)
        acc = acc + xs * w_ref[i:i + 1, :]
    if carry:
        xp_scr[:, SUBLANES - (CONV_W - 1):SUBLANES, :] = xp_scr[:, SUBLANES + seg - (CONV_W - 1):SUBLANES + seg, :]
    return _silu(acc)


def _pad_history(buf):
    return jnp.pad(buf, ((0, 0), (SUBLANES - (CONV_W - 1), 0), (0, 0)))


def _mlstm_kernel(p_ref, gi_ref, gf_ref, hist_ref, c0_ref, n0_ref, m0_ref, cw_ref, cb_ref, gbi_ref, gbf_ref,
                  ng_ref, y_ref, c1_ref, n1_ref, m1_ref, xp_scr, *state_scr, n_seg, seg, n_chunks):
    rows = n_seg * seg
    chunk = pl.program_id(1)
    carried = n_chunks > 1
    if carried:
        c_st, n_st, m_st = state_scr

        @pl.when(chunk == 0)
        def _():
            c_st[...] = c0_ref[...]
            n_st[...] = n0_ref[...]
            m_st[...] = m0_ref[...]
            xp_scr[:, 0:SUBLANES, :] = hist_ref[...]

        c_in, n_in, m_in, c_out, n_out, m_out = c_st, n_st, m_st, c_st, n_st, m_st
    else:
        xp_scr[:, 0:SUBLANES, :] = hist_ref[...]
        c_in, n_in, m_in, c_out, n_out, m_out = c0_ref, n0_ref, m0_ref, c1_ref, n1_ref, m1_ref

    qk = _causal_conv_silu(p_ref[:, :2 * A_QK], xp_scr, cw_ref, cb_ref, n_seg, seg, carried)
    log_i = gi_ref[...] + gbi_ref[...]
    f_pre = gf_ref[...] + gbf_ref[...]
    log_f = jnp.minimum(f_pre, 0.0) - jnp.log1p(jnp.exp(-jnp.abs(f_pre)))
    a = _seg_cumsum(log_f, seg)
    a_rev = _seg_cumsum(log_f, seg, reverse=True)
    m0 = m_in[...]
    inter3 = a.reshape(n_seg, seg, LANES) + m0
    inter = inter3.reshape(rows, LANES)
    src3 = (a_rev - log_f + log_i).reshape(n_seg, seg, LANES)
    inter_end = inter3[:, seg - 1:seg, :]
    m_end = jnp.maximum(inter_end, jnp.max(src3, axis=1, keepdims=True))
    w_old = jnp.exp(inter_end - m_end)
    w_src = jnp.exp(src3 - m_end).reshape(rows, LANES)
    b_t = (a - log_i).T
    mask = _segment_mask(rows, seg)
    seg_of_lane = lax.broadcasted_iota(I32, (A_DK, rows), 1) >> (seg.bit_length() - 1)

    for h in range(A_HEADS):
        q = qk[:, h * A_DK:(h + 1) * A_DK] * (A_DK ** -0.5)
        k = qk[:, A_QK + h * A_DK:A_QK + (h + 1) * A_DK]
        vb = p_ref[:, 2 * A_QK + h * A_DV:2 * A_QK + (h + 1) * A_DV].astype(BF16)
        qb = q.astype(BF16)
        dmat = jnp.where(mask, a[:, h:h + 1] - b_t[h:h + 1, :], NEG_INF)
        m_tok = jnp.maximum(inter[:, h:h + 1], jnp.max(dmat, axis=1, keepdims=True))
        s = _dot_nt(qb, k.astype(BF16)) * jnp.exp(dmat - m_tok)
        w_inter = jnp.exp(inter[:, h:h + 1] - m_tok)
        if n_seg == 1:
            q_c = _dot(qb, c_in[0, h].astype(BF16))
        else:
            q_c = jnp.concatenate([_dot(q[g * seg:(g + 1) * seg], c_in[g, h]) for g in range(n_seg)], axis=0)
        n3 = n_in[:, h:h + 1, :]
        n_tok = jnp.broadcast_to(n3, (n_seg, seg, A_DK)).reshape(rows, A_DK)
        num = _dot(s.astype(BF16), vb) + w_inter * q_c
        den = jnp.sum(s, axis=1, keepdims=True) + w_inter * jnp.sum(q * n_tok, axis=1, keepdims=True)
        hv = num / jnp.maximum(jnp.abs(den), jnp.exp(-m_tok))
        hn = hv * lax.rsqrt(jnp.mean(hv * hv, axis=1, keepdims=True) + EPS) * ng_ref[h:h + 1, :]
        o = jax.nn.sigmoid(p_ref[:, 2 * A_QK + A_V + h * A_DV:2 * A_QK + A_V + (h + 1) * A_DV])
        y_ref[:, h * A_DV:(h + 1) * A_DV] = o * hn
        kw = k * w_src[:, h:h + 1]
        kw_t = kw.T
        for g in range(n_seg):
            kg = kw_t if n_seg == 1 else jnp.where(seg_of_lane == g, kw_t, 0.0)
            c_out[g, h] = w_old[g, :, h:h + 1] * c_in[g, h] + _dot(kg.astype(BF16), vb)
        n_out[:, h:h + 1, :] = (w_old[:, :, h:h + 1] * n3
                                + jnp.sum(kw.reshape(n_seg, seg, A_DK), axis=1, keepdims=True))
    m_out[...] = m_end

    if carried:
        @pl.when(chunk == n_chunks - 1)
        def _():
            c1_ref[...] = c_st[...]
            n1_ref[...] = n_st[...]
            m1_ref[...] = m_st[...]


def _lane_pad(w, cols=LANES):
    return jnp.pad(w, ((0, 0), (0, cols - w.shape[1])))


def mlstm_mixer(grp, x, mod, state, w_in, conv_w, conv_b, gate_b, norm_g, w_out):
    (st_c, c_layer), st_n, st_m, st_conv = state
    n_main = 2 * A_QK + 2 * A_V
    wb = w_in.astype(BF16)
    p_main, gate_i, gate_f = norm_matmul(
        grp, x, mod, 1, 0,
        [wb[:, :n_main], _lane_pad(wb[:, n_main:n_main + A_HEADS]), _lane_pad(wb[:, n_main + A_HEADS:])])
    n_seg, seg, n_chunks, rows = grp.S, grp.L, grp.n_chunks, grp.R
    row_map = lambda b, c: (b * n_chunks + c, 0)
    scratch = [pltpu.VMEM((n_seg, SUBLANES + seg, 2 * A_QK), F32)]
    if n_chunks > 1:
        scratch += [pltpu.VMEM((n_seg, A_HEADS, A_DK, A_DV), F32), pltpu.VMEM((n_seg, A_HEADS, A_DK), F32),
                    pltpu.VMEM((n_seg, 1, LANES), F32)]
    full = lambda shape: pl.BlockSpec(shape, lambda b, c: (0,) * len(shape))
    y, c1, n1, m1 = pl.pallas_call(
        functools.partial(_mlstm_kernel, n_seg=n_seg, seg=seg, n_chunks=n_chunks),
        grid=(grp.n_blocks, n_chunks),
        in_specs=[
            pl.BlockSpec((rows, n_main), row_map),
            pl.BlockSpec((rows, LANES), row_map),
            pl.BlockSpec((rows, LANES), row_map),
            pl.BlockSpec((n_seg, SUBLANES, 2 * A_QK), lambda b, c: (b, 0, 0)),
            pl.BlockSpec((None, n_seg, A_HEADS, A_DK, A_DV), lambda b, c: (c_layer, b, 0, 0, 0)),
            pl.BlockSpec((n_seg, A_HEADS, A_DK), lambda b, c: (b, 0, 0)),
            pl.BlockSpec((n_seg, 1, LANES), lambda b, c: (b, 0, 0)),
            full((CONV_W, 2 * A_QK)), full((1, 2 * A_QK)), full((1, LANES)), full((1, LANES)),
            full((A_HEADS, A_DV)),
        ],
        out_specs=[
            pl.BlockSpec((rows, A_V), row_map),
            pl.BlockSpec((n_seg, A_HEADS, A_DK, A_DV), lambda b, c: (b, 0, 0, 0)),
            pl.BlockSpec((n_seg, A_HEADS, A_DK), lambda b, c: (b, 0, 0)),
            pl.BlockSpec((n_seg, 1, LANES), lambda b, c: (b, 0, 0)),
        ],
        out_shape=[
            jax.ShapeDtypeStruct((grp.N, A_V), F32),
            jax.ShapeDtypeStruct(st_c.shape[1:], F32),
            jax.ShapeDtypeStruct(st_n.shape, F32),
            jax.ShapeDtypeStruct((grp.B, 1, LANES), F32),
        ],
        scratch_shapes=scratch,
        compiler_params=_cparams("arbitrary", "arbitrary"),
        name="mlstm_chunk",
    )(p_main, gate_i, gate_f, _pad_history(st_conv), st_c, st_n, _lane_pad(st_m)[:, None, :],
      conv_w, conv_b.reshape(1, -1), _lane_pad(gate_b[None, :A_HEADS]), _lane_pad(gate_b[None, A_HEADS:]), norm_g)
    x_new = matmul_residual(grp, y, w_out.astype(BF16), x, mod, 2)
    conv_new = p_main.reshape(grp.B, grp.T, n_main)[:, grp.T - (CONV_W - 1):, :2 * A_QK]
    return x_new, (c1, n1, m1[:, 0, :A_HEADS], conv_new)


def _softplus(x):
    return jnp.maximum(x, 0.0) + jnp.log1p(jnp.exp(-jnp.abs(x)))


def _ssd_kernel(z_ref, x_ref, b_ref, c_ref, dt_ref, hx_ref, hb_ref, hc_ref, h0_ref,
                cwx_ref, cbx_ref, cwb_ref, cbb_ref, cwc_ref, cbc_ref, dtb_ref, alog_ref, dskip_ref, ng_ref,
                y_ref, h1_ref, xpx, xpb, xpc, *state_scr, n_seg, seg, n_chunks):
    rows = n_seg * seg
    chunk = pl.program_id(2)
    carried = n_chunks > 1

    def load_history():
        xpx[:, 0:SUBLANES, :] = hx_ref[...]
        xpb[:, 0:SUBLANES, :] = hb_ref[...]
        xpc[:, 0:SUBLANES, :] = hc_ref[...]

    if carried:
        (h_st,) = state_scr

        @pl.when(chunk == 0)
        def _():
            h_st[...] = h0_ref[...].reshape(n_seg, B_GW, B_STATE)
            load_history()

        get_h = lambda g: h_st[g]

        def set_h(g, val):
            h_st[g] = val
    else:
        load_history()
        get_h = lambda g: h0_ref[g].reshape(B_GW, B_STATE)

        def set_h(g, val):
            h1_ref[g] = val.reshape(B_HPG, B_HEADDIM, B_STATE)

    x = _causal_conv_silu(x_ref[...], xpx, cwx_ref, cbx_ref, n_seg, seg, carried)
    bm = _causal_conv_silu(b_ref[...], xpb, cwb_ref, cbb_ref, n_seg, seg, carried)
    cm = _causal_conv_silu(c_ref[...], xpc, cwc_ref, cbc_ref, n_seg, seg, carried)
    dt = _softplus(dt_ref[...] + dtb_ref[...])
    da = dt * (-jnp.exp(alog_ref[...]))
    a = _seg_cumsum(da, seg)
    w_end = jnp.exp(_seg_cumsum(da, seg, reverse=True) - da)
    a_t = a.T
    a_end = a.reshape(n_seg, seg, LANES)[:, seg - 1:seg, :]
    mask = _segment_mask(rows, seg)
    head_of_lane = lax.broadcasted_iota(I32, (rows, B_GW), 1) >> 6
    head_of_row = lax.broadcasted_iota(I32, (B_GW, B_STATE), 0) >> 6
    shift = seg.bit_length() - 1
    seg_of_lane = lax.broadcasted_iota(I32, (B_GW, rows), 1) >> shift

    def per_head(cols, selector):
        out = cols[B_HPG - 1]
        for e in range(B_HPG - 2, -1, -1):
            out = jnp.where(selector == e, cols[e], out)
        return out

    xdt = x * per_head([dt[:, e:e + 1] for e in range(B_HPG)], head_of_lane)
    xdt_b = xdt.astype(BF16)
    bm_b = bm.astype(BF16)
    cb = _dot_nt(cm.astype(BF16), bm_b)
    y = None
    for e in range(B_HPG):
        decay = jnp.exp(jnp.where(mask, a[:, e:e + 1] - a_t[e:e + 1, :], NEG_INF))
        ye = _dot((cb * decay).astype(BF16), xdt_b)
        y = ye if y is None else jnp.where(head_of_lane == e, ye, y)
    if n_seg == 1:
        y_state = _dot_nt(cm.astype(BF16), get_h(0).astype(BF16))
    else:
        y_state = jnp.concatenate([_dot_nt(cm[g * seg:(g + 1) * seg], get_h(g)) for g in range(n_seg)], axis=0)
    y = y + per_head([jnp.exp(a[:, e:e + 1]) for e in range(B_HPG)], head_of_lane) * y_state
    y = (y + dskip_ref[...] * x) * _silu(z_ref[...])
    y_ref[...] = y * lax.rsqrt(jnp.mean(y * y, axis=1, keepdims=True) + EPS) * ng_ref[...]
    xw_t = (xdt * per_head([w_end[:, e:e + 1] for e in range(B_HPG)], head_of_lane)).T
    for g in range(n_seg):
        xg = xw_t if n_seg == 1 else jnp.where(seg_of_lane == g, xw_t, 0.0)
        keep = per_head([jnp.exp(a_end[g][:, e:e + 1]) for e in range(B_HPG)], head_of_row)
        set_h(g, keep * get_h(g) + _dot(xg.astype(BF16), bm_b))

    if carried:
        @pl.when(chunk == n_chunks - 1)
        def _():
            h1_ref[...] = h_st[...].reshape(n_seg, B_HPG, B_HEADDIM, B_STATE)


def ssd_mixer(grp, x, mod, state, w_in, conv_w, conv_b, dt_bias, a_log, d_skip, norm_g, w_out):
    st_h, st_conv = state
    n_main = B_INNER + B_CONV_DIM
    wb = w_in.astype(BF16)
    group_lanes = lambda v: jnp.pad(v.reshape(-1, B_GROUPS, B_HPG), ((0, 0), (0, 0), (0, LANES - B_HPG))).reshape(
        -1, B_GROUPS * LANES)
    p_main, dt_raw = norm_matmul(grp, x, mod, 1, 0, [wb[:, :n_main], group_lanes(wb[:, n_main:])])
    n_seg, seg, n_chunks, rows = grp.S, grp.L, grp.n_chunks, grp.R
    xo, bo, co = B_INNER // B_GW, (2 * B_INNER) // B_STATE, (2 * B_INNER + B_GN) // B_STATE
    cxo, cbo, cco = 0, B_INNER // B_STATE, (B_INNER + B_GN) // B_STATE
    row = lambda b, g, c: b * n_chunks + c
    hist = _pad_history(st_conv)
    scratch = [pltpu.VMEM((n_seg, SUBLANES + seg, B_GW), F32), pltpu.VMEM((n_seg, SUBLANES + seg, B_STATE), F32),
               pltpu.VMEM((n_seg, SUBLANES + seg, B_STATE), F32)]
    if n_chunks > 1:
        scratch.append(pltpu.VMEM((n_seg, B_GW, B_STATE), F32))
    state_spec = pl.BlockSpec((n_seg, B_HPG, B_HEADDIM, B_STATE), lambda b, g, c: (b, g, 0, 0))
    y, h1 = pl.pallas_call(
        functools.partial(_ssd_kernel, n_seg=n_seg, seg=seg, n_chunks=n_chunks),
        grid=(grp.n_blocks, B_GROUPS, n_chunks),
        in_specs=[
            pl.BlockSpec((rows, B_GW), lambda b, g, c: (row(b, g, c), g)),
            pl.BlockSpec((rows, B_GW), lambda b, g, c: (row(b, g, c), xo + g)),
            pl.BlockSpec((rows, B_STATE), lambda b, g, c: (row(b, g, c), bo + g)),
            pl.BlockSpec((rows, B_STATE), lambda b, g, c: (row(b, g, c), co + g)),
            pl.BlockSpec((rows, LANES), lambda b, g, c: (row(b, g, c), g)),
            pl.BlockSpec((n_seg, SUBLANES, B_GW), lambda b, g, c: (b, 0, cxo + g)),
            pl.BlockSpec((n_seg, SUBLANES, B_STATE), lambda b, g, c: (b, 0, cbo + g)),
            pl.BlockSpec((n_seg, SUBLANES, B_STATE), lambda b, g, c: (b, 0, cco + g)),
            state_spec,
            pl.BlockSpec((CONV_W, B_GW), lambda b, g, c: (0, cxo + g)),
            pl.BlockSpec((1, B_GW), lambda b, g, c: (0, cxo + g)),
            pl.BlockSpec((CONV_W, B_STATE), lambda b, g, c: (0, cbo + g)),
            pl.BlockSpec((1, B_STATE), lambda b, g, c: (0, cbo + g)),
            pl.BlockSpec((CONV_W, B_STATE), lambda b, g, c: (0, cco + g)),
            pl.BlockSpec((1, B_STATE), lambda b, g, c: (0, cco + g)),
            pl.BlockSpec((1, LANES), lambda b, g, c: (0, g)),
            pl.BlockSpec((1, LANES), lambda b, g, c: (0, g)),
            pl.BlockSpec((1, B_GW), lambda b, g, c: (0, g)),
            pl.BlockSpec((1, B_GW), lambda b, g, c: (0, g)),
        ],
        out_specs=[pl.BlockSpec((rows, B_GW), lambda b, g, c: (row(b, g, c), g)), state_spec],
        out_shape=[jax.ShapeDtypeStruct((grp.N, B_INNER), F32), jax.ShapeDtypeStruct(st_h.shape, F32)],
        scratch_shapes=scratch,
        compiler_params=_cparams("arbitrary", "arbitrary", "arbitrary"),
        name="ssd_chunk",
    )(p_main, p_main, p_main, p_main, dt_raw, hist, hist, hist, st_h,
      conv_w, conv_b.reshape(1, -1), conv_w, conv_b.reshape(1, -1), conv_w, conv_b.reshape(1, -1),
      group_lanes(dt_bias[None, :]), group_lanes(a_log[None, :]),
      jnp.repeat(d_skip, B_HEADDIM)[None, :], norm_g[None, :])
    x_new = matmul_residual(grp, y, w_out.astype(BF16), x, mod, 2)
    conv_new = p_main.reshape(grp.B, grp.T, n_main)[:, grp.T - (CONV_W - 1):, B_INNER:]
    return x_new, (h1, conv_new)


C_Q = C_HEADS * C_HD
C_KV = C_KV_HEADS * C_HD


def _rope_kernel(p_ref, cos_ref, sin_ref, qg_ref, kg_ref, q_ref, k_ref):
    tm = p_ref.shape[0]
    lane = lax.broadcasted_iota(I32, (tm, LANES), 1)
    low_head = lane < C_HD
    first_half = (lane & (C_HD - 1)) < C_HD // 2
    cos, sin = cos_ref[...], sin_ref[...]

    def norm_rope(xb, gain):
        sq = xb * xb
        s_lo = jnp.sum(jnp.where(low_head, sq, 0.0), axis=1, keepdims=True)
        s_hi = jnp.sum(jnp.where(low_head, 0.0, sq), axis=1, keepdims=True)
        ms = jnp.where(low_head, s_lo, s_hi) * (1.0 / C_HD)
        xn = xb * lax.rsqrt(ms + EPS) * gain
        partner = jnp.where(first_half, pltpu.roll(xn, LANES - C_HD // 2, axis=1), pltpu.roll(xn, C_HD // 2, axis=1))
        return xn * cos + partner * sin

    for j in range(C_Q // LANES):
        q_ref[:, j * LANES:(j + 1) * LANES] = norm_rope(p_ref[:, j * LANES:(j + 1) * LANES], qg_ref[...])
    for j in range(C_KV // LANES):
        k_ref[:, j * LANES:(j + 1) * LANES] = norm_rope(p_ref[:, C_Q + j * LANES:C_Q + (j + 1) * LANES], kg_ref[...])


def _swa_kernel(q_ref, k0_ref, k1_ref, v0_ref, v1_ref, sink_ref, o_ref, *, n_units, tq, blocks_per_seq):
    i = pl.program_id(0)
    rows = C_GROUP * tq
    t = lax.broadcasted_iota(I32, (rows, 2 * WINDOW), 0) & (tq - 1)
    s = lax.broadcasted_iota(I32, (rows, 2 * WINDOW), 1)
    valid = (s >= t) & (s <= t + WINDOW)
    if blocks_per_seq:
        valid = valid & (s >= jnp.where(lax.rem(i, blocks_per_seq) == 0, WINDOW, 0))

    def with_past(past, new):
        if tq < WINDOW:
            new = jnp.concatenate([new, jnp.zeros((WINDOW - tq, new.shape[1]), F32)], axis=0)
        return jnp.concatenate([past, new], axis=0)

    for u in range(n_units):
        kk = with_past(k0_ref[u], k1_ref[u])
        vv = with_past(v0_ref[u], v1_ref[u])
        qu = q_ref[u]
        outs = []
        for kh in range(C_KV_HEADS):
            kc = kk[:, kh * C_HD:(kh + 1) * C_HD].astype(BF16)
            vc = vv[:, kh * C_HD:(kh + 1) * C_HD].astype(BF16)
            qs = jnp.concatenate([qu[:, (kh * C_GROUP + j) * C_HD:(kh * C_GROUP + j + 1) * C_HD]
                                  for j in range(C_GROUP)], axis=0).astype(BF16)
            logits = jnp.where(valid, _dot_nt(qs, kc) * (C_HD ** -0.5), NEG_INF)
            sink = sink_ref[kh][:, 0:1]
            mx = jnp.maximum(jnp.max(logits, axis=1, keepdims=True), sink)
            pr = jnp.exp(logits - mx)
            den = jnp.sum(pr, axis=1, keepdims=True) + jnp.exp(sink - mx)
            o = _dot(pr.astype(BF16), vc) / den
            outs += [o[j * tq:(j + 1) * tq] for j in range(C_GROUP)]
        o_ref[u] = jnp.concatenate(outs, axis=1)


def swa_mixer(grp, x, mod, cache, pos0, w_in, q_g, k_g, sinks, w_out):
    (p,) = norm_matmul(grp, x, mod, 1, 0, [w_in.astype(BF16)])
    half = C_HD // 2
    inv = ROPE_THETA ** (-jnp.arange(half, dtype=F32) / half)
    ang = (pos0 + jnp.arange(grp.T)).astype(F32)[:, None] * inv[None, :]
    cos, sin = jnp.cos(ang), jnp.sin(ang)
    per_token = lambda a: jnp.tile(a, (grp.B, 1))
    cos_t = per_token(jnp.tile(cos, (1, LANES // half)))
    sin_t = per_token(jnp.tile(jnp.concatenate([-sin, sin], axis=1), (1, LANES // C_HD)))
    tm = grp.tm
    gains = lambda g: jnp.tile(g, LANES // C_HD)[None, :]
    qr, kr = pl.pallas_call(
        _rope_kernel,
        grid=(grp.N // tm,),
        in_specs=[
            pl.BlockSpec((tm, C_Q + 2 * C_KV), lambda i: (i, 0)),
            pl.BlockSpec((tm, LANES), lambda i: (i, 0)),
            pl.BlockSpec((tm, LANES), lambda i: (i, 0)),
            pl.BlockSpec((1, LANES), lambda i: (0, 0)),
            pl.BlockSpec((1, LANES), lambda i: (0, 0)),
        ],
        out_specs=[pl.BlockSpec((tm, C_Q), lambda i: (i, 0)), pl.BlockSpec((tm, C_KV), lambda i: (i, 0))],
        out_shape=[jax.ShapeDtypeStruct((grp.N, C_Q), F32), jax.ShapeDtypeStruct((grp.N, C_KV), F32)],
        compiler_params=_cparams("arbitrary"),
        name="qk_norm_rope",
    )(p, cos_t, sin_t, gains(q_g), gains(k_g))

    v_col = (C_Q + C_KV) // C_KV
    if cache is None:
        assert grp.T % WINDOW == 0
        tq, n_units, blocks_per_seq = WINDOW, 1, grp.T // WINDOW
        n_steps = grp.N // tq
        k3 = kr.reshape(n_steps, tq, C_KV)
        p3 = p.reshape(n_steps, tq, C_Q + 2 * C_KV)
        k_args = (k3, k3, p3, p3)
        prev = lambda i: (jnp.maximum(i - 1, 0), 0, 0)
        k_specs = [
            pl.BlockSpec((1, tq, C_KV), prev),
            pl.BlockSpec((1, tq, C_KV), lambda i: (i, 0, 0)),
            pl.BlockSpec((1, tq, C_KV), lambda i: (jnp.maximum(i - 1, 0), 0, v_col)),
            pl.BlockSpec((1, tq, C_KV), lambda i: (i, 0, v_col)),
        ]
    else:
        tq, n_units, blocks_per_seq = grp.T, SUBLANES, 0
        n_steps = grp.B // n_units
        k_args = (cache[0].reshape(grp.B, WINDOW, C_KV), kr.reshape(grp.B, tq, C_KV),
                  cache[1].reshape(grp.B, WINDOW, C_KV), p.reshape(grp.B, tq, C_Q + 2 * C_KV))
        k_specs = [
            pl.BlockSpec((n_units, WINDOW, C_KV), lambda i: (i, 0, 0)),
            pl.BlockSpec((n_units, tq, C_KV), lambda i: (i, 0, 0)),
            pl.BlockSpec((n_units, WINDOW, C_KV), lambda i: (i, 0, 0)),
            pl.BlockSpec((n_units, tq, C_KV), lambda i: (i, 0, v_col)),
        ]
    sink_rows = jnp.broadcast_to(sinks.reshape(C_KV_HEADS, C_GROUP, 1, 1),
                                 (C_KV_HEADS, C_GROUP, tq, LANES)).reshape(C_KV_HEADS, C_GROUP * tq, LANES)
    o = pl.pallas_call(
        functools.partial(_swa_kernel, n_units=n_units, tq=tq, blocks_per_seq=blocks_per_seq),
        grid=(n_steps,),
        in_specs=[pl.BlockSpec((n_units, tq, C_Q), lambda i: (i, 0, 0))] + k_specs
        + [pl.BlockSpec(sink_rows.shape, lambda i: (0, 0, 0))],
        out_specs=pl.BlockSpec((n_units, tq, C_Q), lambda i: (i, 0, 0)),
        out_shape=jax.ShapeDtypeStruct((grp.N // tq, tq, C_Q), F32),
        compiler_params=_cparams("arbitrary"),
        name="swa_attention",
    )(qr.reshape(grp.N // tq, tq, C_Q), *k_args, sink_rows)
    x_new = matmul_residual(grp, o.reshape(grp.N, C_Q), w_out.astype(BF16), x, mod, 2)
    k_new = kr.reshape(grp.B, grp.T, C_KV_HEADS, C_HD)
    v_new = p[:, C_Q + C_KV:].reshape(grp.B, grp.T, C_KV_HEADS, C_HD)
    if cache is None:
        return x_new, (k_new[:, grp.T - WINDOW:], v_new[:, grp.T - WINDOW:])
    return x_new, (jnp.concatenate([cache[0][:, grp.T:], k_new], axis=1),
                   jnp.concatenate([cache[1][:, grp.T:], v_new], axis=1))


def kernel(x_prompt, x_sample, c_prompt, c_sample, state_mlstm_C, state_mlstm_n, state_mlstm_m, state_mlstm_conv,
           state_ssm, state_ssm_conv, cache_swa_k, cache_swa_v, ada_w, ada_b, mlstm_w_in, mlstm_conv_w,
           mlstm_conv_b, mlstm_gate_b, mlstm_norm_g, mlstm_w_out, ssd_w_in, ssd_conv_w, ssd_conv_b, ssd_dt_bias,
           ssd_a_log, ssd_d_skip, ssd_norm_g, ssd_w_out, swa_w_in, swa_q_norm_g, swa_k_norm_g, swa_sinks,
           swa_w_out, moe_w_router, moe_b_router, moe_w1, moe_b1, moe_w2, moe_b2):
    groups = [Group(*x_prompt.shape[:2]), Group(*x_sample.shape[:2])]
    n_prompt = groups[0].B
    xs = [x_prompt.reshape(-1, D_MODEL), x_sample.reshape(-1, D_MODEL)]
    mod_all = ada_modulation(jnp.concatenate([c_prompt, c_sample], axis=0), ada_w, ada_b)
    fresh = lambda s: jnp.zeros((n_prompt,) + s.shape[2:], F32)
    new = [[[] for _ in range(8)] for _ in groups]
    for layer in range(DEPTH):
        kind, j = layer % 3, layer // 3
        mods = [groups[0].expand_mod(mod_all[layer, :n_prompt]), groups[1].expand_mod(mod_all[layer, n_prompt:])]
        for gi, grp in enumerate(groups):
            if kind == 0:
                state = (state_mlstm_n, state_mlstm_m, state_mlstm_conv)
                state = tuple(s[j] if gi else fresh(s) for s in state)
                state = ((state_mlstm_C, j) if gi else (fresh(state_mlstm_C)[None], 0),) + state
                xs[gi], st = mlstm_mixer(grp, xs[gi], mods[gi], state, mlstm_w_in[j], mlstm_conv_w[j], mlstm_conv_b[j],
                                         mlstm_gate_b[j], mlstm_norm_g[j], mlstm_w_out[j])
                first = 0
            elif kind == 1:
                state = tuple(s[j] if gi else fresh(s) for s in (state_ssm, state_ssm_conv))
                xs[gi], st = ssd_mixer(grp, xs[gi], mods[gi], state, ssd_w_in[j], ssd_conv_w[j], ssd_conv_b[j],
                                       ssd_dt_bias[j], ssd_a_log[j], ssd_d_skip[j], ssd_norm_g[j], ssd_w_out[j])
                first = 4
            else:
                cache = (cache_swa_k[j], cache_swa_v[j]) if gi else None
                xs[gi], st = swa_mixer(grp, xs[gi], mods[gi], cache, PAST_LEN if gi else 0, swa_w_in[j],
                                       swa_q_norm_g[j], swa_k_norm_g[j], swa_sinks[j], swa_w_out[j])
                first = 6
            for offset, s in enumerate(st):
                new[gi][first + offset].append(s)
        xs = moe_layer(groups, xs, mods, layer, moe_w_router, moe_b_router, moe_w1, moe_b1, moe_w2, moe_b2)
    outs = [xs[0].reshape(x_prompt.shape), xs[1].reshape(x_sample.shape)]
    for slot in range(8):
        outs += [jnp.stack(new[0][slot]), jnp.stack(new[1][slot])]
    return tuple(outs)
```

```python
import functools
import math

import jax
import jax.numpy as jnp
from jax import lax
from jax.experimental import pallas as pl
from jax.experimental.pallas import tpu as pltpu

F32 = jnp.float32
BF16 = jnp.bfloat16
I32 = jnp.int32

D_MODEL = 1024
DEPTH = 4
PAST_LEN = 8192
EPS = 1e-6
CONV_W = 4
A_HEADS = 4
A_DK = D_MODEL // 8
A_DV = D_MODEL // A_HEADS
A_QK = A_HEADS * A_DK
A_V = A_HEADS * A_DV
B_INNER = 2 * D_MODEL
B_HEADDIM = 64
B_HEADS = B_INNER // B_HEADDIM
B_STATE = 128
B_GROUPS = 8
B_GN = B_GROUPS * B_STATE
B_CONV_DIM = B_INNER + 2 * B_GN
B_HPG = B_HEADS // B_GROUPS
B_GW = B_HPG * B_HEADDIM
C_HEADS = 16
C_KV_HEADS = 4
C_GROUP = C_HEADS // C_KV_HEADS
C_HD = 64
WINDOW = 128
ROPE_THETA = 10000.0
N_EXPERTS = 32
TOP_K = 4
D_FF = D_MODEL
SWIGLU_LIMIT = 7.0
SWIGLU_ALPHA = 1.702

LANES = 128
SUBLANES = 8
VMEM_LIMIT_BYTES = 56 * 1024 * 1024

ROW_TILE = 256
SEQ_CHUNK = 256
SHORT_BLOCK = 128
MOE_TILE = 512
FF_CHUNK = 1024
NEG_INF = float("-inf")
ROW_AS_TILE = (SUBLANES, D_MODEL // SUBLANES)


def _cparams(*sem):
    return pltpu.CompilerParams(dimension_semantics=sem, vmem_limit_bytes=VMEM_LIMIT_BYTES)


def _silu(x):
    return x * jax.nn.sigmoid(x)


def _dot(a, b):
    return jnp.dot(a, b, preferred_element_type=F32)


def _dot_nt(a, b):
    return lax.dot_general(a, b, (((1,), (1,)), ((), ())), preferred_element_type=F32)


class Group:
    def __init__(self, B, T):
        self.B, self.T = B, T
        self.N = B * T
        self.long = T % SEQ_CHUNK == 0
        if self.long:
            self.S, self.L = 1, SEQ_CHUNK
        else:
            assert T == SUBLANES and self.N % SHORT_BLOCK == 0
            self.S, self.L = SHORT_BLOCK // T, T
        self.R = self.S * self.L
        self.n_blocks = B // self.S
        self.n_chunks = T // self.L
        self.tm = min(ROW_TILE, self.N)
        assert self.N % self.tm == 0 and (not self.long or T % self.tm == 0)

    def expand_mod(self, mod):
        if self.long:
            return mod.reshape(self.B, 1, 6 * D_MODEL)
        return jnp.repeat(mod, self.T, axis=0)

    def mod_spec(self, j, rows, row_block_fn):
        if self.long:
            per_seq = self.T // rows
            return pl.BlockSpec((None, 1, D_MODEL), lambda *g: (row_block_fn(*g) // per_seq, 0, j))
        return pl.BlockSpec((rows, D_MODEL), lambda *g: (row_block_fn(*g), j))


def _ada_kernel(c_ref, w_ref, b_ref, o_ref):
    cs = _silu(c_ref[...])
    o_ref[...] = _dot(cs.astype(BF16), w_ref[...].astype(BF16)) + b_ref[...]


def ada_modulation(c_all, ada_w, ada_b):
    rows = c_all.shape[0]
    tn = 1536
    n_out = 6 * D_MODEL
    return pl.pallas_call(
        _ada_kernel,
        grid=(DEPTH, n_out // tn),
        in_specs=[
            pl.BlockSpec((rows, D_MODEL), lambda l, j: (0, 0)),
            pl.BlockSpec((None, D_MODEL, tn), lambda l, j: (l, 0, j)),
            pl.BlockSpec((None, 1, tn), lambda l, j: (l, 0, j)),
        ],
        out_specs=pl.BlockSpec((None, rows, tn), lambda l, j: (l, 0, j)),
        out_shape=jax.ShapeDtypeStruct((DEPTH, rows, n_out), F32),
        compiler_params=_cparams("arbitrary", "arbitrary"),
        name="ada_modulation",
    )(c_all, ada_w, ada_b.reshape(DEPTH, 1, n_out))


def _modulated_norm(x, sc, sh):
    ms = jnp.mean(x * x, axis=-1, keepdims=True)
    return (x * lax.rsqrt(ms + EPS)) * (1.0 + sc) + sh


def _norm_mm_kernel(x_ref, sc_ref, sh_ref, *refs, n_w):
    h = _modulated_norm(x_ref[...], sc_ref[...], sh_ref[...]).astype(BF16)
    for w_ref, o_ref in zip(refs[:n_w], refs[n_w:]):
        o_ref[...] = _dot(h, w_ref[...])


def norm_matmul(grp, x, mod, j_scale, j_shift, weights):
    tm = grp.tm
    n_w = len(weights)
    in_specs = [
        pl.BlockSpec((tm, D_MODEL), lambda i: (i, 0)),
        grp.mod_spec(j_scale, tm, lambda i: i),
        grp.mod_spec(j_shift, tm, lambda i: i),
    ] + [pl.BlockSpec(w.shape, lambda i: (0, 0)) for w in weights]
    out_specs = [pl.BlockSpec((tm, w.shape[1]), lambda i: (i, 0)) for w in weights]
    out_shape = [jax.ShapeDtypeStruct((grp.N, w.shape[1]), F32) for w in weights]
    return pl.pallas_call(
        functools.partial(_norm_mm_kernel, n_w=n_w),
        grid=(grp.N // tm,),
        in_specs=in_specs,
        out_specs=out_specs,
        out_shape=out_shape,
        compiler_params=_cparams("arbitrary"),
        name="norm_matmul",
    )(x, mod, mod, *weights)


def _mm_res_kernel(y_ref, w_ref, x_ref, g_ref, o_ref):
    o_ref[...] = x_ref[...] + g_ref[...] * _dot(y_ref[...].astype(BF16), w_ref[...])


def matmul_residual(grp, y, w, x, mod, j_gate):
    tm = grp.tm
    k = y.shape[1]
    return pl.pallas_call(
        _mm_res_kernel,
        grid=(grp.N // tm,),
        in_specs=[
            pl.BlockSpec((tm, k), lambda i: (i, 0)),
            pl.BlockSpec(w.shape, lambda i: (0, 0)),
            pl.BlockSpec((tm, D_MODEL), lambda i: (i, 0)),
            grp.mod_spec(j_gate, tm, lambda i: i),
        ],
        out_specs=pl.BlockSpec((tm, D_MODEL), lambda i: (i, 0)),
        out_shape=jax.ShapeDtypeStruct((grp.N, D_MODEL), F32),
        compiler_params=_cparams("arbitrary"),
        name="matmul_residual",
    )(y, w, x, mod)


def _router_kernel(x_ref, sc_ref, sh_ref, wr_ref, br_ref, cnt_in_ref,
                   h_ref, ri_ref, rg_ref, cnt_out_ref, cnt_scr):
    i = pl.program_id(0)
    tm = x_ref.shape[0]

    @pl.when(i == 0)
    def _():
        cnt_scr[...] = cnt_in_ref[...]

    h = _modulated_norm(x_ref[...], sc_ref[...], sh_ref[...])
    h_ref[...] = h.reshape(tm, *ROW_AS_TILE)
    h_hi = h.astype(BF16)
    h_lo = (h - h_hi.astype(F32)).astype(BF16)
    hi_terms = _dot(h_hi, wr_ref[...])
    logits = (hi_terms[:, :LANES] + hi_terms[:, LANES:]) + _dot(h_lo, wr_ref[:, :LANES]) + br_ref[...]
    lane = lax.broadcasted_iota(I32, (tm, LANES), 1)
    lane_f = lane.astype(F32)
    work = jnp.where(lane < N_EXPERTS, logits, NEG_INF)
    top_v, top_sel, top_i = [], [], []
    for _ in range(TOP_K):
        mx = jnp.max(work, axis=1, keepdims=True)
        idx = jnp.min(jnp.where(work == mx, lane_f, float(LANES)), axis=1, keepdims=True)
        sel = lane_f == idx
        work = jnp.where(sel, NEG_INF, work)
        top_v.append(mx)
        top_sel.append(sel)
        top_i.append(idx)
    ex = [jnp.exp(v - top_v[0]) for v in top_v]
    inv = 1.0 / (ex[0] + ex[1] + ex[2] + ex[3])
    chosen = jnp.zeros((tm, LANES), F32)
    for sel in top_sel:
        chosen = jnp.where(sel, 1.0, chosen)
    r = lax.broadcasted_iota(I32, (tm, tm), 0)
    c = lax.broadcasted_iota(I32, (tm, tm), 1)
    before = jnp.where(c < r, 1.0, 0.0).astype(BF16)
    rank_all = _dot(before, chosen.astype(BF16)) + cnt_scr[...]
    ri = jnp.zeros((tm, LANES), I32)
    rg = jnp.zeros((tm, LANES), F32)
    for k in range(TOP_K):
        rank_k = jnp.sum(jnp.where(top_sel[k], rank_all, 0.0), axis=1, keepdims=True)
        ri = jnp.where(lane == k, top_i[k].astype(I32), ri)
        ri = jnp.where(lane == TOP_K + k, rank_k.astype(I32), ri)
        rg = jnp.where(lane == k, ex[k] * inv, rg)
    ri_ref[...] = ri
    rg_ref[...] = rg
    cnt_scr[...] = cnt_scr[...] + jnp.sum(chosen, axis=0, keepdims=True)
    cnt_out_ref[...] = cnt_scr[...]


def moe_router(grp, x, mod, w_r, b_r, cnt_in):
    tm = grp.tm
    return pl.pallas_call(
        _router_kernel,
        grid=(grp.N // tm,),
        in_specs=[
            pl.BlockSpec((tm, D_MODEL), lambda i: (i, 0)),
            grp.mod_spec(4, tm, lambda i: i),
            grp.mod_spec(3, tm, lambda i: i),
            pl.BlockSpec((D_MODEL, 2 * LANES), lambda i: (0, 0)),
            pl.BlockSpec((1, LANES), lambda i: (0, 0)),
            pl.BlockSpec((1, LANES), lambda i: (0, 0)),
        ],
        out_specs=[
            pl.BlockSpec((tm,) + ROW_AS_TILE, lambda i: (i, 0, 0)),
            pl.BlockSpec((tm, LANES), lambda i: (i, 0)),
            pl.BlockSpec((tm, LANES), lambda i: (i, 0)),
            pl.BlockSpec((1, LANES), lambda i: (0, 0)),
        ],
        out_shape=[
            jax.ShapeDtypeStruct((grp.N,) + ROW_AS_TILE, F32),
            jax.ShapeDtypeStruct((grp.N, LANES), I32),
            jax.ShapeDtypeStruct((grp.N, LANES), F32),
            jax.ShapeDtypeStruct((1, LANES), F32),
        ],
        scratch_shapes=[pltpu.VMEM((1, LANES), F32)],
        compiler_params=_cparams("arbitrary"),
        name="moe_router",
    )(x, mod, mod, w_r, b_r, cnt_in)


def _row_copy(src_ref, src_row, dst_ref, dst_row, sem):
    return pltpu.make_async_copy(src_ref.at[src_row], dst_ref.at[dst_row], sem)


DMA_UNROLL = 8


def _dispatch_kernel(pos_ref, cnt_ref, off_ref, nv_ref, *refs, tiles, n_sorted_tiles):
    n_groups = len(tiles)
    h_refs = refs[:n_groups]
    xs_ref, zero_scr, sem = refs[n_groups:]
    i = pl.program_id(0)
    tm = h_refs[0].shape[0]
    base = (i - 1) * (tm * TOP_K)

    def zero_tile(j):
        return pltpu.make_async_copy(zero_scr, xs_ref.at[pl.ds(j * MOE_TILE, MOE_TILE)], sem)

    def for_each_partial_tile(action):
        def body(e, carry):
            n = cnt_ref[e]

            @pl.when((n & (MOE_TILE - 1)) != 0)
            def _():
                action(zero_tile((off_ref[e] + n) // MOE_TILE))

            return carry

        lax.fori_loop(0, N_EXPERTS, body, 0)

        def tail(j, carry):
            action(zero_tile(j))
            return carry

        lax.fori_loop(nv_ref[0], n_sorted_tiles, tail, 0)

    @pl.when(i == 0)
    def _():
        zero_scr[...] = jnp.zeros(zero_scr.shape, F32)
        for_each_partial_tile(lambda cp: cp.start())
        for_each_partial_tile(lambda cp: cp.wait())

    def scatter_tile(h_ref):
        def start(t, carry):
            for k in range(TOP_K):
                _row_copy(h_ref, t, xs_ref, pos_ref[base + t * TOP_K + k], sem).start(priority=k % 2)
            return carry

        lax.fori_loop(0, tm, start, 0, unroll=DMA_UNROLL)
        for _ in range(TOP_K):
            pltpu.make_async_copy(h_ref, xs_ref.at[pl.ds(0, tm)], sem).wait()

    first = 1
    for h_ref, n in zip(h_refs, tiles):
        pl.when((i >= first) & (i < first + n))(functools.partial(scatter_tile, h_ref))
        first += n


def moe_dispatch(groups, hs, pos_all, counts, row_off, n_valid, n_sorted_tiles):
    tm = groups[0].tm
    assert all(g.tm == tm for g in groups)
    tiles = tuple(g.N // tm for g in groups)
    starts = [1 + sum(tiles[:gi]) for gi in range(len(tiles))]

    def h_spec(first, n):
        return pl.BlockSpec((tm,) + ROW_AS_TILE, lambda i, *_: (jnp.clip(i - first, 0, n - 1), 0, 0))

    return pl.pallas_call(
        functools.partial(_dispatch_kernel, tiles=tiles, n_sorted_tiles=n_sorted_tiles),
        grid_spec=pltpu.PrefetchScalarGridSpec(
            num_scalar_prefetch=4,
            grid=(sum(tiles) + 1,),
            in_specs=[h_spec(first, n) for first, n in zip(starts, tiles)],
            out_specs=pl.BlockSpec(memory_space=pl.ANY),
            scratch_shapes=[pltpu.VMEM((MOE_TILE,) + ROW_AS_TILE, F32), pltpu.SemaphoreType.DMA],
        ),
        out_shape=jax.ShapeDtypeStruct((n_sorted_tiles * MOE_TILE,) + ROW_AS_TILE, F32),
        compiler_params=_cparams("arbitrary"),
        name="moe_dispatch",
    )(pos_all, counts, row_off, n_valid, *hs)


def _combine_kernel(pos_ref, ys_ref, rg_ref, x_ref, g_ref, o_ref, ybuf, sems):
    i = pl.program_id(0)
    n = pl.num_programs(0)
    tm = x_ref.shape[0]

    def gather_tile(tile, slot):
        base = tile * (tm * TOP_K)

        def start(t, carry):
            for k in range(TOP_K):
                _row_copy(ys_ref, pos_ref[base + t * TOP_K + k], ybuf.at[slot, k], t,
                          sems.at[slot]).start(priority=k % 2)
            return carry

        lax.fori_loop(0, tm, start, 0, unroll=DMA_UNROLL)

    slot = lax.rem(i, 2)

    @pl.when(i == 0)
    def _():
        gather_tile(0, 0)

    @pl.when(i + 1 < n)
    def _():
        gather_tile(i + 1, 1 - slot)

    for k in range(TOP_K):
        pltpu.make_async_copy(ys_ref.at[pl.ds(0, tm)], ybuf.at[slot, k], sems.at[slot]).wait()
    rg = rg_ref[...]
    acc = rg[:, 0:1] * ybuf[slot, 0].reshape(tm, D_MODEL)
    for k in range(1, TOP_K):
        acc = acc + rg[:, k:k + 1] * ybuf[slot, k].reshape(tm, D_MODEL)
    o_ref[...] = x_ref[...] + g_ref[...] * acc


def moe_combine(grp, ys, pos, route_g, x, mod):
    tm = grp.tm
    return pl.pallas_call(
        _combine_kernel,
        grid_spec=pltpu.PrefetchScalarGridSpec(
            num_scalar_prefetch=1,
            grid=(grp.N // tm,),
            in_specs=[
                pl.BlockSpec(memory_space=pl.ANY),
                pl.BlockSpec((tm, LANES), lambda i, p: (i, 0)),
                pl.BlockSpec((tm, D_MODEL), lambda i, p: (i, 0)),
                grp.mod_spec(5, tm, lambda i, p: i),
            ],
            out_specs=pl.BlockSpec((tm, D_MODEL), lambda i, p: (i, 0)),
            scratch_shapes=[pltpu.VMEM((2, TOP_K, tm) + ROW_AS_TILE, F32), pltpu.SemaphoreType.DMA((2,))],
        ),
        out_shape=jax.ShapeDtypeStruct((grp.N, D_MODEL), F32),
        compiler_params=_cparams("arbitrary"),
        name="moe_combine",
    )(pos, ys, route_g, x, mod)


def _experts_kernel(te_ref, tb_ref, nv_ref, to_ref, seq_ref, nu_ref, x_ref, w1_ref, b1_ref, w2_ref, b2_ref, y_ref,
                    w1_scr, w2_scr, w1_stage, w2_stage, sems, *, layer):
    i = pl.program_id(0)

    def weight_copies(ordinal):
        e = seq_ref[ordinal]
        slot = lax.rem(ordinal, 2)
        return (pltpu.make_async_copy(w1_ref.at[layer, e], w1_stage.at[slot], sems.at[slot]),
                pltpu.make_async_copy(w2_ref.at[layer, e], w2_stage.at[slot], sems.at[slot]))

    @pl.when(i == 0)
    def _():
        for cp in weight_copies(0):
            cp.start()

    @pl.when(i < nv_ref[0])
    def _():
        ordinal = to_ref[i]

        @pl.when((i == 0) | (ordinal != to_ref[jnp.maximum(i - 1, 0)]))
        def _():
            @pl.when(ordinal + 1 < nu_ref[0])
            def _():
                for cp in weight_copies(ordinal + 1):
                    cp.start()

            for cp in weight_copies(ordinal):
                cp.wait()
            slot = lax.rem(ordinal, 2)
            w1_scr[...] = w1_stage[slot].astype(BF16)
            w2_scr[...] = w2_stage[slot].astype(BF16)

        xb = x_ref[...].reshape(MOE_TILE, D_MODEL).astype(BF16)
        y = b2_ref[...]
        for j in range(D_FF // FF_CHUNK):
            cols = slice(j * FF_CHUNK, (j + 1) * FF_CHUNK)
            ucols = slice(D_FF + j * FF_CHUNK, D_FF + (j + 1) * FF_CHUNK)
            g = jnp.minimum(_dot(xb, w1_scr[:, cols]) + b1_ref[:, cols], SWIGLU_LIMIT)
            u = jnp.clip(_dot(xb, w1_scr[:, ucols]) + b1_ref[:, ucols], -SWIGLU_LIMIT, SWIGLU_LIMIT)
            act = (u + 1.0) * (g * jax.nn.sigmoid(SWIGLU_ALPHA * g))
            y = y + _dot(act.astype(BF16), w2_scr[cols, :])
        y_ref[...] = y.reshape(y_ref.shape)

    @pl.when(i >= nv_ref[0])
    def _():
        y_ref[...] = jnp.zeros(y_ref.shape, F32)


def moe_experts(xs, tile_expert, tile_block, n_valid, tiles_per, w1, b1, w2, b2, layer):
    n_tiles = tile_expert.shape[0]
    tm = MOE_TILE
    used = tiles_per > 0
    ordinal_of = jnp.cumsum(used.astype(I32)) - 1
    experts = jnp.arange(N_EXPERTS, dtype=I32)
    seq = jnp.sum(jnp.where(used[None, :] & (ordinal_of[None, :] == experts[:, None]), experts[None, :], 0), axis=1)
    tile_ordinal = jnp.sum(jnp.where(tile_expert[:, None] == experts[None, :], ordinal_of[None, :], 0), axis=1)
    n_used = jnp.sum(used.astype(I32)).reshape(1)
    bias_map = lambda i, te, *_: (layer, te[i], 0, 0)
    return pl.pallas_call(
        functools.partial(_experts_kernel, layer=layer),
        grid_spec=pltpu.PrefetchScalarGridSpec(
            num_scalar_prefetch=6,
            grid=(n_tiles,),
            in_specs=[
                pl.BlockSpec((tm,) + ROW_AS_TILE, lambda i, te, tb, *_: (tb[i], 0, 0)),
                pl.BlockSpec(memory_space=pl.ANY),
                pl.BlockSpec((None, None, 1, 2 * D_FF), bias_map),
                pl.BlockSpec(memory_space=pl.ANY),
                pl.BlockSpec((None, None, 1, D_MODEL), bias_map),
            ],
            out_specs=pl.BlockSpec((tm,) + ROW_AS_TILE, lambda i, *_: (i, 0, 0)),
            scratch_shapes=[pltpu.VMEM((D_MODEL, 2 * D_FF), BF16), pltpu.VMEM((D_FF, D_MODEL), BF16),
                            pltpu.VMEM((2, D_MODEL, 2 * D_FF), F32), pltpu.VMEM((2, D_FF, D_MODEL), F32),
                            pltpu.SemaphoreType.DMA((2,))],
        ),
        out_shape=jax.ShapeDtypeStruct(xs.shape, F32),
        compiler_params=_cparams("arbitrary"),
        name="moe_experts",
    )(tile_expert, tile_block, n_valid, tile_ordinal, seq, n_used, xs, w1,
      b1.reshape(DEPTH, N_EXPERTS, 1, 2 * D_FF), w2, b2.reshape(DEPTH, N_EXPERTS, 1, D_MODEL))


def moe_layer(groups, xs_in, mods, layer, w_router, b_router, w1, b1, w2, b2):
    n_total = sum(g.N for g in groups)
    n_pairs = n_total * TOP_K
    n_tiles = -(-n_pairs // MOE_TILE) + N_EXPERTS
    w_r = jnp.pad(w_router[layer], ((0, 0), (0, LANES - N_EXPERTS)))
    w_hi = w_r.astype(BF16)
    w_r = jnp.concatenate([w_hi, (w_r - w_hi.astype(F32)).astype(BF16)], axis=1)
    b_r =jnp.pad(b_router[layer], (0, LANES - N_EXPERTS)).reshape(1, LANES)
    cnt = jnp.zeros((1, LANES), F32)
    routed = []
    for grp, x, mod in zip(groups, xs_in, mods):
        h, route_i, route_g, cnt = moe_router(grp, x, mod, w_r, b_r, cnt)
        routed.append((h, route_i, route_g))
    counts = cnt[0, :N_EXPERTS].astype(I32)
    tiles_per = (counts + MOE_TILE - 1) // MOE_TILE
    tile_end = jnp.cumsum(tiles_per)
    tile_start = tile_end - tiles_per
    n_valid = tile_end[-1:]
    tidx = jnp.minimum(jnp.arange(n_tiles, dtype=I32), n_valid - 1)
    tile_expert = jnp.sum((tile_end[None, :] <= tidx[:, None]).astype(I32), axis=1)
    row_off = tile_start * MOE_TILE
    positions = []
    for _, route_i, _ in routed:
        route_e, rank = route_i[:, :TOP_K], route_i[:, TOP_K:2 * TOP_K]
        onehot = route_e[:, :, None] == jnp.arange(N_EXPERTS, dtype=I32)[None, None, :]
        positions.append((jnp.sum(jnp.where(onehot, row_off[None, None, :], 0), axis=2) + rank).reshape(-1))
    xs = moe_dispatch(groups, [h for h, _, _ in routed], jnp.concatenate(positions), counts, row_off, n_valid,
                      n_tiles)
    ys = moe_experts(xs, tile_expert, tidx, n_valid, tiles_per, w1, b1, w2, b2, layer)
    return [moe_combine(grp, ys, pos, route_g, x, mod)
            for grp, x, mod, pos, (_, _, route_g) in zip(groups, xs_in, mods, positions, routed)]


def _seg_cumsum(x, seg, reverse=False):
    rows = x.shape[0]
    row = lax.broadcasted_iota(I32, x.shape, 0) & (seg - 1)
    sh = 1
    while sh < seg:
        if reverse:
            x = x + jnp.where(row + sh < seg, pltpu.roll(x, rows - sh, axis=0), 0.0)
        else:
            x = x + jnp.where(row >= sh, pltpu.roll(x, sh, axis=0), 0.0)
        sh *= 2
    return x


def _segment_mask(rows, seg):
    ti = lax.broadcasted_iota(I32, (rows, rows), 0)
    si = lax.broadcasted_iota(I32, (rows, rows), 1)
    mask = si <= ti
    if seg < rows:
        shift = seg.bit_length() - 1
        mask = mask & ((ti >> shift) == (si >> shift))
    return mask


def _causal_conv_silu(x, xp_scr, w_ref, b_ref, n_seg, seg, carry):
    rows, ch = x.shape
    xp_scr[:, SUBLANES:, :] = x.reshape(n_seg, seg, ch)
    acc = b_ref[...]
    for i in range(CONV_W):
        back = CONV_W - 1 - i
        xs = x if back == 0 else xp_scr[:, SUBLANES - back:SUBLANES - back + seg, :].reshape(rows, ch)
        acc = acc + xs * w_ref[i:i + 1, :]
    if carry:
        xp_scr[:, SUBLANES - (CONV_W - 1):SUBLANES, :] = xp_scr[:, SUBLANES + seg - (CONV_W - 1):SUBLANES + seg, :]
    return _silu(acc)


def _pad_history(buf):
    return jnp.pad(buf, ((0, 0), (SUBLANES - (CONV_W - 1), 0), (0, 0)))


def _mlstm_kernel(p_ref, gi_ref, gf_ref, hist_ref, c0_ref, n0_ref, m0_ref, cw_ref, cb_ref, gbi_ref, gbf_ref,
                  ng_ref, x_ref, gate_ref, wo_ref, y_ref, c1_ref, n1_ref, m1_ref, xp_scr, *state_scr,
                  n_seg, seg, n_chunks):
    rows = n_seg * seg
    chunk = pl.program_id(1)
    carried = n_chunks > 1
    if carried:
        c_st, n_st, m_st = state_scr

        @pl.when(chunk == 0)
        def _():
            c_st[...] = c0_ref[...]
            n_st[...] = n0_ref[...]
            m_st[...] = m0_ref[...]
            xp_scr[:, 0:SUBLANES, :] = hist_ref[...]

        c_in, n_in, m_in, c_out, n_out, m_out = c_st, n_st, m_st, c_st, n_st, m_st
    else:
        xp_scr[:, 0:SUBLANES, :] = hist_ref[...]
        c_in, n_in, m_in, c_out, n_out, m_out = c0_ref, n0_ref, m0_ref, c1_ref, n1_ref, m1_ref

    qk = _causal_conv_silu(p_ref[:, :2 * A_QK], xp_scr, cw_ref, cb_ref, n_seg, seg, carried)
    log_i = gi_ref[...] + gbi_ref[...]
    f_pre = gf_ref[...] + gbf_ref[...]
    log_f = jnp.minimum(f_pre, 0.0) - jnp.log1p(jnp.exp(-jnp.abs(f_pre)))
    a = _seg_cumsum(log_f, seg)
    a_rev = _seg_cumsum(log_f, seg, reverse=True)
    m0 = m_in[...]
    inter3 = a.reshape(n_seg, seg, LANES) + m0
    inter = inter3.reshape(rows, LANES)
    src3 = (a_rev - log_f + log_i).reshape(n_seg, seg, LANES)
    inter_end = inter3[:, seg - 1:seg, :]
    m_end = jnp.maximum(inter_end, jnp.max(src3, axis=1, keepdims=True))
    w_old = jnp.exp(inter_end - m_end)
    w_src = jnp.exp(src3 - m_end).reshape(rows, LANES)
    b_t = (a - log_i).T
    mask = _segment_mask(rows, seg)
    seg_of_lane = lax.broadcasted_iota(I32, (A_DK, rows), 1) >> (seg.bit_length() - 1)

    out = None
    for h in range(A_HEADS):
        q = qk[:, h * A_DK:(h + 1) * A_DK] * (A_DK ** -0.5)
        k = qk[:, A_QK + h * A_DK:A_QK + (h + 1) * A_DK]
        vb = p_ref[:, 2 * A_QK + h * A_DV:2 * A_QK + (h + 1) * A_DV].astype(BF16)
        qb = q.astype(BF16)
        dmat = jnp.where(mask, a[:, h:h + 1] - b_t[h:h + 1, :], NEG_INF)
        m_tok = jnp.maximum(inter[:, h:h + 1], jnp.max(dmat, axis=1, keepdims=True))
        s = _dot_nt(qb, k.astype(BF16)) * jnp.exp(dmat - m_tok)
        w_inter = jnp.exp(inter[:, h:h + 1] - m_tok)
        if n_seg == 1:
            q_c = _dot(qb, c_in[0, h].astype(BF16))
        else:
            q_c = jnp.concatenate([_dot(q[g * seg:(g + 1) * seg], c_in[g, h]) for g in range(n_seg)], axis=0)
        n3 = n_in[:, h:h + 1, :]
        n_tok = jnp.broadcast_to(n3, (n_seg, seg, A_DK)).reshape(rows, A_DK)
        num = _dot(s.astype(BF16), vb) + w_inter * q_c
        den = jnp.sum(s, axis=1, keepdims=True) + w_inter * jnp.sum(q * n_tok, axis=1, keepdims=True)
        hv = num / jnp.maximum(jnp.abs(den), jnp.exp(-m_tok))
        hn = hv * lax.rsqrt(jnp.mean(hv * hv, axis=1, keepdims=True) + EPS) * ng_ref[h:h + 1, :]
        o = jax.nn.sigmoid(p_ref[:, 2 * A_QK + A_V + h * A_DV:2 * A_QK + A_V + (h + 1) * A_DV])
        head_out = _dot((o * hn).astype(BF16), wo_ref[h * A_DV:(h + 1) * A_DV, :])
        out = head_out if out is None else out + head_out
        kw = k * w_src[:, h:h + 1]
        kw_t = kw.T
        for g in range(n_seg):
            kg = kw_t if n_seg == 1 else jnp.where(seg_of_lane == g, kw_t, 0.0)
            c_out[g, h] = w_old[g, :, h:h + 1] * c_in[g, h] + _dot(kg.astype(BF16), vb)
        n_out[:, h:h + 1, :] = (w_old[:, :, h:h + 1] * n3
                                + jnp.sum(kw.reshape(n_seg, seg, A_DK), axis=1, keepdims=True))
    m_out[...] = m_end
    y_ref[...] = x_ref[...] + gate_ref[...] * out

    if carried:
        @pl.when(chunk == n_chunks - 1)
        def _():
            c1_ref[...] = c_st[...]
            n1_ref[...] = n_st[...]
            m1_ref[...] = m_st[...]


def _lane_pad(w, cols=LANES):
    return jnp.pad(w, ((0, 0), (0, cols - w.shape[1])))


def mlstm_mixer(grp, x, mod, state, w_in, conv_w, conv_b, gate_b, norm_g, w_out):
    (st_c, c_layer), st_n, st_m, st_conv = state
    n_main = 2 * A_QK + 2 * A_V
    wb = w_in.astype(BF16)
    p_main, gate_i, gate_f = norm_matmul(
        grp, x, mod, 1, 0,
        [wb[:, :n_main], _lane_pad(wb[:, n_main:n_main + A_HEADS]), _lane_pad(wb[:, n_main + A_HEADS:])])
    n_seg, seg, n_chunks, rows = grp.S, grp.L, grp.n_chunks, grp.R
    row_map = lambda b, c: (b * n_chunks + c, 0)
    scratch = [pltpu.VMEM((n_seg, SUBLANES + seg, 2 * A_QK), F32)]
    if n_chunks > 1:
        scratch += [pltpu.VMEM((n_seg, A_HEADS, A_DK, A_DV), F32), pltpu.VMEM((n_seg, A_HEADS, A_DK), F32),
                    pltpu.VMEM((n_seg, 1, LANES), F32)]
    full = lambda shape: pl.BlockSpec(shape, lambda b, c: (0,) * len(shape))
    x_new, c1, n1, m1 = pl.pallas_call(
        functools.partial(_mlstm_kernel, n_seg=n_seg, seg=seg, n_chunks=n_chunks),
        grid=(grp.n_blocks, n_chunks),
        in_specs=[
            pl.BlockSpec((rows, n_main), row_map),
            pl.BlockSpec((rows, LANES), row_map),
            pl.BlockSpec((rows, LANES), row_map),
            pl.BlockSpec((n_seg, SUBLANES, 2 * A_QK), lambda b, c: (b, 0, 0)),
            pl.BlockSpec((None, n_seg, A_HEADS, A_DK, A_DV), lambda b, c: (c_layer, b, 0, 0, 0)),
            pl.BlockSpec((n_seg, A_HEADS, A_DK), lambda b, c: (b, 0, 0)),
            pl.BlockSpec((n_seg, 1, LANES), lambda b, c: (b, 0, 0)),
            full((CONV_W, 2 * A_QK)), full((1, 2 * A_QK)), full((1, LANES)), full((1, LANES)),
            full((A_HEADS, A_DV)),
            pl.BlockSpec((rows, D_MODEL), row_map),
            grp.mod_spec(2, rows, lambda b, c: b * n_chunks + c),
            full((A_V, D_MODEL)),
        ],
        out_specs=[
            pl.BlockSpec((rows, D_MODEL), row_map),
            pl.BlockSpec((n_seg, A_HEADS, A_DK, A_DV), lambda b, c: (b, 0, 0, 0)),
            pl.BlockSpec((n_seg, A_HEADS, A_DK), lambda b, c: (b, 0, 0)),
            pl.BlockSpec((n_seg, 1, LANES), lambda b, c: (b, 0, 0)),
        ],
        out_shape=[
            jax.ShapeDtypeStruct((grp.N, D_MODEL), F32),
            jax.ShapeDtypeStruct(st_c.shape[1:], F32),
            jax.ShapeDtypeStruct(st_n.shape, F32),
            jax.ShapeDtypeStruct((grp.B, 1, LANES), F32),
        ],
        scratch_shapes=scratch,
        compiler_params=_cparams("arbitrary", "arbitrary"),
        name="mlstm_chunk",
    )(p_main, gate_i, gate_f, _pad_history(st_conv), st_c, st_n, _lane_pad(st_m)[:, None, :],
      conv_w, conv_b.reshape(1, -1), _lane_pad(gate_b[None, :A_HEADS]), _lane_pad(gate_b[None, A_HEADS:]), norm_g,
      x, mod, w_out.astype(BF16))
    conv_new = p_main.reshape(grp.B, grp.T, n_main)[:, grp.T - (CONV_W - 1):, :2 * A_QK]
    return x_new, (c1, n1, m1[:, 0, :A_HEADS], conv_new)


def _softplus(x):
    return jnp.maximum(x, 0.0) + jnp.log1p(jnp.exp(-jnp.abs(x)))


def _ssd_kernel(z_ref, x_ref, b_ref, c_ref, dt_ref, hx_ref, hb_ref, hc_ref, h0_ref,
                cwx_ref, cbx_ref, cwb_ref, cbb_ref, cwc_ref, cbc_ref, dtb_ref, alog_ref, dskip_ref, ng_ref,
                y_ref, h1_ref, xpx, xpb, xpc, *state_scr, n_seg, seg, n_chunks):
    rows = n_seg * seg
    chunk = pl.program_id(2)
    carried = n_chunks > 1

    def load_history():
        xpx[:, 0:SUBLANES, :] = hx_ref[...]
        xpb[:, 0:SUBLANES, :] = hb_ref[...]
        xpc[:, 0:SUBLANES, :] = hc_ref[...]

    if carried:
        (h_st,) = state_scr

        @pl.when(chunk == 0)
        def _():
            h_st[...] = h0_ref[...].reshape(n_seg, B_GW, B_STATE)
            load_history()

        get_h = lambda g: h_st[g]

        def set_h(g, val):
            h_st[g] = val
    else:
        load_history()
        get_h = lambda g: h0_ref[g].reshape(B_GW, B_STATE)

        def set_h(g, val):
            h1_ref[g] = val.reshape(B_HPG, B_HEADDIM, B_STATE)

    x = _causal_conv_silu(x_ref[...], xpx, cwx_ref, cbx_ref, n_seg, seg, carried)
    bm = _causal_conv_silu(b_ref[...], xpb, cwb_ref, cbb_ref, n_seg, seg, carried)
    cm = _causal_conv_silu(c_ref[...], xpc, cwc_ref, cbc_ref, n_seg, seg, carried)
    dt = _softplus(dt_ref[...] + dtb_ref[...])
    da = dt * (-jnp.exp(alog_ref[...]))
    a = _seg_cumsum(da, seg)
    w_end = jnp.exp(_seg_cumsum(da, seg, reverse=True) - da)
    a_t = a.T
    a_end = a.reshape(n_seg, seg, LANES)[:, seg - 1:seg, :]
    mask = _segment_mask(rows, seg)
    head_of_lane = lax.broadcasted_iota(I32, (rows, B_GW), 1) >> 6
    head_of_row = lax.broadcasted_iota(I32, (B_GW, B_STATE), 0) >> 6
    shift = seg.bit_length() - 1
    seg_of_lane = lax.broadcasted_iota(I32, (B_GW, rows), 1) >> shift

    def per_head(cols, selector):
        out = cols[B_HPG - 1]
        for e in range(B_HPG - 2, -1, -1):
            out = jnp.where(selector == e, cols[e], out)
        return out

    xdt = x * per_head([dt[:, e:e + 1] for e in range(B_HPG)], head_of_lane)
    xdt_b = xdt.astype(BF16)
    bm_b = bm.astype(BF16)
    cb = _dot_nt(cm.astype(BF16), bm_b)
    y = None
    for e in range(B_HPG):
        decay = jnp.exp(jnp.where(mask, a[:, e:e + 1] - a_t[e:e + 1, :], NEG_INF))
        ye = _dot((cb * decay).astype(BF16), xdt_b)
        y = ye if y is None else jnp.where(head_of_lane == e, ye, y)
    if n_seg == 1:
        y_state = _dot_nt(cm.astype(BF16), get_h(0).astype(BF16))
    else:
        y_state = jnp.concatenate([_dot_nt(cm[g * seg:(g + 1) * seg], get_h(g)) for g in range(n_seg)], axis=0)
    y = y + per_head([jnp.exp(a[:, e:e + 1]) for e in range(B_HPG)], head_of_lane) * y_state
    y = (y + dskip_ref[...] * x) * _silu(z_ref[...])
    y_ref[...] = y * lax.rsqrt(jnp.mean(y * y, axis=1, keepdims=True) + EPS) * ng_ref[...]
    xw_t = (xdt * per_head([w_end[:, e:e + 1] for e in range(B_HPG)], head_of_lane)).T
    for g in range(n_seg):
        xg = xw_t if n_seg == 1 else jnp.where(seg_of_lane == g, xw_t, 0.0)
        keep = per_head([jnp.exp(a_end[g][:, e:e + 1]) for e in range(B_HPG)], head_of_row)
        set_h(g, keep * get_h(g) + _dot(xg.astype(BF16), bm_b))

    if carried:
        @pl.when(chunk == n_chunks - 1)
        def _():
            h1_ref[...] = h_st[...].reshape(n_seg, B_HPG, B_HEADDIM, B_STATE)


def ssd_mixer(grp, x, mod, state, w_in, conv_w, conv_b, dt_bias, a_log, d_skip, norm_g, w_out):
    st_h, st_conv = state
    n_main = B_INNER + B_CONV_DIM
    wb = w_in.astype(BF16)
    group_lanes = lambda v: jnp.pad(v.reshape(-1, B_GROUPS, B_HPG), ((0, 0), (0, 0), (0, LANES - B_HPG))).reshape(
        -1, B_GROUPS * LANES)
    p_main, dt_raw = norm_matmul(grp, x, mod, 1, 0, [wb[:, :n_main], group_lanes(wb[:, n_main:])])
    n_seg, seg, n_chunks, rows = grp.S, grp.L, grp.n_chunks, grp.R
    xo, bo, co = B_INNER // B_GW, (2 * B_INNER) // B_STATE, (2 * B_INNER + B_GN) // B_STATE
    cxo, cbo, cco = 0, B_INNER // B_STATE, (B_INNER + B_GN) // B_STATE
    row = lambda b, g, c: b * n_chunks + c
    hist = _pad_history(st_conv)
    scratch = [pltpu.VMEM((n_seg, SUBLANES + seg, B_GW), F32), pltpu.VMEM((n_seg, SUBLANES + seg, B_STATE), F32),
               pltpu.VMEM((n_seg, SUBLANES + seg, B_STATE), F32)]
    if n_chunks > 1:
        scratch.append(pltpu.VMEM((n_seg, B_GW, B_STATE), F32))
    state_spec = pl.BlockSpec((n_seg, B_HPG, B_HEADDIM, B_STATE), lambda b, g, c: (b, g, 0, 0))
    y, h1 = pl.pallas_call(
        functools.partial(_ssd_kernel, n_seg=n_seg, seg=seg, n_chunks=n_chunks),
        grid=(grp.n_blocks, B_GROUPS, n_chunks),
        in_specs=[
            pl.BlockSpec((rows, B_GW), lambda b, g, c: (row(b, g, c), g)),
            pl.BlockSpec((rows, B_GW), lambda b, g, c: (row(b, g, c), xo + g)),
            pl.BlockSpec((rows, B_STATE), lambda b, g, c: (row(b, g, c), bo + g)),
            pl.BlockSpec((rows, B_STATE), lambda b, g, c: (row(b, g, c), co + g)),
            pl.BlockSpec((rows, LANES), lambda b, g, c: (row(b, g, c), g)),
            pl.BlockSpec((n_seg, SUBLANES, B_GW), lambda b, g, c: (b, 0, cxo + g)),
            pl.BlockSpec((n_seg, SUBLANES, B_STATE), lambda b, g, c: (b, 0, cbo + g)),
            pl.BlockSpec((n_seg, SUBLANES, B_STATE), lambda b, g, c: (b, 0, cco + g)),
            state_spec,
            pl.BlockSpec((CONV_W, B_GW), lambda b, g, c: (0, cxo + g)),
            pl.BlockSpec((1, B_GW), lambda b, g, c: (0, cxo + g)),
            pl.BlockSpec((CONV_W, B_STATE), lambda b, g, c: (0, cbo + g)),
            pl.BlockSpec((1, B_STATE), lambda b, g, c: (0, cbo + g)),
            pl.BlockSpec((CONV_W, B_STATE), lambda b, g, c: (0, cco + g)),
            pl.BlockSpec((1, B_STATE), lambda b, g, c: (0, cco + g)),
            pl.BlockSpec((1, LANES), lambda b, g, c: (0, g)),
            pl.BlockSpec((1, LANES), lambda b, g, c: (0, g)),
            pl.BlockSpec((1, B_GW), lambda b, g, c: (0, g)),
            pl.BlockSpec((1, B_GW), lambda b, g, c: (0, g)),
        ],
        out_specs=[pl.BlockSpec((rows, B_GW), lambda b, g, c: (row(b, g, c), g)), state_spec],
        out_shape=[jax.ShapeDtypeStruct((grp.N, B_INNER), F32), jax.ShapeDtypeStruct(st_h.shape, F32)],
        scratch_shapes=scratch,
        compiler_params=_cparams("arbitrary", "arbitrary", "arbitrary"),
        name="ssd_chunk",
    )(p_main, p_main, p_main, p_main, dt_raw, hist, hist, hist, st_h,
      conv_w, conv_b.reshape(1, -1), conv_w, conv_b.reshape(1, -1), conv_w, conv_b.reshape(1, -1),
      group_lanes(dt_bias[None, :]), group_lanes(a_log[None, :]),
      jnp.repeat(d_skip, B_HEADDIM)[None, :], norm_g[None, :])
    x_new = matmul_residual(grp, y, w_out.astype(BF16), x, mod, 2)
    conv_new = p_main.reshape(grp.B, grp.T, n_main)[:, grp.T - (CONV_W - 1):, B_INNER:]
    return x_new, (h1, conv_new)


C_Q = C_HEADS * C_HD
C_KV = C_KV_HEADS * C_HD


def _rope_kernel(p_ref, cos_ref, sin_ref, qg_ref, kg_ref, q_ref, k_ref):
    tm = p_ref.shape[0]
    lane = lax.broadcasted_iota(I32, (tm, LANES), 1)
    low_head = lane < C_HD
    first_half = (lane & (C_HD - 1)) < C_HD // 2
    cos, sin = cos_ref[...], sin_ref[...]

    def norm_rope(xb, gain):
        sq = xb * xb
        s_lo = jnp.sum(jnp.where(low_head, sq, 0.0), axis=1, keepdims=True)
        s_hi = jnp.sum(jnp.where(low_head, 0.0, sq), axis=1, keepdims=True)
        ms = jnp.where(low_head, s_lo, s_hi) * (1.0 / C_HD)
        xn = xb * lax.rsqrt(ms + EPS) * gain
        partner = jnp.where(first_half, pltpu.roll(xn, LANES - C_HD // 2, axis=1), pltpu.roll(xn, C_HD // 2, axis=1))
        return xn * cos + partner * sin

    for j in range(C_Q // LANES):
        q_ref[:, j * LANES:(j + 1) * LANES] = norm_rope(p_ref[:, j * LANES:(j + 1) * LANES], qg_ref[...])
    for j in range(C_KV // LANES):
        k_ref[:, j * LANES:(j + 1) * LANES] = norm_rope(p_ref[:, C_Q + j * LANES:C_Q + (j + 1) * LANES], kg_ref[...])


def _swa_kernel(q_ref, k0_ref, k1_ref, v0_ref, v1_ref, sink_ref, x_ref, gate_ref, wo_ref, o_ref, *,
                n_units, tq, blocks_per_seq):
    i = pl.program_id(0)
    rows = C_GROUP * tq
    t = lax.broadcasted_iota(I32, (rows, 2 * WINDOW), 0) & (tq - 1)
    s = lax.broadcasted_iota(I32, (rows, 2 * WINDOW), 1)
    valid = (s >= t) & (s <= t + WINDOW)
    if blocks_per_seq:
        valid = valid & (s >= jnp.where(lax.rem(i, blocks_per_seq) == 0, WINDOW, 0))

    def with_past(past, new):
        if tq < WINDOW:
            new = jnp.concatenate([new, jnp.zeros((WINDOW - tq, new.shape[1]), F32)], axis=0)
        return jnp.concatenate([past, new], axis=0)

    attn = []
    for u in range(n_units):
        kk = with_past(k0_ref[u], k1_ref[u])
        vv = with_past(v0_ref[u], v1_ref[u])
        qu = q_ref[u]
        outs = []
        for kh in range(C_KV_HEADS):
            kc = kk[:, kh * C_HD:(kh + 1) * C_HD].astype(BF16)
            vc = vv[:, kh * C_HD:(kh + 1) * C_HD].astype(BF16)
            qs = jnp.concatenate([qu[:, (kh * C_GROUP + j) * C_HD:(kh * C_GROUP + j + 1) * C_HD]
                                  for j in range(C_GROUP)], axis=0).astype(BF16)
            logits = jnp.where(valid, _dot_nt(qs, kc) * (C_HD ** -0.5), NEG_INF)
            sink = sink_ref[kh][:, 0:1]
            mx = jnp.maximum(jnp.max(logits, axis=1, keepdims=True), sink)
            pr = jnp.exp(logits - mx)
            den = jnp.sum(pr, axis=1, keepdims=True) + jnp.exp(sink - mx)
            o = _dot(pr.astype(BF16), vc) / den
            outs += [o[j * tq:(j + 1) * tq] for j in range(C_GROUP)]
        attn.append(jnp.concatenate(outs, axis=1))
    attn = attn[0] if n_units == 1 else jnp.concatenate(attn, axis=0)
    o_ref[...] = x_ref[...] + gate_ref[...] * _dot(attn.astype(BF16), wo_ref[...])


def swa_mixer(grp, x, mod, cache, pos0, w_in, q_g, k_g, sinks, w_out):
    (p,) = norm_matmul(grp, x, mod, 1, 0, [w_in.astype(BF16)])
    half = C_HD // 2
    inv = ROPE_THETA ** (-jnp.arange(half, dtype=F32) / half)
    ang = (pos0 + jnp.arange(grp.T)).astype(F32)[:, None] * inv[None, :]
    cos, sin = jnp.cos(ang), jnp.sin(ang)
    per_token = lambda a: jnp.tile(a, (grp.B, 1))
    cos_t = per_token(jnp.tile(cos, (1, LANES // half)))
    sin_t = per_token(jnp.tile(jnp.concatenate([-sin, sin], axis=1), (1, LANES // C_HD)))
    tm = grp.tm
    gains = lambda g: jnp.tile(g, LANES // C_HD)[None, :]
    qr, kr = pl.pallas_call(
        _rope_kernel,
        grid=(grp.N // tm,),
        in_specs=[
            pl.BlockSpec((tm, C_Q + 2 * C_KV), lambda i: (i, 0)),
            pl.BlockSpec((tm, LANES), lambda i: (i, 0)),
            pl.BlockSpec((tm, LANES), lambda i: (i, 0)),
            pl.BlockSpec((1, LANES), lambda i: (0, 0)),
            pl.BlockSpec((1, LANES), lambda i: (0, 0)),
        ],
        out_specs=[pl.BlockSpec((tm, C_Q), lambda i: (i, 0)), pl.BlockSpec((tm, C_KV), lambda i: (i, 0))],
        out_shape=[jax.ShapeDtypeStruct((grp.N, C_Q), F32), jax.ShapeDtypeStruct((grp.N, C_KV), F32)],
        compiler_params=_cparams("arbitrary"),
        name="qk_norm_rope",
    )(p, cos_t, sin_t, gains(q_g), gains(k_g))

    v_col = (C_Q + C_KV) // C_KV
    if cache is None:
        assert grp.T % WINDOW == 0
        tq, n_units, blocks_per_seq = WINDOW, 1, grp.T // WINDOW
        n_steps = grp.N // tq
        k3 = kr.reshape(n_steps, tq, C_KV)
        p3 = p.reshape(n_steps, tq, C_Q + 2 * C_KV)
        k_args = (k3, k3, p3, p3)
        prev = lambda i: (jnp.maximum(i - 1, 0), 0, 0)
        k_specs = [
            pl.BlockSpec((1, tq, C_KV), prev),
            pl.BlockSpec((1, tq, C_KV), lambda i: (i, 0, 0)),
            pl.BlockSpec((1, tq, C_KV), lambda i: (jnp.maximum(i - 1, 0), 0, v_col)),
            pl.BlockSpec((1, tq, C_KV), lambda i: (i, 0, v_col)),
        ]
    else:
        tq, n_units, blocks_per_seq = grp.T, SUBLANES, 0
        n_steps = grp.B // n_units
        k_args = (cache[0].reshape(grp.B, WINDOW, C_KV), kr.reshape(grp.B, tq, C_KV),
                  cache[1].reshape(grp.B, WINDOW, C_KV), p.reshape(grp.B, tq, C_Q + 2 * C_KV))
        k_specs = [
            pl.BlockSpec((n_units, WINDOW, C_KV), lambda i: (i, 0, 0)),
            pl.BlockSpec((n_units, tq, C_KV), lambda i: (i, 0, 0)),
            pl.BlockSpec((n_units, WINDOW, C_KV), lambda i: (i, 0, 0)),
            pl.BlockSpec((n_units, tq, C_KV), lambda i: (i, 0, v_col)),
        ]
    sink_rows = jnp.broadcast_to(sinks.reshape(C_KV_HEADS, C_GROUP, 1, 1),
                                 (C_KV_HEADS, C_GROUP, tq, LANES)).reshape(C_KV_HEADS, C_GROUP * tq, LANES)
    step_rows = n_units * tq
    x_new = pl.pallas_call(
        functools.partial(_swa_kernel, n_units=n_units, tq=tq, blocks_per_seq=blocks_per_seq),
        grid=(n_steps,),
        in_specs=[pl.BlockSpec((n_units, tq, C_Q), lambda i: (i, 0, 0))] + k_specs
        + [pl.BlockSpec(sink_rows.shape, lambda i: (0, 0, 0)),
           pl.BlockSpec((step_rows, D_MODEL), lambda i: (i, 0)),
           grp.mod_spec(2, step_rows, lambda i: i),
           pl.BlockSpec((C_Q, D_MODEL), lambda i: (0, 0))],
        out_specs=pl.BlockSpec((step_rows, D_MODEL), lambda i: (i, 0)),
        out_shape=jax.ShapeDtypeStruct((grp.N, D_MODEL), F32),
        compiler_params=_cparams("arbitrary"),
        name="swa_attention",
    )(qr.reshape(grp.N // tq, tq, C_Q), *k_args, sink_rows, x, mod, w_out.astype(BF16))
    k_new =kr.reshape(grp.B, grp.T, C_KV_HEADS, C_HD)
    v_new = p[:, C_Q + C_KV:].reshape(grp.B, grp.T, C_KV_HEADS, C_HD)
    if cache is None:
        return x_new, (k_new[:, grp.T - WINDOW:], v_new[:, grp.T - WINDOW:])
    return x_new, (jnp.concatenate([cache[0][:, grp.T:], k_new], axis=1),
                   jnp.concatenate([cache[1][:, grp.T:], v_new], axis=1))


def kernel(x_prompt, x_sample, c_prompt, c_sample, state_mlstm_C, state_mlstm_n, state_mlstm_m, state_mlstm_conv,
           state_ssm, state_ssm_conv, cache_swa_k, cache_swa_v, ada_w, ada_b, mlstm_w_in, mlstm_conv_w,
           mlstm_conv_b, mlstm_gate_b, mlstm_norm_g, mlstm_w_out, ssd_w_in, ssd_conv_w, ssd_conv_b, ssd_dt_bias,
           ssd_a_log, ssd_d_skip, ssd_norm_g, ssd_w_out, swa_w_in, swa_q_norm_g, swa_k_norm_g, swa_sinks,
           swa_w_out, moe_w_router, moe_b_router, moe_w1, moe_b1, moe_w2, moe_b2):
    groups = [Group(*x_prompt.shape[:2]), Group(*x_sample.shape[:2])]
    n_prompt = groups[0].B
    xs = [x_prompt.reshape(-1, D_MODEL), x_sample.reshape(-1, D_MODEL)]
    mod_all = ada_modulation(jnp.concatenate([c_prompt, c_sample], axis=0), ada_w, ada_b)
    fresh = lambda s: jnp.zeros((n_prompt,) + s.shape[2:], F32)
    new = [[[] for _ in range(8)] for _ in groups]
    for layer in range(DEPTH):
        kind, j = layer % 3, layer // 3
        mods = [groups[0].expand_mod(mod_all[layer, :n_prompt]), groups[1].expand_mod(mod_all[layer, n_prompt:])]
        for gi, grp in enumerate(groups):
            if kind == 0:
                state = (state_mlstm_n, state_mlstm_m, state_mlstm_conv)
                state = tuple(s[j] if gi else fresh(s) for s in state)
                state = ((state_mlstm_C, j) if gi else (fresh(state_mlstm_C)[None], 0),) + state
                xs[gi], st = mlstm_mixer(grp, xs[gi], mods[gi], state, mlstm_w_in[j], mlstm_conv_w[j], mlstm_conv_b[j],
                                         mlstm_gate_b[j], mlstm_norm_g[j], mlstm_w_out[j])
                first = 0
            elif kind == 1:
                state = tuple(s[j] if gi else fresh(s) for s in (state_ssm, state_ssm_conv))
                xs[gi], st = ssd_mixer(grp, xs[gi], mods[gi], state, ssd_w_in[j], ssd_conv_w[j], ssd_conv_b[j],
                                       ssd_dt_bias[j], ssd_a_log[j], ssd_d_skip[j], ssd_norm_g[j], ssd_w_out[j])
                first = 4
            else:
                cache = (cache_swa_k[j], cache_swa_v[j]) if gi else None
                xs[gi], st = swa_mixer(grp, xs[gi], mods[gi], cache, PAST_LEN if gi else 0, swa_w_in[j],
                                       swa_q_norm_g[j], swa_k_norm_g[j], swa_sinks[j], swa_w_out[j])
                first = 6
            for offset, s in enumerate(st):
                new[gi][first + offset].append(s)
        xs = moe_layer(groups, xs, mods, layer, moe_w_router, moe_b_router, moe_w1, moe_b1, moe_w2, moe_b2)
    outs = [xs[0].reshape(x_prompt.shape), xs[1].reshape(x_sample.shape)]
    for slot in range(8):
        outs += [jnp.stack(new[0][slot]), jnp.stack(new[1][slot])]
    return tuple(outs)
```

```python
import functools
import math

import jax
import jax.numpy as jnp
from jax import lax
from jax.experimental import pallas as pl
from jax.experimental.pallas import tpu as pltpu

F32 = jnp.float32
BF16 = jnp.bfloat16
I32 = jnp.int32

D_MODEL = 1024
DEPTH = 4
PAST_LEN = 8192
EPS = 1e-6
CONV_W = 4
A_HEADS = 4
A_DK = D_MODEL // 8
A_DV = D_MODEL // A_HEADS
A_QK = A_HEADS * A_DK
A_V = A_HEADS * A_DV
B_INNER = 2 * D_MODEL
B_HEADDIM = 64
B_HEADS = B_INNER // B_HEADDIM
B_STATE = 128
B_GROUPS = 8
B_GN = B_GROUPS * B_STATE
B_CONV_DIM = B_INNER + 2 * B_GN
B_HPG = B_HEADS // B_GROUPS
B_GW = B_HPG * B_HEADDIM
C_HEADS = 16
C_KV_HEADS = 4
C_GROUP = C_HEADS // C_KV_HEADS
C_HD = 64
WINDOW = 128
ROPE_THETA = 10000.0
N_EXPERTS = 32
TOP_K = 4
D_FF = D_MODEL
SWIGLU_LIMIT = 7.0
SWIGLU_ALPHA = 1.702

LANES = 128
SUBLANES = 8
VMEM_LIMIT_BYTES = 56 * 1024 * 1024

ROW_TILE = 256
SEQ_CHUNK = 256
SHORT_BLOCK = 128
MOE_TILE = 512
FF_CHUNK = 1024
NEG_INF = float("-inf")
ROW_AS_TILE = (SUBLANES, D_MODEL // SUBLANES)


def _cparams(*sem):
    return pltpu.CompilerParams(dimension_semantics=sem, vmem_limit_bytes=VMEM_LIMIT_BYTES)


def _silu(x):
    return x * jax.nn.sigmoid(x)


def _dot(a, b):
    return jnp.dot(a, b, preferred_element_type=F32)


def _dot_nt(a, b):
    return lax.dot_general(a, b, (((1,), (1,)), ((), ())), preferred_element_type=F32)


class Group:
    def __init__(self, B, T):
        self.B, self.T = B, T
        self.N = B * T
        self.long = T % SEQ_CHUNK == 0
        if self.long:
            self.S, self.L = 1, SEQ_CHUNK
        else:
            assert T == SUBLANES and self.N % SHORT_BLOCK == 0
            self.S, self.L = SHORT_BLOCK // T, T
        self.R = self.S * self.L
        self.n_blocks = B // self.S
        self.n_chunks = T // self.L
        self.tm = min(ROW_TILE, self.N)
        assert self.N % self.tm == 0 and (not self.long or T % self.tm == 0)

    def expand_mod(self, mod):
        if self.long:
            return mod.reshape(self.B, 1, 6 * D_MODEL)
        return jnp.repeat(mod, self.T, axis=0)

    def mod_spec(self, j, rows, row_block_fn):
        if self.long:
            per_seq = self.T // rows
            return pl.BlockSpec((None, 1, D_MODEL), lambda *g: (row_block_fn(*g) // per_seq, 0, j))
        return pl.BlockSpec((rows, D_MODEL), lambda *g: (row_block_fn(*g), j))


def _ada_kernel(c_ref, w_ref, b_ref, o_ref):
    cs = _silu(c_ref[...])
    o_ref[...] = _dot(cs.astype(BF16), w_ref[...].astype(BF16)) + b_ref[...]


def ada_modulation(c_all, ada_w, ada_b):
    rows = c_all.shape[0]
    tn = 1536
    n_out = 6 * D_MODEL
    return pl.pallas_call(
        _ada_kernel,
        grid=(DEPTH, n_out // tn),
        in_specs=[
            pl.BlockSpec((rows, D_MODEL), lambda l, j: (0, 0)),
            pl.BlockSpec((None, D_MODEL, tn), lambda l, j: (l, 0, j)),
            pl.BlockSpec((None, 1, tn), lambda l, j: (l, 0, j)),
        ],
        out_specs=pl.BlockSpec((None, rows, tn), lambda l, j: (l, 0, j)),
        out_shape=jax.ShapeDtypeStruct((DEPTH, rows, n_out), F32),
        compiler_params=_cparams("arbitrary", "arbitrary"),
        name="ada_modulation",
    )(c_all, ada_w, ada_b.reshape(DEPTH, 1, n_out))


def _modulated_norm(x, sc, sh):
    ms = jnp.mean(x * x, axis=-1, keepdims=True)
    return (x * lax.rsqrt(ms + EPS)) * (1.0 + sc) + sh


def _norm_mm_kernel(x_ref, sc_ref, sh_ref, *refs, n_w):
    h = _modulated_norm(x_ref[...], sc_ref[...], sh_ref[...]).astype(BF16)
    for w_ref, o_ref in zip(refs[:n_w], refs[n_w:]):
        o_ref[...] = _dot(h, w_ref[...])


def norm_matmul(grp, x, mod, j_scale, j_shift, weights):
    tm = grp.tm
    n_w = len(weights)
    in_specs = [
        pl.BlockSpec((tm, D_MODEL), lambda i: (i, 0)),
        grp.mod_spec(j_scale, tm, lambda i: i),
        grp.mod_spec(j_shift, tm, lambda i: i),
    ] + [pl.BlockSpec(w.shape, lambda i: (0, 0)) for w in weights]
    out_specs = [pl.BlockSpec((tm, w.shape[1]), lambda i: (i, 0)) for w in weights]
    out_shape = [jax.ShapeDtypeStruct((grp.N, w.shape[1]), F32) for w in weights]
    return pl.pallas_call(
        functools.partial(_norm_mm_kernel, n_w=n_w),
        grid=(grp.N // tm,),
        in_specs=in_specs,
        out_specs=out_specs,
        out_shape=out_shape,
        compiler_params=_cparams("arbitrary"),
        name="norm_matmul",
    )(x, mod, mod, *weights)


def _mm_res_kernel(y_ref, w_ref, x_ref, g_ref, o_ref):
    o_ref[...] = x_ref[...] + g_ref[...] * _dot(y_ref[...].astype(BF16), w_ref[...])


def matmul_residual(grp, y, w, x, mod, j_gate):
    tm = grp.tm
    k = y.shape[1]
    return pl.pallas_call(
        _mm_res_kernel,
        grid=(grp.N // tm,),
        in_specs=[
            pl.BlockSpec((tm, k), lambda i: (i, 0)),
            pl.BlockSpec(w.shape, lambda i: (0, 0)),
            pl.BlockSpec((tm, D_MODEL), lambda i: (i, 0)),
            grp.mod_spec(j_gate, tm, lambda i: i),
        ],
        out_specs=pl.BlockSpec((tm, D_MODEL), lambda i: (i, 0)),
        out_shape=jax.ShapeDtypeStruct((grp.N, D_MODEL), F32),
        compiler_params=_cparams("arbitrary"),
        name="matmul_residual",
    )(y, w, x, mod)


def _router_kernel(x_ref, sc_ref, sh_ref, wr_ref, br_ref, cnt_in_ref,
                   h_ref, ri_ref, rg_ref, cnt_out_ref, cnt_scr):
    i = pl.program_id(0)
    tm = x_ref.shape[0]

    @pl.when(i == 0)
    def _():
        cnt_scr[...] = cnt_in_ref[...]

    h = _modulated_norm(x_ref[...], sc_ref[...], sh_ref[...])
    h_ref[...] = h.reshape(tm, *ROW_AS_TILE)
    h_hi = h.astype(BF16)
    h_lo = (h - h_hi.astype(F32)).astype(BF16)
    hi_terms = _dot(h_hi, wr_ref[...])
    logits = (hi_terms[:, :LANES] + hi_terms[:, LANES:]) + _dot(h_lo, wr_ref[:, :LANES]) + br_ref[...]
    lane = lax.broadcasted_iota(I32, (tm, LANES), 1)
    lane_f = lane.astype(F32)
    work = jnp.where(lane < N_EXPERTS, logits, NEG_INF)
    top_v, top_sel, top_i = [], [], []
    for _ in range(TOP_K):
        mx = jnp.max(work, axis=1, keepdims=True)
        idx = jnp.min(jnp.where(work == mx, lane_f, float(LANES)), axis=1, keepdims=True)
        sel = lane_f == idx
        work = jnp.where(sel, NEG_INF, work)
        top_v.append(mx)
        top_sel.append(sel)
        top_i.append(idx)
    ex = [jnp.exp(v - top_v[0]) for v in top_v]
    inv = 1.0 / (ex[0] + ex[1] + ex[2] + ex[3])
    chosen = jnp.zeros((tm, LANES), F32)
    for sel in top_sel:
        chosen = jnp.where(sel, 1.0, chosen)
    r = lax.broadcasted_iota(I32, (tm, tm), 0)
    c = lax.broadcasted_iota(I32, (tm, tm), 1)
    before = jnp.where(c < r, 1.0, 0.0).astype(BF16)
    rank_all = _dot(before, chosen.astype(BF16)) + cnt_scr[...]
    ri = jnp.zeros((tm, LANES), I32)
    rg = jnp.zeros((tm, LANES), F32)
    for k in range(TOP_K):
        rank_k = jnp.sum(jnp.where(top_sel[k], rank_all, 0.0), axis=1, keepdims=True)
        ri = jnp.where(lane == k, top_i[k].astype(I32), ri)
        ri = jnp.where(lane == TOP_K + k, rank_k.astype(I32), ri)
        rg = jnp.where(lane == k, ex[k] * inv, rg)
    ri_ref[...] = ri
    rg_ref[...] = rg
    cnt_scr[...] = cnt_scr[...] + jnp.sum(chosen, axis=0, keepdims=True)
    cnt_out_ref[...] = cnt_scr[...]


def moe_router(grp, x, mod, w_r, b_r, cnt_in):
    tm = grp.tm
    return pl.pallas_call(
        _router_kernel,
        grid=(grp.N // tm,),
        in_specs=[
            pl.BlockSpec((tm, D_MODEL), lambda i: (i, 0)),
            grp.mod_spec(4, tm, lambda i: i),
            grp.mod_spec(3, tm, lambda i: i),
            pl.BlockSpec((D_MODEL, 2 * LANES), lambda i: (0, 0)),
            pl.BlockSpec((1, LANES), lambda i: (0, 0)),
            pl.BlockSpec((1, LANES), lambda i: (0, 0)),
        ],
        out_specs=[
            pl.BlockSpec((tm,) + ROW_AS_TILE, lambda i: (i, 0, 0)),
            pl.BlockSpec((tm, LANES), lambda i: (i, 0)),
            pl.BlockSpec((tm, LANES), lambda i: (i, 0)),
            pl.BlockSpec((1, LANES), lambda i: (0, 0)),
        ],
        out_shape=[
            jax.ShapeDtypeStruct((grp.N,) + ROW_AS_TILE, F32),
            jax.ShapeDtypeStruct((grp.N, LANES), I32),
            jax.ShapeDtypeStruct((grp.N, LANES), F32),
            jax.ShapeDtypeStruct((1, LANES), F32),
        ],
        scratch_shapes=[pltpu.VMEM((1, LANES), F32)],
        compiler_params=_cparams("arbitrary"),
        name="moe_router",
    )(x, mod, mod, w_r, b_r, cnt_in)


def _row_copy(src_ref, src_row, dst_ref, dst_row, sem):
    return pltpu.make_async_copy(src_ref.at[src_row], dst_ref.at[dst_row], sem)


DMA_UNROLL = 8


def _dispatch_kernel(pos_ref, cnt_ref, off_ref, nv_ref, *refs, tiles, n_sorted_tiles):
    n_groups = len(tiles)
    h_refs = refs[:n_groups]
    xs_ref, zero_scr, sem = refs[n_groups:]
    i = pl.program_id(0)
    tm = h_refs[0].shape[0]
    base = (i - 1) * (tm * TOP_K)

    def zero_tile(j):
        return pltpu.make_async_copy(zero_scr, xs_ref.at[pl.ds(j * MOE_TILE, MOE_TILE)], sem)

    def for_each_partial_tile(action):
        def body(e, carry):
            n = cnt_ref[e]

            @pl.when((n & (MOE_TILE - 1)) != 0)
            def _():
                action(zero_tile((off_ref[e] + n) // MOE_TILE))

            return carry

        lax.fori_loop(0, N_EXPERTS, body, 0)

        def tail(j, carry):
            action(zero_tile(j))
            return carry

        lax.fori_loop(nv_ref[0], n_sorted_tiles, tail, 0)

    @pl.when(i == 0)
    def _():
        zero_scr[...] = jnp.zeros(zero_scr.shape, F32)
        for_each_partial_tile(lambda cp: cp.start())
        for_each_partial_tile(lambda cp: cp.wait())

    def scatter_tile(h_ref):
        def start(t, carry):
            for k in range(TOP_K):
                _row_copy(h_ref, t, xs_ref, pos_ref[base + t * TOP_K + k], sem).start(priority=k % 2)
            return carry

        lax.fori_loop(0, tm, start, 0, unroll=DMA_UNROLL)
        for _ in range(TOP_K):
            pltpu.make_async_copy(h_ref, xs_ref.at[pl.ds(0, tm)], sem).wait()

    first = 1
    for h_ref, n in zip(h_refs, tiles):
        pl.when((i >= first) & (i < first + n))(functools.partial(scatter_tile, h_ref))
        first += n


def moe_dispatch(groups, hs, pos_all, counts, row_off, n_valid, n_sorted_tiles):
    tm = groups[0].tm
    assert all(g.tm == tm for g in groups)
    tiles = tuple(g.N // tm for g in groups)
    starts = [1 + sum(tiles[:gi]) for gi in range(len(tiles))]

    def h_spec(first, n):
        return pl.BlockSpec((tm,) + ROW_AS_TILE, lambda i, *_: (jnp.clip(i - first, 0, n - 1), 0, 0))

    return pl.pallas_call(
        functools.partial(_dispatch_kernel, tiles=tiles, n_sorted_tiles=n_sorted_tiles),
        grid_spec=pltpu.PrefetchScalarGridSpec(
            num_scalar_prefetch=4,
            grid=(sum(tiles) + 1,),
            in_specs=[h_spec(first, n) for first, n in zip(starts, tiles)],
            out_specs=pl.BlockSpec(memory_space=pl.ANY),
            scratch_shapes=[pltpu.VMEM((MOE_TILE,) + ROW_AS_TILE, F32), pltpu.SemaphoreType.DMA],
        ),
        out_shape=jax.ShapeDtypeStruct((n_sorted_tiles * MOE_TILE,) + ROW_AS_TILE, F32),
        compiler_params=_cparams("arbitrary"),
        name="moe_dispatch",
    )(pos_all, counts, row_off, n_valid, *hs)


def _combine_kernel(pos_ref, ys_ref, rg_ref, x_ref, g_ref, o_ref, ybuf, sems):
    i = pl.program_id(0)
    n = pl.num_programs(0)
    tm = x_ref.shape[0]

    def gather_tile(tile, slot):
        base = tile * (tm * TOP_K)

        def start(t, carry):
            for k in range(TOP_K):
                _row_copy(ys_ref, pos_ref[base + t * TOP_K + k], ybuf.at[slot, k], t,
                          sems.at[slot]).start(priority=k % 2)
            return carry

        lax.fori_loop(0, tm, start, 0, unroll=DMA_UNROLL)

    slot = lax.rem(i, 2)

    @pl.when(i == 0)
    def _():
        gather_tile(0, 0)

    @pl.when(i + 1 < n)
    def _():
        gather_tile(i + 1, 1 - slot)

    for k in range(TOP_K):
        pltpu.make_async_copy(ys_ref.at[pl.ds(0, tm)], ybuf.at[slot, k], sems.at[slot]).wait()
    rg = rg_ref[...]
    acc = rg[:, 0:1] * ybuf[slot, 0].reshape(tm, D_MODEL)
    for k in range(1, TOP_K):
        acc = acc + rg[:, k:k + 1] * ybuf[slot, k].reshape(tm, D_MODEL)
    o_ref[...] = x_ref[...] + g_ref[...] * acc


def moe_combine(grp, ys, pos, route_g, x, mod):
    tm = grp.tm
    return pl.pallas_call(
        _combine_kernel,
        grid_spec=pltpu.PrefetchScalarGridSpec(
            num_scalar_prefetch=1,
            grid=(grp.N // tm,),
            in_specs=[
                pl.BlockSpec(memory_space=pl.ANY),
                pl.BlockSpec((tm, LANES), lambda i, p: (i, 0)),
                pl.BlockSpec((tm, D_MODEL), lambda i, p: (i, 0)),
                grp.mod_spec(5, tm, lambda i, p: i),
            ],
            out_specs=pl.BlockSpec((tm, D_MODEL), lambda i, p: (i, 0)),
            scratch_shapes=[pltpu.VMEM((2, TOP_K, tm) + ROW_AS_TILE, F32), pltpu.SemaphoreType.DMA((2,))],
        ),
        out_shape=jax.ShapeDtypeStruct((grp.N, D_MODEL), F32),
        compiler_params=_cparams("arbitrary"),
        name="moe_combine",
    )(pos, ys, route_g, x, mod)


def _experts_kernel(te_ref, tb_ref, nv_ref, to_ref, seq_ref, nu_ref, x_ref, w1_ref, b1_ref, w2_ref, b2_ref, y_ref,
                    w1_scr, w2_scr, w1_stage, w2_stage, sems, *, layer):
    i = pl.program_id(0)

    def weight_copies(ordinal):
        e = seq_ref[ordinal]
        slot = lax.rem(ordinal, 2)
        return (pltpu.make_async_copy(w1_ref.at[layer, e], w1_stage.at[slot], sems.at[slot]),
                pltpu.make_async_copy(w2_ref.at[layer, e], w2_stage.at[slot], sems.at[slot]))

    @pl.when(i == 0)
    def _():
        for cp in weight_copies(0):
            cp.start()

    @pl.when(i < nv_ref[0])
    def _():
        ordinal = to_ref[i]

        @pl.when((i == 0) | (ordinal != to_ref[jnp.maximum(i - 1, 0)]))
        def _():
            @pl.when(ordinal + 1 < nu_ref[0])
            def _():
                for cp in weight_copies(ordinal + 1):
                    cp.start()

            for cp in weight_copies(ordinal):
                cp.wait()
            slot = lax.rem(ordinal, 2)
            w1_scr[...] = w1_stage[slot].astype(BF16)
            w2_scr[...] = w2_stage[slot].astype(BF16)

        xb = x_ref[...].reshape(MOE_TILE, D_MODEL).astype(BF16)
        y = b2_ref[...]
        for j in range(D_FF // FF_CHUNK):
            cols = slice(j * FF_CHUNK, (j + 1) * FF_CHUNK)
            ucols = slice(D_FF + j * FF_CHUNK, D_FF + (j + 1) * FF_CHUNK)
            g = jnp.minimum(_dot(xb, w1_scr[:, cols]) + b1_ref[:, cols], SWIGLU_LIMIT)
            u = jnp.clip(_dot(xb, w1_scr[:, ucols]) + b1_ref[:, ucols], -SWIGLU_LIMIT, SWIGLU_LIMIT)
            act = (u + 1.0) * (g * jax.nn.sigmoid(SWIGLU_ALPHA * g))
            y = y + _dot(act.astype(BF16), w2_scr[cols, :])
        y_ref[...] = y.reshape(y_ref.shape)

    @pl.when(i >= nv_ref[0])
    def _():
        y_ref[...] = jnp.zeros(y_ref.shape, F32)


def moe_experts(xs, tile_expert, tile_block, n_valid, tiles_per, w1, b1, w2, b2, layer):
    n_tiles = tile_expert.shape[0]
    tm = MOE_TILE
    used = tiles_per > 0
    ordinal_of = jnp.cumsum(used.astype(I32)) - 1
    experts = jnp.arange(N_EXPERTS, dtype=I32)
    seq = jnp.sum(jnp.where(used[None, :] & (ordinal_of[None, :] == experts[:, None]), experts[None, :], 0), axis=1)
    tile_ordinal = jnp.sum(jnp.where(tile_expert[:, None] == experts[None, :], ordinal_of[None, :], 0), axis=1)
    n_used = jnp.sum(used.astype(I32)).reshape(1)
    bias_map = lambda i, te, *_: (layer, te[i], 0, 0)
    return pl.pallas_call(
        functools.partial(_experts_kernel, layer=layer),
        grid_spec=pltpu.PrefetchScalarGridSpec(
            num_scalar_prefetch=6,
            grid=(n_tiles,),
            in_specs=[
                pl.BlockSpec((tm,) + ROW_AS_TILE, lambda i, te, tb, *_: (tb[i], 0, 0)),
                pl.BlockSpec(memory_space=pl.ANY),
                pl.BlockSpec((None, None, 1, 2 * D_FF), bias_map),
                pl.BlockSpec(memory_space=pl.ANY),
                pl.BlockSpec((None, None, 1, D_MODEL), bias_map),
            ],
            out_specs=pl.BlockSpec((tm,) + ROW_AS_TILE, lambda i, *_: (i, 0, 0)),
            scratch_shapes=[pltpu.VMEM((D_MODEL, 2 * D_FF), BF16), pltpu.VMEM((D_FF, D_MODEL), BF16),
                            pltpu.VMEM((2, D_MODEL, 2 * D_FF), F32), pltpu.VMEM((2, D_FF, D_MODEL), F32),
                            pltpu.SemaphoreType.DMA((2,))],
        ),
        out_shape=jax.ShapeDtypeStruct(xs.shape, F32),
        compiler_params=_cparams("arbitrary"),
        name="moe_experts",
    )(tile_expert, tile_block, n_valid, tile_ordinal, seq, n_used, xs, w1,
      b1.reshape(DEPTH, N_EXPERTS, 1, 2 * D_FF), w2, b2.reshape(DEPTH, N_EXPERTS, 1, D_MODEL))


def moe_layer(groups, xs_in, mods, layer, w_router, b_router, w1, b1, w2, b2):
    n_total = sum(g.N for g in groups)
    n_pairs = n_total * TOP_K
    n_tiles = -(-n_pairs // MOE_TILE) + N_EXPERTS
    w_r = jnp.pad(w_router[layer], ((0, 0), (0, LANES - N_EXPERTS)))
    w_hi = w_r.astype(BF16)
    w_r = jnp.concatenate([w_hi, (w_r - w_hi.astype(F32)).astype(BF16)], axis=1)
    b_r =jnp.pad(b_router[layer], (0, LANES - N_EXPERTS)).reshape(1, LANES)
    cnt = jnp.zeros((1, LANES), F32)
    routed = []
    for grp, x, mod in zip(groups, xs_in, mods):
        h, route_i, route_g, cnt = moe_router(grp, x, mod, w_r, b_r, cnt)
        routed.append((h, route_i, route_g))
    counts = cnt[0, :N_EXPERTS].astype(I32)
    tiles_per = (counts + MOE_TILE - 1) // MOE_TILE
    tile_end = jnp.cumsum(tiles_per)
    tile_start = tile_end - tiles_per
    n_valid = tile_end[-1:]
    tidx = jnp.minimum(jnp.arange(n_tiles, dtype=I32), n_valid - 1)
    tile_expert = jnp.sum((tile_end[None, :] <= tidx[:, None]).astype(I32), axis=1)
    row_off = tile_start * MOE_TILE
    positions = []
    for _, route_i, _ in routed:
        route_e, rank = route_i[:, :TOP_K], route_i[:, TOP_K:2 * TOP_K]
        onehot = route_e[:, :, None] == jnp.arange(N_EXPERTS, dtype=I32)[None, None, :]
        positions.append((jnp.sum(jnp.where(onehot, row_off[None, None, :], 0), axis=2) + rank).reshape(-1))
    xs = moe_dispatch(groups, [h for h, _, _ in routed], jnp.concatenate(positions), counts, row_off, n_valid,
                      n_tiles)
    ys = moe_experts(xs, tile_expert, tidx, n_valid, tiles_per, w1, b1, w2, b2, layer)
    return [moe_combine(grp, ys, pos, route_g, x, mod)
            for grp, x, mod, pos, (_, _, route_g) in zip(groups, xs_in, mods, positions, routed)]


def _seg_cumsum(x, seg, reverse=False):
    rows = x.shape[0]
    row = lax.broadcasted_iota(I32, x.shape, 0) & (seg - 1)
    sh = 1
    while sh < seg:
        if reverse:
            x = x + jnp.where(row + sh < seg, pltpu.roll(x, rows - sh, axis=0), 0.0)
        else:
            x = x + jnp.where(row >= sh, pltpu.roll(x, sh, axis=0), 0.0)
        sh *= 2
    return x


def _segment_mask(rows, seg):
    ti = lax.broadcasted_iota(I32, (rows, rows), 0)
    si = lax.broadcasted_iota(I32, (rows, rows), 1)
    mask = si <= ti
    if seg < rows:
        shift = seg.bit_length() - 1
        mask = mask & ((ti >> shift) == (si >> shift))
    return mask


def _causal_conv_silu(x, xp_scr, w_ref, b_ref, n_seg, seg, carry):
    rows, ch = x.shape
    xp_scr[:, SUBLANES:, :] = x.reshape(n_seg, seg, ch)
    acc = b_ref[...]
    for i in range(CONV_W):
        back = CONV_W - 1 - i
        xs = x if back == 0 else xp_scr[:, SUBLANES - back:SUBLANES - back + seg, :].reshape(rows, ch)
        acc = acc + xs * w_ref[i:i + 1, :]
    if carry:
        xp_scr[:, SUBLANES - (CONV_W - 1):SUBLANES, :] = xp_scr[:, SUBLANES + seg - (CONV_W - 1):SUBLANES + seg, :]
    return _silu(acc)


def _pad_history(buf):
    return jnp.pad(buf, ((0, 0), (SUBLANES - (CONV_W - 1), 0), (0, 0)))


def _mlstm_kernel(p_ref, gi_ref, gf_ref, hist_ref, c0_ref, n0_ref, m0_ref, cw_ref, cb_ref, gbi_ref, gbf_ref,
                  ng_ref, x_ref, gate_ref, wo_ref, y_ref, c1_ref, n1_ref, m1_ref, xp_scr, *state_scr,
                  n_seg, seg, n_chunks):
    rows = n_seg * seg
    chunk = pl.program_id(1)
    carried = n_chunks > 1
    if carried:
        c_st, n_st, m_st = state_scr

        @pl.when(chunk == 0)
        def _():
            c_st[...] = c0_ref[...]
            n_st[...] = n0_ref[...]
            m_st[...] = m0_ref[...]
            xp_scr[:, 0:SUBLANES, :] = hist_ref[...]

        c_in, n_in, m_in, c_out, n_out, m_out = c_st, n_st, m_st, c_st, n_st, m_st
    else:
        xp_scr[:, 0:SUBLANES, :] = hist_ref[...]
        c_in, n_in, m_in, c_out, n_out, m_out = c0_ref, n0_ref, m0_ref, c1_ref, n1_ref, m1_ref

    qk = _causal_conv_silu(p_ref[:, :2 * A_QK], xp_scr, cw_ref, cb_ref, n_seg, seg, carried)
    log_i = gi_ref[...] + gbi_ref[...]
    f_pre = gf_ref[...] + gbf_ref[...]
    log_f = jnp.minimum(f_pre, 0.0) - jnp.log1p(jnp.exp(-jnp.abs(f_pre)))
    a = _seg_cumsum(log_f, seg)
    a_rev = _seg_cumsum(log_f, seg, reverse=True)
    m0 = m_in[...]
    inter3 = a.reshape(n_seg, seg, LANES) + m0
    inter = inter3.reshape(rows, LANES)
    src3 = (a_rev - log_f + log_i).reshape(n_seg, seg, LANES)
    inter_end = inter3[:, seg - 1:seg, :]
    m_end = jnp.maximum(inter_end, jnp.max(src3, axis=1, keepdims=True))
    w_old = jnp.exp(inter_end - m_end)
    w_src = jnp.exp(src3 - m_end).reshape(rows, LANES)
    b_t = (a - log_i).T
    mask = _segment_mask(rows, seg)
    seg_of_lane = lax.broadcasted_iota(I32, (A_DK, rows), 1) >> (seg.bit_length() - 1)

    out = None
    for h in range(A_HEADS):
        q = qk[:, h * A_DK:(h + 1) * A_DK] * (A_DK ** -0.5)
        k = qk[:, A_QK + h * A_DK:A_QK + (h + 1) * A_DK]
        vb = p_ref[:, 2 * A_QK + h * A_DV:2 * A_QK + (h + 1) * A_DV].astype(BF16)
        qb = q.astype(BF16)
        dmat = jnp.where(mask, a[:, h:h + 1] - b_t[h:h + 1, :], NEG_INF)
        m_tok = jnp.maximum(inter[:, h:h + 1], jnp.max(dmat, axis=1, keepdims=True))
        s = _dot_nt(qb, k.astype(BF16)) * jnp.exp(dmat - m_tok)
        w_inter = jnp.exp(inter[:, h:h + 1] - m_tok)
        if n_seg == 1:
            q_c = _dot(qb, c_in[0, h].astype(BF16))
        else:
            q_c = jnp.concatenate([_dot(q[g * seg:(g + 1) * seg], c_in[g, h]) for g in range(n_seg)], axis=0)
        n3 = n_in[:, h:h + 1, :]
        n_tok = jnp.broadcast_to(n3, (n_seg, seg, A_DK)).reshape(rows, A_DK)
        num = _dot(s.astype(BF16), vb) + w_inter * q_c
        den = jnp.sum(s, axis=1, keepdims=True) + w_inter * jnp.sum(q * n_tok, axis=1, keepdims=True)
        hv = num / jnp.maximum(jnp.abs(den), jnp.exp(-m_tok))
        hn = hv * lax.rsqrt(jnp.mean(hv * hv, axis=1, keepdims=True) + EPS) * ng_ref[h:h + 1, :]
        o = jax.nn.sigmoid(p_ref[:, 2 * A_QK + A_V + h * A_DV:2 * A_QK + A_V + (h + 1) * A_DV])
        head_out = _dot((o * hn).astype(BF16), wo_ref[h * A_DV:(h + 1) * A_DV, :])
        out = head_out if out is None else out + head_out
        kw = k * w_src[:, h:h + 1]
        kw_t = kw.T
        for g in range(n_seg):
            kg = kw_t if n_seg == 1 else jnp.where(seg_of_lane == g, kw_t, 0.0)
            c_out[g, h] = w_old[g, :, h:h + 1] * c_in[g, h] + _dot(kg.astype(BF16), vb)
        n_out[:, h:h + 1, :] = (w_old[:, :, h:h + 1] * n3
                                + jnp.sum(kw.reshape(n_seg, seg, A_DK), axis=1, keepdims=True))
    m_out[...] = m_end
    y_ref[...] = x_ref[...] + gate_ref[...] * out

    if carried:
        @pl.when(chunk == n_chunks - 1)
        def _():
            c1_ref[...] = c_st[...]
            n1_ref[...] = n_st[...]
            m1_ref[...] = m_st[...]


def _lane_pad(w, cols=LANES):
    return jnp.pad(w, ((0, 0), (0, cols - w.shape[1])))


def mlstm_mixer(grp, x, mod, state, w_in, conv_w, conv_b, gate_b, norm_g, w_out):
    (st_c, c_layer), st_n, st_m, st_conv = state
    n_main = 2 * A_QK + 2 * A_V
    wb = w_in.astype(BF16)
    p_main, gate_i, gate_f = norm_matmul(
        grp, x, mod, 1, 0,
        [wb[:, :n_main], _lane_pad(wb[:, n_main:n_main + A_HEADS]), _lane_pad(wb[:, n_main + A_HEADS:])])
    n_seg, seg, n_chunks, rows = grp.S, grp.L, grp.n_chunks, grp.R
    row_map = lambda b, c: (b * n_chunks + c, 0)
    scratch = [pltpu.VMEM((n_seg, SUBLANES + seg, 2 * A_QK), F32)]
    if n_chunks > 1:
        scratch += [pltpu.VMEM((n_seg, A_HEADS, A_DK, A_DV), F32), pltpu.VMEM((n_seg, A_HEADS, A_DK), F32),
                    pltpu.VMEM((n_seg, 1, LANES), F32)]
    full = lambda shape: pl.BlockSpec(shape, lambda b, c: (0,) * len(shape))
    x_new, c1, n1, m1 = pl.pallas_call(
        functools.partial(_mlstm_kernel, n_seg=n_seg, seg=seg, n_chunks=n_chunks),
        grid=(grp.n_blocks, n_chunks),
        in_specs=[
            pl.BlockSpec((rows, n_main), row_map),
            pl.BlockSpec((rows, LANES), row_map),
            pl.BlockSpec((rows, LANES), row_map),
            pl.BlockSpec((n_seg, SUBLANES, 2 * A_QK), lambda b, c: (b, 0, 0)),
            pl.BlockSpec((None, n_seg, A_HEADS, A_DK, A_DV), lambda b, c: (c_layer, b, 0, 0, 0)),
            pl.BlockSpec((n_seg, A_HEADS, A_DK), lambda b, c: (b, 0, 0)),
            pl.BlockSpec((n_seg, 1, LANES), lambda b, c: (b, 0, 0)),
            full((CONV_W, 2 * A_QK)), full((1, 2 * A_QK)), full((1, LANES)), full((1, LANES)),
            full((A_HEADS, A_DV)),
            pl.BlockSpec((rows, D_MODEL), row_map),
            grp.mod_spec(2, rows, lambda b, c: b * n_chunks + c),
            full((A_V, D_MODEL)),
        ],
        out_specs=[
            pl.BlockSpec((rows, D_MODEL), row_map),
            pl.BlockSpec((n_seg, A_HEADS, A_DK, A_DV), lambda b, c: (b, 0, 0, 0)),
            pl.BlockSpec((n_seg, A_HEADS, A_DK), lambda b, c: (b, 0, 0)),
            pl.BlockSpec((n_seg, 1, LANES), lambda b, c: (b, 0, 0)),
        ],
        out_shape=[
            jax.ShapeDtypeStruct((grp.N, D_MODEL), F32),
            jax.ShapeDtypeStruct(st_c.shape[1:], F32),
            jax.ShapeDtypeStruct(st_n.shape, F32),
            jax.ShapeDtypeStruct((grp.B, 1, LANES), F32),
        ],
        scratch_shapes=scratch,
        compiler_params=_cparams("arbitrary", "arbitrary"),
        name="mlstm_chunk",
    )(p_main, gate_i, gate_f, _pad_history(st_conv), st_c, st_n, _lane_pad(st_m)[:, None, :],
      conv_w, conv_b.reshape(1, -1), _lane_pad(gate_b[None, :A_HEADS]), _lane_pad(gate_b[None, A_HEADS:]), norm_g,
      x, mod, w_out.astype(BF16))
    conv_new = p_main.reshape(grp.B, grp.T, n_main)[:, grp.T - (CONV_W - 1):, :2 * A_QK]
    return x_new, (c1, n1, m1[:, 0, :A_HEADS], conv_new)


def _softplus(x):
    return jnp.maximum(x, 0.0) + jnp.log1p(jnp.exp(-jnp.abs(x)))


def _ssd_kernel(z_ref, x_ref, b_ref, c_ref, dt_ref, hx_ref, hb_ref, hc_ref, h0_ref,
                cwx_ref, cbx_ref, cwb_ref, cbb_ref, cwc_ref, cbc_ref, dtb_ref, alog_ref, dskip_ref, ng_ref,
                y_ref, h1_ref, xpx, xpb, xpc, *state_scr, n_seg, seg, n_chunks):
    rows = n_seg * seg
    chunk = pl.program_id(2)
    carried = n_chunks > 1

    def load_history():
        xpx[:, 0:SUBLANES, :] = hx_ref[...]
        xpb[:, 0:SUBLANES, :] = hb_ref[...]
        xpc[:, 0:SUBLANES, :] = hc_ref[...]

    if carried:
        (h_st,) = state_scr

        @pl.when(chunk == 0)
        def _():
            h_st[...] = h0_ref[...].reshape(n_seg, B_GW, B_STATE)
            load_history()

        get_h = lambda g: h_st[g]

        def set_h(g, val):
            h_st[g] = val
    else:
        load_history()
        get_h = lambda g: h0_ref[g].reshape(B_GW, B_STATE)

        def set_h(g, val):
            h1_ref[g] = val.reshape(B_HPG, B_HEADDIM, B_STATE)

    x = _causal_conv_silu(x_ref[...], xpx, cwx_ref, cbx_ref, n_seg, seg, carried)
    bm = _causal_conv_silu(b_ref[...], xpb, cwb_ref, cbb_ref, n_seg, seg, carried)
    cm = _causal_conv_silu(c_ref[...], xpc, cwc_ref, cbc_ref, n_seg, seg, carried)
    dt = _softplus(dt_ref[...] + dtb_ref[...])
    da = dt * (-jnp.exp(alog_ref[...]))
    a = _seg_cumsum(da, seg)
    w_end = jnp.exp(_seg_cumsum(da, seg, reverse=True) - da)
    a_t = a.T
    a_end = a.reshape(n_seg, seg, LANES)[:, seg - 1:seg, :]
    mask = _segment_mask(rows, seg)
    head_of_lane = lax.broadcasted_iota(I32, (rows, B_GW), 1) >> 6
    head_of_row = lax.broadcasted_iota(I32, (B_GW, B_STATE), 0) >> 6
    shift = seg.bit_length() - 1
    seg_of_lane = lax.broadcasted_iota(I32, (B_GW, rows), 1) >> shift

    def per_head(cols, selector):
        out = cols[B_HPG - 1]
        for e in range(B_HPG - 2, -1, -1):
            out = jnp.where(selector == e, cols[e], out)
        return out

    xdt = x * per_head([dt[:, e:e + 1] for e in range(B_HPG)], head_of_lane)
    xdt_b = xdt.astype(BF16)
    bm_b = bm.astype(BF16)
    cb = _dot_nt(cm.astype(BF16), bm_b)
    y = None
    for e in range(B_HPG):
        decay = jnp.exp(jnp.where(mask, a[:, e:e + 1] - a_t[e:e + 1, :], NEG_INF))
        ye = _dot((cb * decay).astype(BF16), xdt_b)
        y = ye if y is None else jnp.where(head_of_lane == e, ye, y)
    if n_seg == 1:
        y_state = _dot_nt(cm.astype(BF16), get_h(0).astype(BF16))
    else:
        y_state = jnp.concatenate([_dot_nt(cm[g * seg:(g + 1) * seg], get_h(g)) for g in range(n_seg)], axis=0)
    y = y + per_head([jnp.exp(a[:, e:e + 1]) for e in range(B_HPG)], head_of_lane) * y_state
    y = (y + dskip_ref[...] * x) * _silu(z_ref[...])
    y_ref[...] = y * lax.rsqrt(jnp.mean(y * y, axis=1, keepdims=True) + EPS) * ng_ref[...]
    xw_t = (xdt * per_head([w_end[:, e:e + 1] for e in range(B_HPG)], head_of_lane)).T
    for g in range(n_seg):
        xg = xw_t if n_seg == 1 else jnp.where(seg_of_lane == g, xw_t, 0.0)
        keep = per_head([jnp.exp(a_end[g][:, e:e + 1]) for e in range(B_HPG)], head_of_row)
        set_h(g, keep * get_h(g) + _dot(xg.astype(BF16), bm_b))

    if carried:
        @pl.when(chunk == n_chunks - 1)
        def _():
            h1_ref[...] = h_st[...].reshape(n_seg, B_HPG, B_HEADDIM, B_STATE)


def ssd_mixer(grp, x, mod, state, w_in, conv_w, conv_b, dt_bias, a_log, d_skip, norm_g, w_out):
    st_h, st_conv = state
    n_main = B_INNER + B_CONV_DIM
    wb = w_in.astype(BF16)
    group_lanes = lambda v: jnp.pad(v.reshape(-1, B_GROUPS, B_HPG), ((0, 0), (0, 0), (0, LANES - B_HPG))).reshape(
        -1, B_GROUPS * LANES)
    p_main, dt_raw = norm_matmul(grp, x, mod, 1, 0, [wb[:, :n_main], group_lanes(wb[:, n_main:])])
    n_seg, seg, n_chunks, rows = grp.S, grp.L, grp.n_chunks, grp.R
    xo, bo, co = B_INNER // B_GW, (2 * B_INNER) // B_STATE, (2 * B_INNER + B_GN) // B_STATE
    cxo, cbo, cco = 0, B_INNER // B_STATE, (B_INNER + B_GN) // B_STATE
    row = lambda b, g, c: b * n_chunks + c
    hist = _pad_history(st_conv)
    scratch = [pltpu.VMEM((n_seg, SUBLANES + seg, B_GW), F32), pltpu.VMEM((n_seg, SUBLANES + seg, B_STATE), F32),
               pltpu.VMEM((n_seg, SUBLANES + seg, B_STATE), F32)]
    if n_chunks > 1:
        scratch.append(pltpu.VMEM((n_seg, B_GW, B_STATE), F32))
    state_spec = pl.BlockSpec((n_seg, B_HPG, B_HEADDIM, B_STATE), lambda b, g, c: (b, g, 0, 0))
    y, h1 = pl.pallas_call(
        functools.partial(_ssd_kernel, n_seg=n_seg, seg=seg, n_chunks=n_chunks),
        grid=(grp.n_blocks, B_GROUPS, n_chunks),
        in_specs=[
            pl.BlockSpec((rows, B_GW), lambda b, g, c: (row(b, g, c), g)),
            pl.BlockSpec((rows, B_GW), lambda b, g, c: (row(b, g, c), xo + g)),
            pl.BlockSpec((rows, B_STATE), lambda b, g, c: (row(b, g, c), bo + g)),
            pl.BlockSpec((rows, B_STATE), lambda b, g, c: (row(b, g, c), co + g)),
            pl.BlockSpec((rows, LANES), lambda b, g, c: (row(b, g, c), g)),
            pl.BlockSpec((n_seg, SUBLANES, B_GW), lambda b, g, c: (b, 0, cxo + g)),
            pl.BlockSpec((n_seg, SUBLANES, B_STATE), lambda b, g, c: (b, 0, cbo + g)),
            pl.BlockSpec((n_seg, SUBLANES, B_STATE), lambda b, g, c: (b, 0, cco + g)),
            state_spec,
            pl.BlockSpec((CONV_W, B_GW), lambda b, g, c: (0, cxo + g)),
            pl.BlockSpec((1, B_GW), lambda b, g, c: (0, cxo + g)),
            pl.BlockSpec((CONV_W, B_STATE), lambda b, g, c: (0, cbo + g)),
            pl.BlockSpec((1, B_STATE), lambda b, g, c: (0, cbo + g)),
            pl.BlockSpec((CONV_W, B_STATE), lambda b, g, c: (0, cco + g)),
            pl.BlockSpec((1, B_STATE), lambda b, g, c: (0, cco + g)),
            pl.BlockSpec((1, LANES), lambda b, g, c: (0, g)),
            pl.BlockSpec((1, LANES), lambda b, g, c: (0, g)),
            pl.BlockSpec((1, B_GW), lambda b, g, c: (0, g)),
            pl.BlockSpec((1, B_GW), lambda b, g, c: (0, g)),
        ],
        out_specs=[pl.BlockSpec((rows, B_GW), lambda b, g, c: (row(b, g, c), g)), state_spec],
        out_shape=[jax.ShapeDtypeStruct((grp.N, B_INNER), F32), jax.ShapeDtypeStruct(st_h.shape, F32)],
        scratch_shapes=scratch,
        compiler_params=_cparams("arbitrary", "arbitrary", "arbitrary"),
        name="ssd_chunk",
    )(p_main, p_main, p_main, p_main, dt_raw, hist, hist, hist, st_h,
      conv_w, conv_b.reshape(1, -1), conv_w, conv_b.reshape(1, -1), conv_w, conv_b.reshape(1, -1),
      group_lanes(dt_bias[None, :]), group_lanes(a_log[None, :]),
      jnp.repeat(d_skip, B_HEADDIM)[None, :], norm_g[None, :])
    x_new = matmul_residual(grp, y, w_out.astype(BF16), x, mod, 2)
    conv_new = p_main.reshape(grp.B, grp.T, n_main)[:, grp.T - (CONV_W - 1):, B_INNER:]
    return x_new, (h1, conv_new)


C_Q = C_HEADS * C_HD
C_KV = C_KV_HEADS * C_HD


def _rope_kernel(p_ref, cos_ref, sin_ref, qg_ref, kg_ref, q_ref, k_ref):
    tm = p_ref.shape[0]
    lane = lax.broadcasted_iota(I32, (tm, LANES), 1)
    low_head = lane < C_HD
    first_half = (lane & (C_HD - 1)) < C_HD // 2
    cos, sin = cos_ref[...], sin_ref[...]

    def norm_rope(xb, gain):
        sq = xb * xb
        s_lo = jnp.sum(jnp.where(low_head, sq, 0.0), axis=1, keepdims=True)
        s_hi = jnp.sum(jnp.where(low_head, 0.0, sq), axis=1, keepdims=True)
        ms = jnp.where(low_head, s_lo, s_hi) * (1.0 / C_HD)
        xn = xb * lax.rsqrt(ms + EPS) * gain
        partner = jnp.where(first_half, pltpu.roll(xn, LANES - C_HD // 2, axis=1), pltpu.roll(xn, C_HD // 2, axis=1))
        return xn * cos + partner * sin

    for j in range(C_Q // LANES):
        q_ref[:, j * LANES:(j + 1) * LANES] = norm_rope(p_ref[:, j * LANES:(j + 1) * LANES], qg_ref[...])
    for j in range(C_KV // LANES):
        k_ref[:, j * LANES:(j + 1) * LANES] = norm_rope(p_ref[:, C_Q + j * LANES:C_Q + (j + 1) * LANES], kg_ref[...])


def _swa_kernel(q_ref, k0_ref, k1_ref, v0_ref, v1_ref, sink_ref, o_ref, *, n_units, tq, blocks_per_seq):
    i = pl.program_id(0)
    rows = C_GROUP * tq
    t = lax.broadcasted_iota(I32, (rows, 2 * WINDOW), 0) & (tq - 1)
    s = lax.broadcasted_iota(I32, (rows, 2 * WINDOW), 1)
    valid = (s >= t) & (s <= t + WINDOW)
    if blocks_per_seq:
        valid = valid & (s >= jnp.where(lax.rem(i, blocks_per_seq) == 0, WINDOW, 0))

    def with_past(past, new):
        if tq < WINDOW:
            new = jnp.concatenate([new, jnp.zeros((WINDOW - tq, new.shape[1]), F32)], axis=0)
        return jnp.concatenate([past, new], axis=0)

    for u in range(n_units):
        kk = with_past(k0_ref[u], k1_ref[u])
        vv = with_past(v0_ref[u], v1_ref[u])
        qu = q_ref[u]
        outs = []
        for kh in range(C_KV_HEADS):
            kc = kk[:, kh * C_HD:(kh + 1) * C_HD].astype(BF16)
            vc = vv[:, kh * C_HD:(kh + 1) * C_HD].astype(BF16)
            qs = jnp.concatenate([qu[:, (kh * C_GROUP + j) * C_HD:(kh * C_GROUP + j + 1) * C_HD]
                                  for j in range(C_GROUP)], axis=0).astype(BF16)
            logits = jnp.where(valid, _dot_nt(qs, kc) * (C_HD ** -0.5), NEG_INF)
            sink = sink_ref[kh][:, 0:1]
            mx = jnp.maximum(jnp.max(logits, axis=1, keepdims=True), sink)
            pr = jnp.exp(logits - mx)
            den = jnp.sum(pr, axis=1, keepdims=True) + jnp.exp(sink - mx)
            o = _dot(pr.astype(BF16), vc) / den
            outs += [o[j * tq:(j + 1) * tq] for j in range(C_GROUP)]
        o_ref[u] = jnp.concatenate(outs, axis=1)


def swa_mixer(grp, x, mod, cache, pos0, w_in, q_g, k_g, sinks, w_out):
    (p,) = norm_matmul(grp, x, mod, 1, 0, [w_in.astype(BF16)])
    half = C_HD // 2
    inv = ROPE_THETA ** (-jnp.arange(half, dtype=F32) / half)
    ang = (pos0 + jnp.arange(grp.T)).astype(F32)[:, None] * inv[None, :]
    cos, sin = jnp.cos(ang), jnp.sin(ang)
    per_token = lambda a: jnp.tile(a, (grp.B, 1))
    cos_t = per_token(jnp.tile(cos, (1, LANES // half)))
    sin_t = per_token(jnp.tile(jnp.concatenate([-sin, sin], axis=1), (1, LANES // C_HD)))
    tm = grp.tm
    gains = lambda g: jnp.tile(g, LANES // C_HD)[None, :]
    qr, kr = pl.pallas_call(
        _rope_kernel,
        grid=(grp.N // tm,),
        in_specs=[
            pl.BlockSpec((tm, C_Q + 2 * C_KV), lambda i: (i, 0)),
            pl.BlockSpec((tm, LANES), lambda i: (i, 0)),
            pl.BlockSpec((tm, LANES), lambda i: (i, 0)),
            pl.BlockSpec((1, LANES), lambda i: (0, 0)),
            pl.BlockSpec((1, LANES), lambda i: (0, 0)),
        ],
        out_specs=[pl.BlockSpec((tm, C_Q), lambda i: (i, 0)), pl.BlockSpec((tm, C_KV), lambda i: (i, 0))],
        out_shape=[jax.ShapeDtypeStruct((grp.N, C_Q), F32), jax.ShapeDtypeStruct((grp.N, C_KV), F32)],
        compiler_params=_cparams("arbitrary"),
        name="qk_norm_rope",
    )(p, cos_t, sin_t, gains(q_g), gains(k_g))

    v_col = (C_Q + C_KV) // C_KV
    if cache is None:
        assert grp.T % WINDOW == 0
        tq, n_units, blocks_per_seq = WINDOW, 1, grp.T // WINDOW
        n_steps = grp.N // tq
        k3 = kr.reshape(n_steps, tq, C_KV)
        p3 = p.reshape(n_steps, tq, C_Q + 2 * C_KV)
        k_args = (k3, k3, p3, p3)
        prev = lambda i: (jnp.maximum(i - 1, 0), 0, 0)
        k_specs = [
            pl.BlockSpec((1, tq, C_KV), prev),
            pl.BlockSpec((1, tq, C_KV), lambda i: (i, 0, 0)),
            pl.BlockSpec((1, tq, C_KV), lambda i: (jnp.maximum(i - 1, 0), 0, v_col)),
            pl.BlockSpec((1, tq, C_KV), lambda i: (i, 0, v_col)),
        ]
    else:
        tq, n_units, blocks_per_seq = grp.T, SUBLANES, 0
        n_steps = grp.B // n_units
        k_args = (cache[0].reshape(grp.B, WINDOW, C_KV), kr.reshape(grp.B, tq, C_KV),
                  cache[1].reshape(grp.B, WINDOW, C_KV), p.reshape(grp.B, tq, C_Q + 2 * C_KV))
        k_specs = [
            pl.BlockSpec((n_units, WINDOW, C_KV), lambda i: (i, 0, 0)),
            pl.BlockSpec((n_units, tq, C_KV), lambda i: (i, 0, 0)),
            pl.BlockSpec((n_units, WINDOW, C_KV), lambda i: (i, 0, 0)),
            pl.BlockSpec((n_units, tq, C_KV), lambda i: (i, 0, v_col)),
        ]
    sink_rows = jnp.broadcast_to(sinks.reshape(C_KV_HEADS, C_GROUP, 1, 1),
                                 (C_KV_HEADS, C_GROUP, tq, LANES)).reshape(C_KV_HEADS, C_GROUP * tq, LANES)
    o = pl.pallas_call(
        functools.partial(_swa_kernel, n_units=n_units, tq=tq, blocks_per_seq=blocks_per_seq),
        grid=(n_steps,),
        in_specs=[pl.BlockSpec((n_units, tq, C_Q), lambda i: (i, 0, 0))] + k_specs
        + [pl.BlockSpec(sink_rows.shape, lambda i: (0, 0, 0))],
        out_specs=pl.BlockSpec((n_units, tq, C_Q), lambda i: (i, 0, 0)),
        out_shape=jax.ShapeDtypeStruct((grp.N // tq, tq, C_Q), F32),
        compiler_params=_cparams("arbitrary"),
        name="swa_attention",
    )(qr.reshape(grp.N // tq, tq, C_Q), *k_args, sink_rows)
    x_new = matmul_residual(grp, o.reshape(grp.N, C_Q), w_out.astype(BF16), x, mod, 2)
    k_new = kr.reshape(grp.B, grp.T, C_KV_HEADS, C_HD)
    v_new = p[:, C_Q + C_KV:].reshape(grp.B, grp.T, C_KV_HEADS, C_HD)
    if cache is None:
        return x_new, (k_new[:, grp.T - WINDOW:], v_new[:, grp.T - WINDOW:])
    return x_new, (jnp.concatenate([cache[0][:, grp.T:], k_new], axis=1),
                   jnp.concatenate([cache[1][:, grp.T:], v_new], axis=1))


def kernel(x_prompt, x_sample, c_prompt, c_sample, state_mlstm_C, state_mlstm_n, state_mlstm_m, state_mlstm_conv,
           state_ssm, state_ssm_conv, cache_swa_k, cache_swa_v, ada_w, ada_b, mlstm_w_in, mlstm_conv_w,
           mlstm_conv_b, mlstm_gate_b, mlstm_norm_g, mlstm_w_out, ssd_w_in, ssd_conv_w, ssd_conv_b, ssd_dt_bias,
           ssd_a_log, ssd_d_skip, ssd_norm_g, ssd_w_out, swa_w_in, swa_q_norm_g, swa_k_norm_g, swa_sinks,
           swa_w_out, moe_w_router, moe_b_router, moe_w1, moe_b1, moe_w2, moe_b2):
    groups = [Group(*x_prompt.shape[:2]), Group(*x_sample.shape[:2])]
    n_prompt = groups[0].B
    xs = [x_prompt.reshape(-1, D_MODEL), x_sample.reshape(-1, D_MODEL)]
    mod_all = ada_modulation(jnp.concatenate([c_prompt, c_sample], axis=0), ada_w, ada_b)
    fresh = lambda s: jnp.zeros((n_prompt,) + s.shape[2:], F32)
    new = [[[] for _ in range(8)] for _ in groups]
    for layer in range(DEPTH):
        kind, j = layer % 3, layer // 3
        mods = [groups[0].expand_mod(mod_all[layer, :n_prompt]), groups[1].expand_mod(mod_all[layer, n_prompt:])]
        for gi, grp in enumerate(groups):
            if kind == 0:
                state = (state_mlstm_n, state_mlstm_m, state_mlstm_conv)
                state = tuple(s[j] if gi else fresh(s) for s in state)
                state = ((state_mlstm_C, j) if gi else (fresh(state_mlstm_C)[None], 0),) + state
                xs[gi], st = mlstm_mixer(grp, xs[gi], mods[gi], state, mlstm_w_in[j], mlstm_conv_w[j], mlstm_conv_b[j],
                                         mlstm_gate_b[j], mlstm_norm_g[j], mlstm_w_out[j])
                first = 0
            elif kind == 1:
                state = tuple(s[j] if gi else fresh(s) for s in (state_ssm, state_ssm_conv))
                xs[gi], st = ssd_mixer(grp, xs[gi], mods[gi], state, ssd_w_in[j], ssd_conv_w[j], ssd_conv_b[j],
                                       ssd_dt_bias[j], ssd_a_log[j], ssd_d_skip[j], ssd_norm_g[j], ssd_w_out[j])
                first = 4
            else:
                cache = (cache_swa_k[j], cache_swa_v[j]) if gi else None
                xs[gi], st = swa_mixer(grp, xs[gi], mods[gi], cache, PAST_LEN if gi else 0, swa_w_in[j],
                                       swa_q_norm_g[j], swa_k_norm_g[j], swa_sinks[j], swa_w_out[j])
                first = 6
            for offset, s in enumerate(st):
                new[gi][first + offset].append(s)
        xs = moe_layer(groups, xs, mods, layer, moe_w_router, moe_b_router, moe_w1, moe_b1, moe_w2, moe_b2)
    outs = [xs[0].reshape(x_prompt.shape), xs[1].reshape(x_sample.shape)]
    for slot in range(8):
        outs += [jnp.stack(new[0][slot]), jnp.stack(new[1][slot])]
    return tuple(outs)
```

```python
import functools
import math

import jax
import jax.numpy as jnp
from jax import lax
from jax.experimental import pallas as pl
from jax.experimental.pallas import tpu as pltpu

F32 = jnp.float32
BF16 = jnp.bfloat16
I32 = jnp.int32

D_MODEL = 1024
DEPTH = 4
PAST_LEN = 8192
EPS = 1e-6
CONV_W = 4
A_HEADS = 4
A_DK = D_MODEL // 8
A_DV = D_MODEL // A_HEADS
A_QK = A_HEADS * A_DK
A_V = A_HEADS * A_DV
B_INNER = 2 * D_MODEL
B_HEADDIM = 64
B_HEADS = B_INNER // B_HEADDIM
B_STATE = 128
B_GROUPS = 8
B_GN = B_GROUPS * B_STATE
B_CONV_DIM = B_INNER + 2 * B_GN
B_HPG = B_HEADS // B_GROUPS
B_GW = B_HPG * B_HEADDIM
C_HEADS = 16
C_KV_HEADS = 4
C_GROUP = C_HEADS // C_KV_HEADS
C_HD = 64
WINDOW = 128
ROPE_THETA = 10000.0
N_EXPERTS = 32
TOP_K = 4
D_FF = D_MODEL
SWIGLU_LIMIT = 7.0
SWIGLU_ALPHA = 1.702

LANES = 128
SUBLANES = 8
VMEM_LIMIT_BYTES = 56 * 1024 * 1024

ROW_TILE = 512
NORM_MATMUL_VMEM = 40 * 1024 * 1024
SEQ_CHUNK = 256
SHORT_BLOCK = 128
MOE_TILE = 512
FF_CHUNK = 1024
NEG_INF = float("-inf")
ROW_AS_TILE = (SUBLANES, D_MODEL // SUBLANES)


def _cparams(*sem):
    return pltpu.CompilerParams(dimension_semantics=sem, vmem_limit_bytes=VMEM_LIMIT_BYTES)


def _silu(x):
    return x * jax.nn.sigmoid(x)


def _dot(a, b):
    return jnp.dot(a, b, preferred_element_type=F32)


def _dot_nt(a, b):
    return lax.dot_general(a, b, (((1,), (1,)), ((), ())), preferred_element_type=F32)


class Group:
    def __init__(self, B, T):
        self.B, self.T = B, T
        self.N = B * T
        self.long = T % SEQ_CHUNK == 0
        if self.long:
            self.S, self.L = 1, SEQ_CHUNK
        else:
            assert T == SUBLANES and self.N % SHORT_BLOCK == 0
            self.S, self.L = SHORT_BLOCK // T, T
        self.R = self.S * self.L
        self.n_blocks = B // self.S
        self.n_chunks = T // self.L
        self.tm = min(ROW_TILE, self.N)
        assert self.N % self.tm == 0 and (not self.long or T % self.tm == 0)

    def expand_mod(self, mod):
        if self.long:
            return mod.reshape(self.B, 1, 6 * D_MODEL)
        return jnp.repeat(mod, self.T, axis=0)

    def mod_spec(self, j, rows, row_block_fn):
        if self.long:
            per_seq = self.T // rows
            return pl.BlockSpec((None, 1, D_MODEL), lambda *g: (row_block_fn(*g) // per_seq, 0, j))
        return pl.BlockSpec((rows, D_MODEL), lambda *g: (row_block_fn(*g), j))


def _ada_kernel(c_ref, w_ref, b_ref, o_ref):
    cs = _silu(c_ref[...])
    o_ref[...] = _dot(cs.astype(BF16), w_ref[...].astype(BF16)) + b_ref[...]


def ada_modulation(c_all, ada_w, ada_b):
    rows = c_all.shape[0]
    tn = 1536
    n_out = 6 * D_MODEL
    return pl.pallas_call(
        _ada_kernel,
        grid=(DEPTH, n_out // tn),
        in_specs=[
            pl.BlockSpec((rows, D_MODEL), lambda l, j: (0, 0)),
            pl.BlockSpec((None, D_MODEL, tn), lambda l, j: (l, 0, j)),
            pl.BlockSpec((None, 1, tn), lambda l, j: (l, 0, j)),
        ],
        out_specs=pl.BlockSpec((None, rows, tn), lambda l, j: (l, 0, j)),
        out_shape=jax.ShapeDtypeStruct((DEPTH, rows, n_out), F32),
        compiler_params=_cparams("arbitrary", "arbitrary"),
        name="ada_modulation",
    )(c_all, ada_w, ada_b.reshape(DEPTH, 1, n_out))


def _modulated_norm(x, sc, sh):
    ms = jnp.mean(x * x, axis=-1, keepdims=True)
    return (x * lax.rsqrt(ms + EPS)) * (1.0 + sc) + sh


def _norm_mm_kernel(x_ref, sc_ref, sh_ref, *refs, n_w):
    h = _modulated_norm(x_ref[...], sc_ref[...], sh_ref[...]).astype(BF16)
    for w_ref, o_ref in zip(refs[:n_w], refs[n_w:]):
        o_ref[...] = _dot(h, w_ref[...])


def norm_matmul(grp, x, mod, j_scale, j_shift, weights):
    tm = grp.tm
    cols = sum(w.shape[1] for w in weights)
    while 2 * (tm * cols * 4 + D_MODEL * cols * 2) > NORM_MATMUL_VMEM:
        tm //= 2
    n_w = len(weights)
    in_specs = [
        pl.BlockSpec((tm, D_MODEL), lambda i: (i, 0)),
        grp.mod_spec(j_scale, tm, lambda i: i),
        grp.mod_spec(j_shift, tm, lambda i: i),
    ] + [pl.BlockSpec(w.shape, lambda i: (0, 0)) for w in weights]
    out_specs = [pl.BlockSpec((tm, w.shape[1]), lambda i: (i, 0)) for w in weights]
    out_shape = [jax.ShapeDtypeStruct((grp.N, w.shape[1]), F32) for w in weights]
    return pl.pallas_call(
        functools.partial(_norm_mm_kernel, n_w=n_w),
        grid=(grp.N // tm,),
        in_specs=in_specs,
        out_specs=out_specs,
        out_shape=out_shape,
        compiler_params=_cparams("arbitrary"),
        name="norm_matmul",
    )(x, mod, mod, *weights)


def _mm_res_kernel(y_ref, w_ref, x_ref, g_ref, o_ref):
    o_ref[...] = x_ref[...] + g_ref[...] * _dot(y_ref[...].astype(BF16), w_ref[...])


def matmul_residual(grp, y, w, x, mod, j_gate):
    tm = grp.tm
    k = y.shape[1]
    return pl.pallas_call(
        _mm_res_kernel,
        grid=(grp.N // tm,),
        in_specs=[
            pl.BlockSpec((tm, k), lambda i: (i, 0)),
            pl.BlockSpec(w.shape, lambda i: (0, 0)),
            pl.BlockSpec((tm, D_MODEL), lambda i: (i, 0)),
            grp.mod_spec(j_gate, tm, lambda i: i),
        ],
        out_specs=pl.BlockSpec((tm, D_MODEL), lambda i: (i, 0)),
        out_shape=jax.ShapeDtypeStruct((grp.N, D_MODEL), F32),
        compiler_params=_cparams("arbitrary"),
        name="matmul_residual",
    )(y, w, x, mod)


def _router_kernel(x_ref, sc_ref, sh_ref, wr_ref, br_ref, cnt_in_ref,
                   h_ref, ri_ref, rg_ref, cnt_out_ref, cnt_scr):
    i = pl.program_id(0)
    tm = x_ref.shape[0]

    @pl.when(i == 0)
    def _():
        cnt_scr[...] = cnt_in_ref[...]

    h = _modulated_norm(x_ref[...], sc_ref[...], sh_ref[...])
    h_ref[...] = h.reshape(tm, *ROW_AS_TILE)
    h_hi = h.astype(BF16)
    h_lo = (h - h_hi.astype(F32)).astype(BF16)
    hi_terms = _dot(h_hi, wr_ref[...])
    logits = (hi_terms[:, :LANES] + hi_terms[:, LANES:]) + _dot(h_lo, wr_ref[:, :LANES]) + br_ref[...]
    lane = lax.broadcasted_iota(I32, (tm, LANES), 1)
    lane_f = lane.astype(F32)
    work = jnp.where(lane < N_EXPERTS, logits, NEG_INF)
    top_v, top_sel, top_i = [], [], []
    for _ in range(TOP_K):
        mx = jnp.max(work, axis=1, keepdims=True)
        idx = jnp.min(jnp.where(work == mx, lane_f, float(LANES)), axis=1, keepdims=True)
        sel = lane_f == idx
        work = jnp.where(sel, NEG_INF, work)
        top_v.append(mx)
        top_sel.append(sel)
        top_i.append(idx)
    ex = [jnp.exp(v - top_v[0]) for v in top_v]
    inv = 1.0 / (ex[0] + ex[1] + ex[2] + ex[3])
    chosen = jnp.zeros((tm, LANES), F32)
    for sel in top_sel:
        chosen = jnp.where(sel, 1.0, chosen)
    r = lax.broadcasted_iota(I32, (tm, tm), 0)
    c = lax.broadcasted_iota(I32, (tm, tm), 1)
    before = jnp.where(c < r, 1.0, 0.0).astype(BF16)
    rank_all = _dot(before, chosen.astype(BF16)) + cnt_scr[...]
    ri = jnp.zeros((tm, LANES), I32)
    rg = jnp.zeros((tm, LANES), F32)
    for k in range(TOP_K):
        rank_k = jnp.sum(jnp.where(top_sel[k], rank_all, 0.0), axis=1, keepdims=True)
        ri = jnp.where(lane == k, top_i[k].astype(I32), ri)
        ri = jnp.where(lane == TOP_K + k, rank_k.astype(I32), ri)
        rg = jnp.where(lane == k, ex[k] * inv, rg)
    ri_ref[...] = ri
    rg_ref[...] = rg
    cnt_scr[...] = cnt_scr[...] + jnp.sum(chosen, axis=0, keepdims=True)
    cnt_out_ref[...] = cnt_scr[...]


def moe_router(grp, x, mod, w_r, b_r, cnt_in):
    tm = grp.tm
    return pl.pallas_call(
        _router_kernel,
        grid=(grp.N // tm,),
        in_specs=[
            pl.BlockSpec((tm, D_MODEL), lambda i: (i, 0)),
            grp.mod_spec(4, tm, lambda i: i),
            grp.mod_spec(3, tm, lambda i: i),
            pl.BlockSpec((D_MODEL, 2 * LANES), lambda i: (0, 0)),
            pl.BlockSpec((1, LANES), lambda i: (0, 0)),
            pl.BlockSpec((1, LANES), lambda i: (0, 0)),
        ],
        out_specs=[
            pl.BlockSpec((tm,) + ROW_AS_TILE, lambda i: (i, 0, 0)),
            pl.BlockSpec((tm, LANES), lambda i: (i, 0)),
            pl.BlockSpec((tm, LANES), lambda i: (i, 0)),
            pl.BlockSpec((1, LANES), lambda i: (0, 0)),
        ],
        out_shape=[
            jax.ShapeDtypeStruct((grp.N,) + ROW_AS_TILE, F32),
            jax.ShapeDtypeStruct((grp.N, LANES), I32),
            jax.ShapeDtypeStruct((grp.N, LANES), F32),
            jax.ShapeDtypeStruct((1, LANES), F32),
        ],
        scratch_shapes=[pltpu.VMEM((1, LANES), F32)],
        compiler_params=_cparams("arbitrary"),
        name="moe_router",
    )(x, mod, mod, w_r, b_r, cnt_in)


def _row_copy(src_ref, src_row, dst_ref, dst_row, sem):
    return pltpu.make_async_copy(src_ref.at[src_row], dst_ref.at[dst_row], sem)


DMA_UNROLL = 8


def _dispatch_kernel(pos_ref, cnt_ref, off_ref, nv_ref, *refs, tiles, n_sorted_tiles):
    n_groups = len(tiles)
    h_refs = refs[:n_groups]
    xs_ref, zero_scr, sem = refs[n_groups:]
    i = pl.program_id(0)
    tm = h_refs[0].shape[0]
    base = (i - 1) * (tm * TOP_K)

    def zero_tile(j):
        return pltpu.make_async_copy(zero_scr, xs_ref.at[pl.ds(j * MOE_TILE, MOE_TILE)], sem)

    def for_each_partial_tile(action):
        def body(e, carry):
            n = cnt_ref[e]

            @pl.when((n & (MOE_TILE - 1)) != 0)
            def _():
                action(zero_tile((off_ref[e] + n) // MOE_TILE))

            return carry

        lax.fori_loop(0, N_EXPERTS, body, 0)

        def tail(j, carry):
            action(zero_tile(j))
            return carry

        lax.fori_loop(nv_ref[0], n_sorted_tiles, tail, 0)

    @pl.when(i == 0)
    def _():
        zero_scr[...] = jnp.zeros(zero_scr.shape, F32)
        for_each_partial_tile(lambda cp: cp.start())
        for_each_partial_tile(lambda cp: cp.wait())

    def scatter_tile(h_ref):
        def start(t, carry):
            for k in range(TOP_K):
                _row_copy(h_ref, t, xs_ref, pos_ref[base + t * TOP_K + k], sem).start(priority=k % 2)
            return carry

        lax.fori_loop(0, tm, start, 0, unroll=DMA_UNROLL)
        for _ in range(TOP_K):
            pltpu.make_async_copy(h_ref, xs_ref.at[pl.ds(0, tm)], sem).wait()

    first = 1
    for h_ref, n in zip(h_refs, tiles):
        pl.when((i >= first) & (i < first + n))(functools.partial(scatter_tile, h_ref))
        first += n


def moe_dispatch(groups, hs, pos_all, counts, row_off, n_valid, n_sorted_tiles):
    tm = groups[0].tm
    assert all(g.tm == tm for g in groups)
    tiles = tuple(g.N // tm for g in groups)
    starts = [1 + sum(tiles[:gi]) for gi in range(len(tiles))]

    def h_spec(first, n):
        return pl.BlockSpec((tm,) + ROW_AS_TILE, lambda i, *_: (jnp.clip(i - first, 0, n - 1), 0, 0))

    return pl.pallas_call(
        functools.partial(_dispatch_kernel, tiles=tiles, n_sorted_tiles=n_sorted_tiles),
        grid_spec=pltpu.PrefetchScalarGridSpec(
            num_scalar_prefetch=4,
            grid=(sum(tiles) + 1,),
            in_specs=[h_spec(first, n) for first, n in zip(starts, tiles)],
            out_specs=pl.BlockSpec(memory_space=pl.ANY),
            scratch_shapes=[pltpu.VMEM((MOE_TILE,) + ROW_AS_TILE, F32), pltpu.SemaphoreType.DMA],
        ),
        out_shape=jax.ShapeDtypeStruct((n_sorted_tiles * MOE_TILE,) + ROW_AS_TILE, F32),
        compiler_params=_cparams("arbitrary"),
        name="moe_dispatch",
    )(pos_all, counts, row_off, n_valid, *hs)


def _combine_kernel(pos_ref, ys_ref, rg_ref, x_ref, g_ref, o_ref, ybuf, sems):
    i = pl.program_id(0)
    n = pl.num_programs(0)
    tm = x_ref.shape[0]

    def gather_tile(tile, slot):
        base = tile * (tm * TOP_K)

        def start(t, carry):
            for k in range(TOP_K):
                _row_copy(ys_ref, pos_ref[base + t * TOP_K + k], ybuf.at[slot, k], t,
                          sems.at[slot]).start(priority=k % 2)
            return carry

        lax.fori_loop(0, tm, start, 0, unroll=DMA_UNROLL)

    slot = lax.rem(i, 2)

    @pl.when(i == 0)
    def _():
        gather_tile(0, 0)

    @pl.when(i + 1 < n)
    def _():
        gather_tile(i + 1, 1 - slot)

    for k in range(TOP_K):
        pltpu.make_async_copy(ys_ref.at[pl.ds(0, tm)], ybuf.at[slot, k], sems.at[slot]).wait()
    rg = rg_ref[...]
    acc = rg[:, 0:1] * ybuf[slot, 0].reshape(tm, D_MODEL)
    for k in range(1, TOP_K):
        acc = acc + rg[:, k:k + 1] * ybuf[slot, k].reshape(tm, D_MODEL)
    o_ref[...] = x_ref[...] + g_ref[...] * acc


def moe_combine(grp, ys, pos, route_g, x, mod):
    tm = grp.tm
    return pl.pallas_call(
        _combine_kernel,
        grid_spec=pltpu.PrefetchScalarGridSpec(
            num_scalar_prefetch=1,
            grid=(grp.N // tm,),
            in_specs=[
                pl.BlockSpec(memory_space=pl.ANY),
                pl.BlockSpec((tm, LANES), lambda i, p: (i, 0)),
                pl.BlockSpec((tm, D_MODEL), lambda i, p: (i, 0)),
                grp.mod_spec(5, tm, lambda i, p: i),
            ],
            out_specs=pl.BlockSpec((tm, D_MODEL), lambda i, p: (i, 0)),
            scratch_shapes=[pltpu.VMEM((2, TOP_K, tm) + ROW_AS_TILE, F32), pltpu.SemaphoreType.DMA((2,))],
        ),
        out_shape=jax.ShapeDtypeStruct((grp.N, D_MODEL), F32),
        compiler_params=_cparams("arbitrary"),
        name="moe_combine",
    )(pos, ys, route_g, x, mod)


def _experts_kernel(te_ref, tb_ref, nv_ref, to_ref, seq_ref, nu_ref, x_ref, w1_ref, b1_ref, w2_ref, b2_ref, y_ref,
                    w1_scr, w2_scr, w1_stage, w2_stage, sems, *, layer):
    i = pl.program_id(0)

    def weight_copies(ordinal):
        e = seq_ref[ordinal]
        slot = lax.rem(ordinal, 2)
        return (pltpu.make_async_copy(w1_ref.at[layer, e], w1_stage.at[slot], sems.at[slot]),
                pltpu.make_async_copy(w2_ref.at[layer, e], w2_stage.at[slot], sems.at[slot]))

    @pl.when(i == 0)
    def _():
        for cp in weight_copies(0):
            cp.start()

    @pl.when(i < nv_ref[0])
    def _():
        ordinal = to_ref[i]

        @pl.when((i == 0) | (ordinal != to_ref[jnp.maximum(i - 1, 0)]))
        def _():
            @pl.when(ordinal + 1 < nu_ref[0])
            def _():
                for cp in weight_copies(ordinal + 1):
                    cp.start()

            for cp in weight_copies(ordinal):
                cp.wait()
            slot = lax.rem(ordinal, 2)
            w1_scr[...] = w1_stage[slot].astype(BF16)
            w2_scr[...] = w2_stage[slot].astype(BF16)

        xb = x_ref[...].reshape(MOE_TILE, D_MODEL).astype(BF16)
        y = b2_ref[...]
        for j in range(D_FF // FF_CHUNK):
            cols = slice(j * FF_CHUNK, (j + 1) * FF_CHUNK)
            ucols = slice(D_FF + j * FF_CHUNK, D_FF + (j + 1) * FF_CHUNK)
            g = jnp.minimum(_dot(xb, w1_scr[:, cols]) + b1_ref[:, cols], SWIGLU_LIMIT)
            u = jnp.clip(_dot(xb, w1_scr[:, ucols]) + b1_ref[:, ucols], -SWIGLU_LIMIT, SWIGLU_LIMIT)
            act = (u + 1.0) * (g * jax.nn.sigmoid(SWIGLU_ALPHA * g))
            y = y + _dot(act.astype(BF16), w2_scr[cols, :])
        y_ref[...] = y.reshape(y_ref.shape)

    @pl.when(i >= nv_ref[0])
    def _():
        y_ref[...] = jnp.zeros(y_ref.shape, F32)


def moe_experts(xs, tile_expert, tile_block, n_valid, tiles_per, w1, b1, w2, b2, layer):
    n_tiles = tile_expert.shape[0]
    tm = MOE_TILE
    used = tiles_per > 0
    ordinal_of = jnp.cumsum(used.astype(I32)) - 1
    experts = jnp.arange(N_EXPERTS, dtype=I32)
    seq = jnp.sum(jnp.where(used[None, :] & (ordinal_of[None, :] == experts[:, None]), experts[None, :], 0), axis=1)
    tile_ordinal = jnp.sum(jnp.where(tile_expert[:, None] == experts[None, :], ordinal_of[None, :], 0), axis=1)
    n_used = jnp.sum(used.astype(I32)).reshape(1)
    bias_map = lambda i, te, *_: (layer, te[i], 0, 0)
    return pl.pallas_call(
        functools.partial(_experts_kernel, layer=layer),
        grid_spec=pltpu.PrefetchScalarGridSpec(
            num_scalar_prefetch=6,
            grid=(n_tiles,),
            in_specs=[
                pl.BlockSpec((tm,) + ROW_AS_TILE, lambda i, te, tb, *_: (tb[i], 0, 0)),
                pl.BlockSpec(memory_space=pl.ANY),
                pl.BlockSpec((None, None, 1, 2 * D_FF), bias_map),
                pl.BlockSpec(memory_space=pl.ANY),
                pl.BlockSpec((None, None, 1, D_MODEL), bias_map),
            ],
            out_specs=pl.BlockSpec((tm,) + ROW_AS_TILE, lambda i, *_: (i, 0, 0)),
            scratch_shapes=[pltpu.VMEM((D_MODEL, 2 * D_FF), BF16), pltpu.VMEM((D_FF, D_MODEL), BF16),
                            pltpu.VMEM((2, D_MODEL, 2 * D_FF), F32), pltpu.VMEM((2, D_FF, D_MODEL), F32),
                            pltpu.SemaphoreType.DMA((2,))],
        ),
        out_shape=jax.ShapeDtypeStruct(xs.shape, F32),
        compiler_params=_cparams("arbitrary"),
        name="moe_experts",
    )(tile_expert, tile_block, n_valid, tile_ordinal, seq, n_used, xs, w1,
      b1.reshape(DEPTH, N_EXPERTS, 1, 2 * D_FF), w2, b2.reshape(DEPTH, N_EXPERTS, 1, D_MODEL))


def moe_layer(groups, xs_in, mods, layer, w_router, b_router, w1, b1, w2, b2):
    n_total = sum(g.N for g in groups)
    n_pairs = n_total * TOP_K
    n_tiles = -(-n_pairs // MOE_TILE) + N_EXPERTS
    w_r = jnp.pad(w_router[layer], ((0, 0), (0, LANES - N_EXPERTS)))
    w_hi = w_r.astype(BF16)
    w_r = jnp.concatenate([w_hi, (w_r - w_hi.astype(F32)).astype(BF16)], axis=1)
    b_r =jnp.pad(b_router[layer], (0, LANES - N_EXPERTS)).reshape(1, LANES)
    cnt = jnp.zeros((1, LANES), F32)
    routed = []
    for grp, x, mod in zip(groups, xs_in, mods):
        h, route_i, route_g, cnt = moe_router(grp, x, mod, w_r, b_r, cnt)
        routed.append((h, route_i, route_g))
    counts = cnt[0, :N_EXPERTS].astype(I32)
    tiles_per = (counts + MOE_TILE - 1) // MOE_TILE
    tile_end = jnp.cumsum(tiles_per)
    tile_start = tile_end - tiles_per
    n_valid = tile_end[-1:]
    tidx = jnp.minimum(jnp.arange(n_tiles, dtype=I32), n_valid - 1)
    tile_expert = jnp.sum((tile_end[None, :] <= tidx[:, None]).astype(I32), axis=1)
    row_off = tile_start * MOE_TILE
    positions = []
    for _, route_i, _ in routed:
        route_e, rank = route_i[:, :TOP_K], route_i[:, TOP_K:2 * TOP_K]
        onehot = route_e[:, :, None] == jnp.arange(N_EXPERTS, dtype=I32)[None, None, :]
        positions.append((jnp.sum(jnp.where(onehot, row_off[None, None, :], 0), axis=2) + rank).reshape(-1))
    xs = moe_dispatch(groups, [h for h, _, _ in routed], jnp.concatenate(positions), counts, row_off, n_valid,
                      n_tiles)
    ys = moe_experts(xs, tile_expert, tidx, n_valid, tiles_per, w1, b1, w2, b2, layer)
    return [moe_combine(grp, ys, pos, route_g, x, mod)
            for grp, x, mod, pos, (_, _, route_g) in zip(groups, xs_in, mods, positions, routed)]


def _seg_cumsum(x, seg, reverse=False):
    rows = x.shape[0]
    row = lax.broadcasted_iota(I32, x.shape, 0) & (seg - 1)
    sh = 1
    while sh < seg:
        if reverse:
            x = x + jnp.where(row + sh < seg, pltpu.roll(x, rows - sh, axis=0), 0.0)
        else:
            x = x + jnp.where(row >= sh, pltpu.roll(x, sh, axis=0), 0.0)
        sh *= 2
    return x


def _segment_mask(rows, seg):
    ti = lax.broadcasted_iota(I32, (rows, rows), 0)
    si = lax.broadcasted_iota(I32, (rows, rows), 1)
    mask = si <= ti
    if seg < rows:
        shift = seg.bit_length() - 1
        mask = mask & ((ti >> shift) == (si >> shift))
    return mask


def _causal_conv_silu(x, xp_scr, w_ref, b_ref, n_seg, seg, carry):
    rows, ch = x.shape
    xp_scr[:, SUBLANES:, :] = x.reshape(n_seg, seg, ch)
    acc = b_ref[...]
    for i in range(CONV_W):
        back = CONV_W - 1 - i
        xs = x if back == 0 else xp_scr[:, SUBLANES - back:SUBLANES - back + seg, :].reshape(rows, ch)
        acc = acc + xs * w_ref[i:i + 1, :]
    if carry:
        xp_scr[:, SUBLANES - (CONV_W - 1):SUBLANES, :] = xp_scr[:, SUBLANES + seg - (CONV_W - 1):SUBLANES + seg, :]
    return _silu(acc)


def _pad_history(buf):
    return jnp.pad(buf, ((0, 0), (SUBLANES - (CONV_W - 1), 0), (0, 0)))


def _mlstm_kernel(p_ref, gi_ref, gf_ref, hist_ref, c0_ref, n0_ref, m0_ref, cw_ref, cb_ref, gbi_ref, gbf_ref,
                  ng_ref, x_ref, gate_ref, wo_ref, y_ref, c1_ref, n1_ref, m1_ref, xp_scr, *state_scr,
                  n_seg, seg, n_chunks):
    rows = n_seg * seg
    chunk = pl.program_id(1)
    carried = n_chunks > 1
    if carried:
        c_st, n_st, m_st = state_scr

        @pl.when(chunk == 0)
        def _():
            c_st[...] = c0_ref[...]
            n_st[...] = n0_ref[...]
            m_st[...] = m0_ref[...]
            xp_scr[:, 0:SUBLANES, :] = hist_ref[...]

        c_in, n_in, m_in, c_out, n_out, m_out = c_st, n_st, m_st, c_st, n_st, m_st
    else:
        xp_scr[:, 0:SUBLANES, :] = hist_ref[...]
        c_in, n_in, m_in, c_out, n_out, m_out = c0_ref, n0_ref, m0_ref, c1_ref, n1_ref, m1_ref

    qk = _causal_conv_silu(p_ref[:, :2 * A_QK], xp_scr, cw_ref, cb_ref, n_seg, seg, carried)
    log_i = gi_ref[...] + gbi_ref[...]
    f_pre = gf_ref[...] + gbf_ref[...]
    log_f = jnp.minimum(f_pre, 0.0) - jnp.log1p(jnp.exp(-jnp.abs(f_pre)))
    a = _seg_cumsum(log_f, seg)
    a_rev = _seg_cumsum(log_f, seg, reverse=True)
    m0 = m_in[...]
    inter3 = a.reshape(n_seg, seg, LANES) + m0
    inter = inter3.reshape(rows, LANES)
    src3 = (a_rev - log_f + log_i).reshape(n_seg, seg, LANES)
    inter_end = inter3[:, seg - 1:seg, :]
    m_end = jnp.maximum(inter_end, jnp.max(src3, axis=1, keepdims=True))
    w_old = jnp.exp(inter_end - m_end)
    w_src = jnp.exp(src3 - m_end).reshape(rows, LANES)
    b_t = (a - log_i).T
    mask = _segment_mask(rows, seg)
    seg_of_lane = lax.broadcasted_iota(I32, (A_DK, rows), 1) >> (seg.bit_length() - 1)

    out = None
    for h in range(A_HEADS):
        q = qk[:, h * A_DK:(h + 1) * A_DK] * (A_DK ** -0.5)
        k = qk[:, A_QK + h * A_DK:A_QK + (h + 1) * A_DK]
        vb = p_ref[:, 2 * A_QK + h * A_DV:2 * A_QK + (h + 1) * A_DV].astype(BF16)
        qb = q.astype(BF16)
        dmat = jnp.where(mask, a[:, h:h + 1] - b_t[h:h + 1, :], NEG_INF)
        m_tok = jnp.maximum(inter[:, h:h + 1], jnp.max(dmat, axis=1, keepdims=True))
        s = _dot_nt(qb, k.astype(BF16)) * jnp.exp(dmat - m_tok)
        w_inter = jnp.exp(inter[:, h:h + 1] - m_tok)
        if n_seg == 1:
            q_c = _dot(qb, c_in[0, h].astype(BF16))
        else:
            q_c = jnp.concatenate([_dot(q[g * seg:(g + 1) * seg], c_in[g, h]) for g in range(n_seg)], axis=0)
        n3 = n_in[:, h:h + 1, :]
        n_tok = jnp.broadcast_to(n3, (n_seg, seg, A_DK)).reshape(rows, A_DK)
        num = _dot(s.astype(BF16), vb) + w_inter * q_c
        den = jnp.sum(s, axis=1, keepdims=True) + w_inter * jnp.sum(q * n_tok, axis=1, keepdims=True)
        hv = num / jnp.maximum(jnp.abs(den), jnp.exp(-m_tok))
        hn = hv * lax.rsqrt(jnp.mean(hv * hv, axis=1, keepdims=True) + EPS) * ng_ref[h:h + 1, :]
        o = jax.nn.sigmoid(p_ref[:, 2 * A_QK + A_V + h * A_DV:2 * A_QK + A_V + (h + 1) * A_DV])
        head_out = _dot((o * hn).astype(BF16), wo_ref[h * A_DV:(h + 1) * A_DV, :])
        out = head_out if out is None else out + head_out
        kw = k * w_src[:, h:h + 1]
        kw_t = kw.T
        for g in range(n_seg):
            kg = kw_t if n_seg == 1 else jnp.where(seg_of_lane == g, kw_t, 0.0)
            c_out[g, h] = w_old[g, :, h:h + 1] * c_in[g, h] + _dot(kg.astype(BF16), vb)
        n_out[:, h:h + 1, :] = (w_old[:, :, h:h + 1] * n3
                                + jnp.sum(kw.reshape(n_seg, seg, A_DK), axis=1, keepdims=True))
    m_out[...] = m_end
    y_ref[...] = x_ref[...] + gate_ref[...] * out

    if carried:
        @pl.when(chunk == n_chunks - 1)
        def _():
            c1_ref[...] = c_st[...]
            n1_ref[...] = n_st[...]
            m1_ref[...] = m_st[...]


def _lane_pad(w, cols=LANES):
    return jnp.pad(w, ((0, 0), (0, cols - w.shape[1])))


def mlstm_mixer(grp, x, mod, state, w_in, conv_w, conv_b, gate_b, norm_g, w_out):
    (st_c, c_layer), st_n, st_m, st_conv = state
    n_main = 2 * A_QK + 2 * A_V
    wb = w_in.astype(BF16)
    p_main, gate_i, gate_f = norm_matmul(
        grp, x, mod, 1, 0,
        [wb[:, :n_main], _lane_pad(wb[:, n_main:n_main + A_HEADS]), _lane_pad(wb[:, n_main + A_HEADS:])])
    n_seg, seg, n_chunks, rows = grp.S, grp.L, grp.n_chunks, grp.R
    row_map = lambda b, c: (b * n_chunks + c, 0)
    scratch = [pltpu.VMEM((n_seg, SUBLANES + seg, 2 * A_QK), F32)]
    if n_chunks > 1:
        scratch += [pltpu.VMEM((n_seg, A_HEADS, A_DK, A_DV), F32), pltpu.VMEM((n_seg, A_HEADS, A_DK), F32),
                    pltpu.VMEM((n_seg, 1, LANES), F32)]
    full = lambda shape: pl.BlockSpec(shape, lambda b, c: (0,) * len(shape))
    x_new, c1, n1, m1 = pl.pallas_call(
        functools.partial(_mlstm_kernel, n_seg=n_seg, seg=seg, n_chunks=n_chunks),
        grid=(grp.n_blocks, n_chunks),
        in_specs=[
            pl.BlockSpec((rows, n_main), row_map),
            pl.BlockSpec((rows, LANES), row_map),
            pl.BlockSpec((rows, LANES), row_map),
            pl.BlockSpec((n_seg, SUBLANES, 2 * A_QK), lambda b, c: (b, 0, 0)),
            pl.BlockSpec((None, n_seg, A_HEADS, A_DK, A_DV), lambda b, c: (c_layer, b, 0, 0, 0)),
            pl.BlockSpec((n_seg, A_HEADS, A_DK), lambda b, c: (b, 0, 0)),
            pl.BlockSpec((n_seg, 1, LANES), lambda b, c: (b, 0, 0)),
            full((CONV_W, 2 * A_QK)), full((1, 2 * A_QK)), full((1, LANES)), full((1, LANES)),
            full((A_HEADS, A_DV)),
            pl.BlockSpec((rows, D_MODEL), row_map),
            grp.mod_spec(2, rows, lambda b, c: b * n_chunks + c),
            full((A_V, D_MODEL)),
        ],
        out_specs=[
            pl.BlockSpec((rows, D_MODEL), row_map),
            pl.BlockSpec((n_seg, A_HEADS, A_DK, A_DV), lambda b, c: (b, 0, 0, 0)),
            pl.BlockSpec((n_seg, A_HEADS, A_DK), lambda b, c: (b, 0, 0)),
            pl.BlockSpec((n_seg, 1, LANES), lambda b, c: (b, 0, 0)),
        ],
        out_shape=[
            jax.ShapeDtypeStruct((grp.N, D_MODEL), F32),
            jax.ShapeDtypeStruct(st_c.shape[1:], F32),
            jax.ShapeDtypeStruct(st_n.shape, F32),
            jax.ShapeDtypeStruct((grp.B, 1, LANES), F32),
        ],
        scratch_shapes=scratch,
        compiler_params=_cparams("arbitrary", "arbitrary"),
        name="mlstm_chunk",
    )(p_main, gate_i, gate_f, _pad_history(st_conv), st_c, st_n, _lane_pad(st_m)[:, None, :],
      conv_w, conv_b.reshape(1, -1), _lane_pad(gate_b[None, :A_HEADS]), _lane_pad(gate_b[None, A_HEADS:]), norm_g,
      x, mod, w_out.astype(BF16))
    conv_new = p_main.reshape(grp.B, grp.T, n_main)[:, grp.T - (CONV_W - 1):, :2 * A_QK]
    return x_new, (c1, n1, m1[:, 0, :A_HEADS], conv_new)


def _softplus(x):
    return jnp.maximum(x, 0.0) + jnp.log1p(jnp.exp(-jnp.abs(x)))


def _ssd_kernel(z_ref, x_ref, b_ref, c_ref, dt_ref, hx_ref, hb_ref, hc_ref, h0_ref,
                cwx_ref, cbx_ref, cwb_ref, cbb_ref, cwc_ref, cbc_ref, dtb_ref, alog_ref, dskip_ref, ng_ref,
                y_ref, h1_ref, xpx, xpb, xpc, *state_scr, n_seg, seg, n_chunks):
    rows = n_seg * seg
    chunk = pl.program_id(2)
    carried = n_chunks > 1

    def load_history():
        xpx[:, 0:SUBLANES, :] = hx_ref[...]
        xpb[:, 0:SUBLANES, :] = hb_ref[...]
        xpc[:, 0:SUBLANES, :] = hc_ref[...]

    if carried:
        (h_st,) = state_scr

        @pl.when(chunk == 0)
        def _():
            h_st[...] = h0_ref[...].reshape(n_seg, B_GW, B_STATE)
            load_history()

        get_h = lambda g: h_st[g]

        def set_h(g, val):
            h_st[g] = val
    else:
        load_history()
        get_h = lambda g: h0_ref[g].reshape(B_GW, B_STATE)

        def set_h(g, val):
            h1_ref[g] = val.reshape(B_HPG, B_HEADDIM, B_STATE)

    x = _causal_conv_silu(x_ref[...], xpx, cwx_ref, cbx_ref, n_seg, seg, carried)
    bm = _causal_conv_silu(b_ref[...], xpb, cwb_ref, cbb_ref, n_seg, seg, carried)
    cm = _causal_conv_silu(c_ref[...], xpc, cwc_ref, cbc_ref, n_seg, seg, carried)
    dt = _softplus(dt_ref[...] + dtb_ref[...])
    da = dt * (-jnp.exp(alog_ref[...]))
    a = _seg_cumsum(da, seg)
    w_end = jnp.exp(_seg_cumsum(da, seg, reverse=True) - da)
    a_t = a.T
    a_end = a.reshape(n_seg, seg, LANES)[:, seg - 1:seg, :]
    mask = _segment_mask(rows, seg)
    head_of_lane = lax.broadcasted_iota(I32, (rows, B_GW), 1) >> 6
    head_of_row = lax.broadcasted_iota(I32, (B_GW, B_STATE), 0) >> 6
    shift = seg.bit_length() - 1
    seg_of_lane = lax.broadcasted_iota(I32, (B_GW, rows), 1) >> shift

    def per_head(cols, selector):
        out = cols[B_HPG - 1]
        for e in range(B_HPG - 2, -1, -1):
            out = jnp.where(selector == e, cols[e], out)
        return out

    xdt = x * per_head([dt[:, e:e + 1] for e in range(B_HPG)], head_of_lane)
    xdt_b = xdt.astype(BF16)
    bm_b = bm.astype(BF16)
    cb = _dot_nt(cm.astype(BF16), bm_b)
    y = None
    for e in range(B_HPG):
        decay = jnp.exp(jnp.where(mask, a[:, e:e + 1] - a_t[e:e + 1, :], NEG_INF))
        ye = _dot((cb * decay).astype(BF16), xdt_b)
        y = ye if y is None else jnp.where(head_of_lane == e, ye, y)
    if n_seg == 1:
        y_state = _dot_nt(cm.astype(BF16), get_h(0).astype(BF16))
    else:
        y_state = jnp.concatenate([_dot_nt(cm[g * seg:(g + 1) * seg], get_h(g)) for g in range(n_seg)], axis=0)
    y = y + per_head([jnp.exp(a[:, e:e + 1]) for e in range(B_HPG)], head_of_lane) * y_state
    y = (y + dskip_ref[...] * x) * _silu(z_ref[...])
    y_ref[...] = y * lax.rsqrt(jnp.mean(y * y, axis=1, keepdims=True) + EPS) * ng_ref[...]
    xw_t = (xdt * per_head([w_end[:, e:e + 1] for e in range(B_HPG)], head_of_lane)).T
    for g in range(n_seg):
        xg = xw_t if n_seg == 1 else jnp.where(seg_of_lane == g, xw_t, 0.0)
        keep = per_head([jnp.exp(a_end[g][:, e:e + 1]) for e in range(B_HPG)], head_of_row)
        set_h(g, keep * get_h(g) + _dot(xg.astype(BF16), bm_b))

    if carried:
        @pl.when(chunk == n_chunks - 1)
        def _():
            h1_ref[...] = h_st[...].reshape(n_seg, B_HPG, B_HEADDIM, B_STATE)


def ssd_mixer(grp, x, mod, state, w_in, conv_w, conv_b, dt_bias, a_log, d_skip, norm_g, w_out):
    st_h, st_conv = state
    n_main = B_INNER + B_CONV_DIM
    wb = w_in.astype(BF16)
    group_lanes = lambda v: jnp.pad(v.reshape(-1, B_GROUPS, B_HPG), ((0, 0), (0, 0), (0, LANES - B_HPG))).reshape(
        -1, B_GROUPS * LANES)
    p_main, dt_raw = norm_matmul(grp, x, mod, 1, 0, [wb[:, :n_main], group_lanes(wb[:, n_main:])])
    n_seg, seg, n_chunks, rows = grp.S, grp.L, grp.n_chunks, grp.R
    xo, bo, co = B_INNER // B_GW, (2 * B_INNER) // B_STATE, (2 * B_INNER + B_GN) // B_STATE
    cxo, cbo, cco = 0, B_INNER // B_STATE, (B_INNER + B_GN) // B_STATE
    row = lambda b, g, c: b * n_chunks + c
    hist = _pad_history(st_conv)
    scratch = [pltpu.VMEM((n_seg, SUBLANES + seg, B_GW), F32), pltpu.VMEM((n_seg, SUBLANES + seg, B_STATE), F32),
               pltpu.VMEM((n_seg, SUBLANES + seg, B_STATE), F32)]
    if n_chunks > 1:
        scratch.append(pltpu.VMEM((n_seg, B_GW, B_STATE), F32))
    state_spec = pl.BlockSpec((n_seg, B_HPG, B_HEADDIM, B_STATE), lambda b, g, c: (b, g, 0, 0))
    y, h1 = pl.pallas_call(
        functools.partial(_ssd_kernel, n_seg=n_seg, seg=seg, n_chunks=n_chunks),
        grid=(grp.n_blocks, B_GROUPS, n_chunks),
        in_specs=[
            pl.BlockSpec((rows, B_GW), lambda b, g, c: (row(b, g, c), g)),
            pl.BlockSpec((rows, B_GW), lambda b, g, c: (row(b, g, c), xo + g)),
            pl.BlockSpec((rows, B_STATE), lambda b, g, c: (row(b, g, c), bo + g)),
            pl.BlockSpec((rows, B_STATE), lambda b, g, c: (row(b, g, c), co + g)),
            pl.BlockSpec((rows, LANES), lambda b, g, c: (row(b, g, c), g)),
            pl.BlockSpec((n_seg, SUBLANES, B_GW), lambda b, g, c: (b, 0, cxo + g)),
            pl.BlockSpec((n_seg, SUBLANES, B_STATE), lambda b, g, c: (b, 0, cbo + g)),
            pl.BlockSpec((n_seg, SUBLANES, B_STATE), lambda b, g, c: (b, 0, cco + g)),
            state_spec,
            pl.BlockSpec((CONV_W, B_GW), lambda b, g, c: (0, cxo + g)),
            pl.BlockSpec((1, B_GW), lambda b, g, c: (0, cxo + g)),
            pl.BlockSpec((CONV_W, B_STATE), lambda b, g, c: (0, cbo + g)),
            pl.BlockSpec((1, B_STATE), lambda b, g, c: (0, cbo + g)),
            pl.BlockSpec((CONV_W, B_STATE), lambda b, g, c: (0, cco + g)),
            pl.BlockSpec((1, B_STATE), lambda b, g, c: (0, cco + g)),
            pl.BlockSpec((1, LANES), lambda b, g, c: (0, g)),
            pl.BlockSpec((1, LANES), lambda b, g, c: (0, g)),
            pl.BlockSpec((1, B_GW), lambda b, g, c: (0, g)),
            pl.BlockSpec((1, B_GW), lambda b, g, c: (0, g)),
        ],
        out_specs=[pl.BlockSpec((rows, B_GW), lambda b, g, c: (row(b, g, c), g)), state_spec],
        out_shape=[jax.ShapeDtypeStruct((grp.N, B_INNER), F32), jax.ShapeDtypeStruct(st_h.shape, F32)],
        scratch_shapes=scratch,
        compiler_params=_cparams("arbitrary", "arbitrary", "arbitrary"),
        name="ssd_chunk",
    )(p_main, p_main, p_main, p_main, dt_raw, hist, hist, hist, st_h,
      conv_w, conv_b.reshape(1, -1), conv_w, conv_b.reshape(1, -1), conv_w, conv_b.reshape(1, -1),
      group_lanes(dt_bias[None, :]), group_lanes(a_log[None, :]),
      jnp.repeat(d_skip, B_HEADDIM)[None, :], norm_g[None, :])
    x_new = matmul_residual(grp, y, w_out.astype(BF16), x, mod, 2)
    conv_new = p_main.reshape(grp.B, grp.T, n_main)[:, grp.T - (CONV_W - 1):, B_INNER:]
    return x_new, (h1, conv_new)


C_Q = C_HEADS * C_HD
C_KV = C_KV_HEADS * C_HD


def _rope_kernel(p_ref, cos_ref, sin_ref, qg_ref, kg_ref, q_ref, k_ref):
    tm = p_ref.shape[0]
    lane = lax.broadcasted_iota(I32, (tm, LANES), 1)
    low_head = lane < C_HD
    first_half = (lane & (C_HD - 1)) < C_HD // 2
    cos, sin = cos_ref[...], sin_ref[...]

    def norm_rope(xb, gain):
        sq = xb * xb
        s_lo = jnp.sum(jnp.where(low_head, sq, 0.0), axis=1, keepdims=True)
        s_hi = jnp.sum(jnp.where(low_head, 0.0, sq), axis=1, keepdims=True)
        ms = jnp.where(low_head, s_lo, s_hi) * (1.0 / C_HD)
        xn = xb * lax.rsqrt(ms + EPS) * gain
        partner = jnp.where(first_half, pltpu.roll(xn, LANES - C_HD // 2, axis=1), pltpu.roll(xn, C_HD // 2, axis=1))
        return xn * cos + partner * sin

    for j in range(C_Q // LANES):
        q_ref[:, j * LANES:(j + 1) * LANES] = norm_rope(p_ref[:, j * LANES:(j + 1) * LANES], qg_ref[...])
    for j in range(C_KV // LANES):
        k_ref[:, j * LANES:(j + 1) * LANES] = norm_rope(p_ref[:, C_Q + j * LANES:C_Q + (j + 1) * LANES], kg_ref[...])


def _swa_kernel(q_ref, k0_ref, k1_ref, v0_ref, v1_ref, sink_ref, o_ref, *, n_units, tq, blocks_per_seq):
    i = pl.program_id(0)
    rows = C_GROUP * tq
    t = lax.broadcasted_iota(I32, (rows, 2 * WINDOW), 0) & (tq - 1)
    s = lax.broadcasted_iota(I32, (rows, 2 * WINDOW), 1)
    valid = (s >= t) & (s <= t + WINDOW)
    if blocks_per_seq:
        valid = valid & (s >= jnp.where(lax.rem(i, blocks_per_seq) == 0, WINDOW, 0))

    def with_past(past, new):
        if tq < WINDOW:
            new = jnp.concatenate([new, jnp.zeros((WINDOW - tq, new.shape[1]), F32)], axis=0)
        return jnp.concatenate([past, new], axis=0)

    for u in range(n_units):
        kk = with_past(k0_ref[u], k1_ref[u])
        vv = with_past(v0_ref[u], v1_ref[u])
        qu = q_ref[u]
        outs = []
        for kh in range(C_KV_HEADS):
            kc = kk[:, kh * C_HD:(kh + 1) * C_HD].astype(BF16)
            vc = vv[:, kh * C_HD:(kh + 1) * C_HD].astype(BF16)
            qs = jnp.concatenate([qu[:, (kh * C_GROUP + j) * C_HD:(kh * C_GROUP + j + 1) * C_HD]
                                  for j in range(C_GROUP)], axis=0).astype(BF16)
            logits = jnp.where(valid, _dot_nt(qs, kc) * (C_HD ** -0.5), NEG_INF)
            sink = sink_ref[kh][:, 0:1]
            mx = jnp.maximum(jnp.max(logits, axis=1, keepdims=True), sink)
            pr = jnp.exp(logits - mx)
            den = jnp.sum(pr, axis=1, keepdims=True) + jnp.exp(sink - mx)
            o = _dot(pr.astype(BF16), vc) / den
            outs += [o[j * tq:(j + 1) * tq] for j in range(C_GROUP)]
        o_ref[u] = jnp.concatenate(outs, axis=1)


def swa_mixer(grp, x, mod, cache, pos0, w_in, q_g, k_g, sinks, w_out):
    (p,) = norm_matmul(grp, x, mod, 1, 0, [w_in.astype(BF16)])
    half = C_HD // 2
    inv = ROPE_THETA ** (-jnp.arange(half, dtype=F32) / half)
    ang = (pos0 + jnp.arange(grp.T)).astype(F32)[:, None] * inv[None, :]
    cos, sin = jnp.cos(ang), jnp.sin(ang)
    per_token = lambda a: jnp.tile(a, (grp.B, 1))
    cos_t = per_token(jnp.tile(cos, (1, LANES // half)))
    sin_t = per_token(jnp.tile(jnp.concatenate([-sin, sin], axis=1), (1, LANES // C_HD)))
    tm = grp.tm
    gains = lambda g: jnp.tile(g, LANES // C_HD)[None, :]
    qr, kr = pl.pallas_call(
        _rope_kernel,
        grid=(grp.N // tm,),
        in_specs=[
            pl.BlockSpec((tm, C_Q + 2 * C_KV), lambda i: (i, 0)),
            pl.BlockSpec((tm, LANES), lambda i: (i, 0)),
            pl.BlockSpec((tm, LANES), lambda i: (i, 0)),
            pl.BlockSpec((1, LANES), lambda i: (0, 0)),
            pl.BlockSpec((1, LANES), lambda i: (0, 0)),
        ],
        out_specs=[pl.BlockSpec((tm, C_Q), lambda i: (i, 0)), pl.BlockSpec((tm, C_KV), lambda i: (i, 0))],
        out_shape=[jax.ShapeDtypeStruct((grp.N, C_Q), F32), jax.ShapeDtypeStruct((grp.N, C_KV), F32)],
        compiler_params=_cparams("arbitrary"),
        name="qk_norm_rope",
    )(p, cos_t, sin_t, gains(q_g), gains(k_g))

    v_col = (C_Q + C_KV) // C_KV
    if cache is None:
        assert grp.T % WINDOW == 0
        tq, n_units, blocks_per_seq = WINDOW, 1, grp.T // WINDOW
        n_steps = grp.N // tq
        k3 = kr.reshape(n_steps, tq, C_KV)
        p3 = p.reshape(n_steps, tq, C_Q + 2 * C_KV)
        k_args = (k3, k3, p3, p3)
        prev = lambda i: (jnp.maximum(i - 1, 0), 0, 0)
        k_specs = [
            pl.BlockSpec((1, tq, C_KV), prev),
            pl.BlockSpec((1, tq, C_KV), lambda i: (i, 0, 0)),
            pl.BlockSpec((1, tq, C_KV), lambda i: (jnp.maximum(i - 1, 0), 0, v_col)),
            pl.BlockSpec((1, tq, C_KV), lambda i: (i, 0, v_col)),
        ]
    else:
        tq, n_units, blocks_per_seq = grp.T, SUBLANES, 0
        n_steps = grp.B // n_units
        k_args = (cache[0].reshape(grp.B, WINDOW, C_KV), kr.reshape(grp.B, tq, C_KV),
                  cache[1].reshape(grp.B, WINDOW, C_KV), p.reshape(grp.B, tq, C_Q + 2 * C_KV))
        k_specs = [
            pl.BlockSpec((n_units, WINDOW, C_KV), lambda i: (i, 0, 0)),
            pl.BlockSpec((n_units, tq, C_KV), lambda i: (i, 0, 0)),
            pl.BlockSpec((n_units, WINDOW, C_KV), lambda i: (i, 0, 0)),
            pl.BlockSpec((n_units, tq, C_KV), lambda i: (i, 0, v_col)),
        ]
    sink_rows = jnp.broadcast_to(sinks.reshape(C_KV_HEADS, C_GROUP, 1, 1),
                                 (C_KV_HEADS, C_GROUP, tq, LANES)).reshape(C_KV_HEADS, C_GROUP * tq, LANES)
    o = pl.pallas_call(
        functools.partial(_swa_kernel, n_units=n_units, tq=tq, blocks_per_seq=blocks_per_seq),
        grid=(n_steps,),
        in_specs=[pl.BlockSpec((n_units, tq, C_Q), lambda i: (i, 0, 0))] + k_specs
        + [pl.BlockSpec(sink_rows.shape, lambda i: (0, 0, 0))],
        out_specs=pl.BlockSpec((n_units, tq, C_Q), lambda i: (i, 0, 0)),
        out_shape=jax.ShapeDtypeStruct((grp.N // tq, tq, C_Q), F32),
        compiler_params=_cparams("arbitrary"),
        name="swa_attention",
    )(qr.reshape(grp.N // tq, tq, C_Q), *k_args, sink_rows)
    x_new = matmul_residual(grp, o.reshape(grp.N, C_Q), w_out.astype(BF16), x, mod, 2)
    k_new = kr.reshape(grp.B, grp.T, C_KV_HEADS, C_HD)
    v_new = p[:, C_Q + C_KV:].reshape(grp.B, grp.T, C_KV_HEADS, C_HD)
    if cache is None:
        return x_new, (k_new[:, grp.T - WINDOW:], v_new[:, grp.T - WINDOW:])
    return x_new, (jnp.concatenate([cache[0][:, grp.T:], k_new], axis=1),
                   jnp.concatenate([cache[1][:, grp.T:], v_new], axis=1))


def kernel(x_prompt, x_sample, c_prompt, c_sample, state_mlstm_C, state_mlstm_n, state_mlstm_m, state_mlstm_conv,
           state_ssm, state_ssm_conv, cache_swa_k, cache_swa_v, ada_w, ada_b, mlstm_w_in, mlstm_conv_w,
           mlstm_conv_b, mlstm_gate_b, mlstm_norm_g, mlstm_w_out, ssd_w_in, ssd_conv_w, ssd_conv_b, ssd_dt_bias,
           ssd_a_log, ssd_d_skip, ssd_norm_g, ssd_w_out, swa_w_in, swa_q_norm_g, swa_k_norm_g, swa_sinks,
           swa_w_out, moe_w_router, moe_b_router, moe_w1, moe_b1, moe_w2, moe_b2):
    groups = [Group(*x_prompt.shape[:2]), Group(*x_sample.shape[:2])]
    n_prompt = groups[0].B
    xs = [x_prompt.reshape(-1, D_MODEL), x_sample.reshape(-1, D_MODEL)]
    mod_all = ada_modulation(jnp.concatenate([c_prompt, c_sample], axis=0), ada_w, ada_b)
    fresh = lambda s: jnp.zeros((n_prompt,) + s.shape[2:], F32)
    new = [[[] for _ in range(8)] for _ in groups]
    for layer in range(DEPTH):
        kind, j = layer % 3, layer // 3
        mods = [groups[0].expand_mod(mod_all[layer, :n_prompt]), groups[1].expand_mod(mod_all[layer, n_prompt:])]
        for gi, grp in enumerate(groups):
            if kind == 0:
                state = (state_mlstm_n, state_mlstm_m, state_mlstm_conv)
                state = tuple(s[j] if gi else fresh(s) for s in state)
                state = ((state_mlstm_C, j) if gi else (fresh(state_mlstm_C)[None], 0),) + state
                xs[gi], st = mlstm_mixer(grp, xs[gi], mods[gi], state, mlstm_w_in[j], mlstm_conv_w[j], mlstm_conv_b[j],
                                         mlstm_gate_b[j], mlstm_norm_g[j], mlstm_w_out[j])
                first = 0
            elif kind == 1:
                state = tuple(s[j] if gi else fresh(s) for s in (state_ssm, state_ssm_conv))
                xs[gi], st = ssd_mixer(grp, xs[gi], mods[gi], state, ssd_w_in[j], ssd_conv_w[j], ssd_conv_b[j],
                                       ssd_dt_bias[j], ssd_a_log[j], ssd_d_skip[j], ssd_norm_g[j], ssd_w_out[j])
                first = 4
            else:
                cache = (cache_swa_k[j], cache_swa_v[j]) if gi else None
                xs[gi], st = swa_mixer(grp, xs[gi], mods[gi], cache, PAST_LEN if gi else 0, swa_w_in[j],
                                       swa_q_norm_g[j], swa_k_norm_g[j], swa_sinks[j], swa_w_out[j])
                first = 6
            for offset, s in enumerate(st):
                new[gi][first + offset].append(s)
        xs = moe_layer(groups, xs, mods, layer, moe_w_router, moe_b_router, moe_w1, moe_b1, moe_w2, moe_b2)
    outs = [xs[0].reshape(x_prompt.shape), xs[1].reshape(x_sample.shape)]
    for slot in range(8):
        outs += [jnp.stack(new[0][slot]), jnp.stack(new[1][slot])]
    return tuple(outs)
```
